```python
import math
import jax
import jax.numpy as jnp
from jax import lax
import numpy as np

D_MODEL = 1024
BATCH = 2
SEQ = 8192
DEPTH = 4

GRID_W = 64
CTX_LEN = 256
N_MOD = 9
N_NORMS = 6
D_FF = 2816
MACARON_W = 0.5
NORM_EPS = 1e-6
NEG_INF = -1e30

HY_CH = 256
HY_ORDER = 2
HY_EMB = 33
HY_FILT = 64
HY_SHORT = 3
HY_TARGET = 1e-2
HY_FAST = 0.3
HY_SLOW = 1.5

RW_HEADS = 6
RW_HD = 64
RW_W = RW_HEADS * RW_HD
RW_DECAY_LORA = 64
RW_AAA_LORA = 64
RW_GATE_LORA = 128
RW_GN_EPS = 64e-5
ROPE_BASE = 10000.0

NA_HEADS = 6
NA_HD = 64
NA_W = NA_HEADS * NA_HD
WIN_ROWS = 8
WIN_COLS = 16
COL_BLOCK = 16
COL_BAND = 32

MIX_W = HY_CH + RW_W + NA_W
HY_IN = (HY_ORDER + 1) * HY_CH
RW_IN = 3 * RW_W + 2 * RW_DECAY_LORA + 2 * RW_AAA_LORA + RW_GATE_LORA
NA_IN = 3 * NA_W
IN_W = HY_IN + RW_IN + NA_IN

kernel_name = 'hybrid_hyena_rwkv7_natten_dit_trunk'


def rmsnorm(x, g):
    xf = x.astype(jnp.float32)
    y = xf * lax.rsqrt(jnp.mean(xf * xf, axis=-1, keepdims=True) + NORM_EPS)
    return (y * g.astype(jnp.float32)).astype(x.dtype)


def modulate(x, shift, scale):
    return x * (1 + scale) + shift


def swiglu(u, w_gu, w_dn):
    gate, up = jnp.split(u @ w_gu, 2, axis=-1)
    return (jax.nn.silu(gate) * up) @ w_dn


def ffn_sublayer(h, shift, scale, gate, g_pre, g_post, w_gu, w_dn):
    u = modulate(rmsnorm(h, g_pre), shift, scale)
    return h + MACARON_W * gate * rmsnorm(swiglu(u, w_gu, w_dn), g_post)


def centred_depthwise_conv(x, w, b):
    k = w.shape[0]
    half = k // 2
    L = x.shape[1]
    xp = jnp.pad(x, ((0, 0), (half, half), (0, 0)))
    return sum(xp[:, j:j + L] * w[j] for j in range(k)) + b


def neighbour_tokens(x):
    xp = jnp.pad(x, ((0, 0), (1, 1), (0, 0)))
    return xp[:, :-2], xp[:, 2:]


def axial_rope_tables(L, hd):
    nf = hd // 4
    t = jnp.arange(L)
    row = (t // GRID_W).astype(jnp.float32)
    col = (t % GRID_W).astype(jnp.float32)
    inv = ROPE_BASE ** (-jnp.arange(nf, dtype=jnp.float32) / nf)
    ang_r = row[:, None] * inv
    ang_c = col[:, None] * inv
    ang = jnp.concatenate([ang_r, ang_r, ang_c, ang_c], axis=-1)
    return jnp.cos(ang), jnp.sin(ang)


def apply_rope(x, cos, sin):
    sh = x.shape
    L, hd = cos.shape
    xa = x.reshape(sh[:-1] + (2, 2, hd // 4))
    rot = jnp.stack([-xa[..., 1, :], xa[..., 0, :]], axis=-2).reshape(sh)
    bshape = (1, L) + (1,) * (x.ndim - 3) + (hd,)
    return (x * cos.reshape(bshape) + rot * sin.reshape(bshape)).astype(x.dtype)


def hyena_filters(L, w1, b1, w2, b2, w3, freq):
    f32 = jnp.float32
    t = jnp.linspace(0.0, 1.0, L, dtype=f32)[:, None]
    bands = (HY_EMB - 1) // 2
    ang = 2.0 * math.pi * jnp.arange(L, dtype=f32)[:, None] / L
    fr = jnp.linspace(1e-4, bands - 1, bands, dtype=f32)[None, :]
    z = jnp.concatenate([t, jnp.cos(fr * ang), -jnp.sin(fr * ang)], axis=-1)
    fq = freq.astype(f32)
    h = jnp.sin(fq * (z @ w1.astype(f32) + b1.astype(f32)))
    h = jnp.sin(fq * (h @ w2.astype(f32) + b2.astype(f32)))
    h = (h @ w3.astype(f32)).reshape(L, HY_ORDER, 2, HY_CH)
    deltas = jnp.abs(jnp.linspace(math.log(HY_TARGET) / HY_SLOW, math.log(HY_TARGET) / HY_FAST, HY_CH, dtype=f32))
    return h * jnp.exp(-t * deltas)[:, None, None, :]


def bidir_long_conv(u, h_fwd, h_bwd, bias):
    L = u.shape[1]
    n = 2 * L
    k = jnp.concatenate([h_fwd, jnp.zeros_like(h_fwd[:1]), h_bwd[:0:-1]], axis=0)
    uf = u.astype(jnp.float32)
    y = jnp.fft.irfft(jnp.fft.rfft(uf, n=n, axis=1) * jnp.fft.rfft(k, n=n, axis=0)[None], n=n, axis=1)[:, :L]
    return (y + uf * bias.astype(jnp.float32)).astype(u.dtype)


def hyena_mixer(p, conv_w, conv_b, filters, bias):
    v, *gates = jnp.split(centred_depthwise_conv(p, conv_w, conv_b), HY_ORDER + 1, axis=-1)
    z = v
    for o, gate in enumerate(gates):
        z = gate * bidir_long_conv(z, filters[:, o, 0], filters[:, o, 1], bias[o])
    return z


def l2_normalize(x):
    xf = x.astype(jnp.float32)
    n = jnp.sqrt(jnp.sum(xf * xf, axis=-1, keepdims=True))
    return (xf / jnp.maximum(n, 1e-12)).astype(x.dtype)


def rwkv_inputs(p, mu, w0, w2, a0, a2, g2, k_k, k_a, rope_cs):
    B, L, _ = p.shape
    prev, nxt = neighbour_tokens(p)
    xs = p + mu[0] * (prev - p) + mu[1] * (nxt - p)
    i1 = 3 * RW_W + 2 * RW_DECAY_LORA
    r, k, v, wd, ad, gd = jnp.split(xs, [RW_W, 2 * RW_W, 3 * RW_W, i1, i1 + 2 * RW_AAA_LORA], axis=-1)
    heads = lambda t: t.reshape(t.shape[:-1] + (RW_HEADS, RW_HD))
    g = jax.nn.sigmoid(gd) @ g2
    wd = jnp.tanh(wd.reshape(B, L, 2, RW_DECAY_LORA))
    logw = -jax.nn.softplus(-(w0 + jnp.einsum('bldr,drc->bldc', wd, w2))) - 0.5
    decay = jnp.exp(-jnp.exp(logw.astype(jnp.float32)))
    a = jax.nn.sigmoid(a0 + jnp.einsum('bldr,drc->bldc', ad.reshape(B, L, 2, RW_AAA_LORA), a2))
    kk = l2_normalize(heads(k * k_k))
    k_dir = heads(k[:, :, None] * (1 + (a - 1) * k_a))
    b = kk[:, :, None] * heads(a)
    r, v = heads(r), heads(v)
    r_s, kk_s, k_s, b_s = r, kk, k_dir, b
    if rope_cs is not None:
        r_s, kk_s, k_s, b_s = (apply_rope(t, rope_cs[0], rope_cs[1]) for t in (r, kk, k_dir, b))
    return (r_s, heads(decay), k_s, v, -kk_s, b_s), (r, k_dir, v, g)


def wkv7_scan(state0, scan_in, d, reverse, emit):
    r, decay, k, v, a, b = scan_in
    seq = tuple(jnp.moveaxis(t.astype(jnp.float32), 1, 0) for t in (r, decay[:, :, d], k[:, :, d], v, a, b[:, :, d]))

    def step(S, inp):
        r_t, w_t, k_t, v_t, a_t, b_t = inp
        sa = jnp.einsum('bhij,bhj->bhi', S, a_t)
        S = S * w_t[:, :, None, :] + sa[..., None] * b_t[:, :, None, :] + v_t[..., None] * k_t[:, :, None, :]
        return S, (jnp.einsum('bhij,bhj->bhi', S, r_t) if emit else None)

    S, ys = lax.scan(step, state0, seq, reverse=reverse)
    return S, (jnp.moveaxis(ys, 0, 1) if emit else None)


def rwkv_output(y_f, y_b, out_in, r_k, ln_w, ln_b):
    r, k_dir, v, g = out_in
    B, L = y_f.shape[:2]
    y = y_f + y_b
    mean = jnp.mean(y, axis=-1, keepdims=True)
    var = jnp.mean(jnp.square(y - mean), axis=-1, keepdims=True)
    y = ((y - mean) * lax.rsqrt(var + RW_GN_EPS)).reshape(B, L, RW_W) * ln_w + ln_b
    bonus = jnp.sum(r[:, :, None] * k_dir * r_k, axis=(2, 4))[..., None] * v
    return (y.astype(g.dtype) + bonus.reshape(B, L, RW_W)) * g


def natten_column_tables():
    ncb = GRID_W // COL_BLOCK
    qcols = np.arange(GRID_W).reshape(ncb, COL_BLOCK)
    win0 = np.clip(qcols - WIN_COLS // 2, 0, GRID_W - WIN_COLS)
    band0 = np.clip(np.arange(ncb) * COL_BLOCK - WIN_COLS // 2, 0, GRID_W - COL_BAND)
    band_cols = band0[:, None] + np.arange(COL_BAND)
    kc = band_cols[:, None, :]
    valid = (kc >= win0[..., None]) & (kc < win0[..., None] + WIN_COLS)
    col_off = np.clip(kc - qcols[..., None] + WIN_COLS - 1, 0, 2 * WIN_COLS - 2)
    return band_cols, valid, col_off


def natten_latent(q, k, v, k_ctx, v_ctx, rpb):
    B, L = q.shape[:2]
    rows = L // GRID_W
    kr = min(WIN_ROWS, rows)
    ncb = GRID_W // COL_BLOCK
    band_cols, valid, col_off = natten_column_tables()
    valid = jnp.asarray(valid)
    grid = lambda t: t.reshape(B, rows, GRID_W, NA_HEADS, NA_HD)
    qg, kg, vg = grid(q), grid(k), grid(v)
    scale = NA_HD ** -0.5
    n_loc = kr * COL_BAND

    def row_block(i):
        start = jnp.clip(i - WIN_ROWS // 2, 0, rows - kr)
        q_i = lax.dynamic_index_in_dim(qg, i, axis=1, keepdims=False).reshape(B, ncb, COL_BLOCK, NA_HEADS, NA_HD)
        k_i = jnp.take(lax.dynamic_slice_in_dim(kg, start, kr, axis=1), band_cols, axis=2)
        v_i = jnp.take(lax.dynamic_slice_in_dim(vg, start, kr, axis=1), band_cols, axis=2)
        row_off = start + jnp.arange(kr) - i + WIN_ROWS - 1
        bias = jnp.transpose(rpb[:, row_off][:, :, col_off], (0, 2, 3, 1, 4))
        s_loc = jnp.einsum('bnqhd,brnkhd->bhnqrk', q_i, k_i).astype(jnp.float32) * scale + bias.astype(jnp.float32)
        s_loc = jnp.where(valid[:, :, None, :], s_loc, NEG_INF)
        s_ctx = jnp.einsum('bnqhd,bchd->bhnqc', q_i, k_ctx).astype(jnp.float32) * scale
        s = jnp.concatenate([s_loc.reshape(B, NA_HEADS, ncb, COL_BLOCK, n_loc), s_ctx], axis=-1)
        p = jax.nn.softmax(s, axis=-1).astype(v.dtype)
        p_loc = p[..., :n_loc].reshape(B, NA_HEADS, ncb, COL_BLOCK, kr, COL_BAND)
        o = jnp.einsum('bhnqrk,brnkhd->bnqhd', p_loc, v_i) + jnp.einsum('bhnqc,bchd->bnqhd', p[..., n_loc:], v_ctx)
        return o.reshape(B, GRID_W, NA_W)

    out = lax.map(row_block, jnp.arange(rows))
    return jnp.moveaxis(out, 0, 1).reshape(B, L, NA_W)


def context_attention(q, k, v):
    B, Lc = q.shape[:2]
    s = jnp.einsum('bqhd,bkhd->bhqk', q, k).astype(jnp.float32) * NA_HD ** -0.5
    p = jax.nn.softmax(s, axis=-1).astype(v.dtype)
    return jnp.einsum('bhqk,bkhd->bqhd', p, v).reshape(B, Lc, NA_W)


def token_mixer(u_ctx, u_lat, rope_cs, w_in, w_out, hy, rw, na_rpb, ctx_out):
    split = lambda p: jnp.split(p, [HY_IN, HY_IN + RW_IN], axis=-1)
    hy_c, rw_c, na_c = split(u_ctx @ w_in)
    hy_l, rw_l, na_l = split(u_lat @ w_in)
    hy_conv_w, hy_conv_b, f_w1, f_b1, f_w2, f_b2, f_w3, f_freq, hy_bias = hy
    o_hy_l = hyena_mixer(hy_l, hy_conv_w, hy_conv_b, hyena_filters(u_lat.shape[1], f_w1, f_b1, f_w2, f_b2, f_w3, f_freq), hy_bias)
    mu, w0, w2, a0, a2, g2, k_k, k_a, r_k, ln_w, ln_b = rw
    scan_c, out_c = rwkv_inputs(rw_c, mu, w0, w2, a0, a2, g2, k_k, k_a, None)
    scan_l, out_l = rwkv_inputs(rw_l, mu, w0, w2, a0, a2, g2, k_k, k_a, rope_cs)
    s0 = jnp.zeros((u_lat.shape[0], RW_HEADS, RW_HD, RW_HD), jnp.float32)
    s_f, y_fc = wkv7_scan(s0, scan_c, 0, False, ctx_out)
    s_b, y_bc = wkv7_scan(s0, scan_c, 1, True, ctx_out)
    _, y_fl = wkv7_scan(s_f, scan_l, 0, False, True)
    _, y_bl = wkv7_scan(s_b, scan_l, 1, True, True)
    o_rw_l = rwkv_output(y_fl, y_bl, out_l, r_k, ln_w, ln_b)
    heads = lambda t: t.reshape(t.shape[:-1] + (NA_HEADS, NA_HD))
    q_c, k_c, v_c = (heads(t) for t in jnp.split(na_c, 3, axis=-1))
    q_l, k_l, v_l = (heads(t) for t in jnp.split(na_l, 3, axis=-1))
    o_na_l = natten_latent(q_l, k_l, v_l, k_c, v_c, na_rpb)
    o_lat = jnp.concatenate([o_hy_l.astype(u_lat.dtype), o_rw_l.astype(u_lat.dtype), o_na_l], axis=-1) @ w_out
    if not ctx_out:
        return o_lat, None
    o_hy_c = hyena_mixer(hy_c, hy_conv_w, hy_conv_b, hyena_filters(u_ctx.shape[1], f_w1, f_b1, f_w2, f_b2, f_w3, f_freq), hy_bias)
    o_rw_c = rwkv_output(y_fc, y_bc, out_c, r_k, ln_w, ln_b)
    o_na_c = context_attention(q_c, k_c, v_c)
    o_ctx = jnp.concatenate([o_hy_c.astype(u_ctx.dtype), o_rw_c.astype(u_ctx.dtype), o_na_c], axis=-1) @ w_out
    return o_lat, o_ctx


def setup_inputs(seed: int = 0) -> dict:
    key = jax.random.key(seed)
    ks = iter(jax.random.split(key, 40))
    f32 = jnp.float32
    D = D_MODEL

    def nrm(shape, s=1.0):
        return s * jax.random.normal(next(ks), shape, f32)

    return {
        'x': nrm((BATCH, SEQ, D)),
        'c': nrm((BATCH, D)),
        'ctx': nrm((BATCH, CTX_LEN, D)),
        'c_ctx': nrm((D,)),
        'mod_w': nrm((DEPTH, D, N_MOD * D), 0.5 * D ** -0.5),
        'mod_b': nrm((DEPTH, N_MOD * D), 0.01),
        'norm_g': 1.0 + nrm((DEPTH, N_NORMS, D), 0.05),
        'ffn1_wgu': nrm((DEPTH, D, 2 * D_FF), D ** -0.5),
        'ffn1_wdn': nrm((DEPTH, D_FF, D), D_FF ** -0.5),
        'ffn2_wgu': nrm((DEPTH, D, 2 * D_FF), D ** -0.5),
        'ffn2_wdn': nrm((DEPTH, D_FF, D), D_FF ** -0.5),
        'w_in': nrm((DEPTH, D, IN_W), D ** -0.5),
        'w_out': nrm((DEPTH, MIX_W, D), MIX_W ** -0.5),
        'hy_conv_w': nrm((DEPTH, HY_SHORT, HY_IN), 0.5),
        'hy_conv_b': nrm((DEPTH, HY_IN), 0.01),
        'hy_f_w1': nrm((DEPTH, HY_EMB, HY_FILT), HY_EMB ** -0.5),
        'hy_f_b1': nrm((DEPTH, HY_FILT), 0.1),
        'hy_f_w2': nrm((DEPTH, HY_FILT, HY_FILT), HY_FILT ** -0.5),
        'hy_f_b2': nrm((DEPTH, HY_FILT), 0.1),
        'hy_f_w3': nrm((DEPTH, HY_FILT, HY_ORDER * 2 * HY_CH), 0.01),
        'hy_freq': 1.0 + nrm((DEPTH, HY_FILT), 0.05),
        'hy_bias': nrm((DEPTH, HY_ORDER, HY_CH), 0.5),
        'rw_mu': jax.random.uniform(next(ks), (DEPTH, 2, RW_IN), f32, 0.0, 0.5),
        'rw_w0': -1.0 + nrm((DEPTH, 2, RW_W), 0.5),
        'rw_w2': nrm((DEPTH, 2, RW_DECAY_LORA, RW_W), 0.5 * RW_DECAY_LORA ** -0.5),
        'rw_a0': nrm((DEPTH, 2, RW_W), 0.1),
        'rw_a2': nrm((DEPTH, 2, RW_AAA_LORA, RW_W), 0.5 * RW_AAA_LORA ** -0.5),
        'rw_g2': nrm((DEPTH, RW_GATE_LORA, RW_W), RW_GATE_LORA ** -0.5),
        'rw_k_k': 0.85 + nrm((DEPTH, RW_W), 0.05),
        'rw_k_a': 1.0 + nrm((DEPTH, RW_W), 0.05),
        'rw_r_k': nrm((DEPTH, RW_HEADS, RW_HD), 0.1),
        'rw_ln_w': 1.0 + nrm((DEPTH, RW_W), 0.05),
        'rw_ln_b': nrm((DEPTH, RW_W), 0.01),
        'na_rpb': nrm((DEPTH, NA_HEADS, 2 * WIN_ROWS - 1, 2 * WIN_COLS - 1), 0.1),
    }


def reference(x, c, ctx, c_ctx, mod_w, mod_b, norm_g, ffn1_wgu, ffn1_wdn, ffn2_wgu, ffn2_wdn, w_in, w_out,
              hy_conv_w, hy_conv_b, hy_f_w1, hy_f_b1, hy_f_w2, hy_f_b2, hy_f_w3, hy_freq, hy_bias,
              rw_mu, rw_w0, rw_w2, rw_a0, rw_a2, rw_g2, rw_k_k, rw_k_a, rw_r_k, rw_ln_w, rw_ln_b, na_rpb):
    rope_cs = axial_rope_tables(x.shape[1], RW_HD)
    silu_c = jax.nn.silu(c)
    silu_cc = jax.nn.silu(c_ctx)
    h_lat, h_ctx = x, ctx
    for l in range(DEPTH):
        ctx_out = l < DEPTH - 1
        m_l = jnp.split((silu_c @ mod_w[l] + mod_b[l])[:, None, :], N_MOD, axis=-1)
        m_c = jnp.split(silu_cc @ mod_w[l] + mod_b[l], N_MOD, axis=-1)
        g = norm_g[l]
        h_lat = ffn_sublayer(h_lat, m_l[0], m_l[1], m_l[2], g[0], g[1], ffn1_wgu[l], ffn1_wdn[l])
        h_ctx = ffn_sublayer(h_ctx, m_c[0], m_c[1], m_c[2], g[0], g[1], ffn1_wgu[l], ffn1_wdn[l])
        u_lat = modulate(rmsnorm(h_lat, g[2]), m_l[3], m_l[4])
        u_ctx = modulate(rmsnorm(h_ctx, g[2]), m_c[3], m_c[4])
        hy = (hy_conv_w[l], hy_conv_b[l], hy_f_w1[l], hy_f_b1[l], hy_f_w2[l], hy_f_b2[l], hy_f_w3[l], hy_freq[l], hy_bias[l])
        rw = (rw_mu[l], rw_w0[l], rw_w2[l], rw_a0[l], rw_a2[l], rw_g2[l], rw_k_k[l], rw_k_a[l], rw_r_k[l], rw_ln_w[l], rw_ln_b[l])
        o_lat, o_ctx = token_mixer(u_ctx, u_lat, rope_cs, w_in[l], w_out[l], hy, rw, na_rpb[l], ctx_out)
        h_lat = h_lat + m_l[5] * rmsnorm(o_lat, g[3])
        h_lat = ffn_sublayer(h_lat, m_l[6], m_l[7], m_l[8], g[4], g[5], ffn2_wgu[l], ffn2_wdn[l])
        if ctx_out:
            h_ctx = h_ctx + m_c[5] * rmsnorm(o_ctx, g[3])
            h_ctx = ffn_sublayer(h_ctx, m_c[6], m_c[7], m_c[8], g[4], g[5], ffn2_wgu[l], ffn2_wdn[l])
    return h_lat
```

```python
import functools
import math

import numpy as np
import jax
import jax.numpy as jnp
from jax import lax
from jax.experimental import pallas as pl
from jax.experimental.pallas import tpu as pltpu

D_MODEL = 1024
BATCH = 2
SEQ = 8192
DEPTH = 4
GRID_W = 64
CTX_LEN = 256
T_ALL = SEQ + CTX_LEN
N_MOD = 9
D_FF = 2816
MACARON_W = 0.5
NORM_EPS = 1e-6
NEG_INF = -1e30

HY_CH = 256
HY_ORDER = 2
HY_EMB = 33
HY_FILT = 64
HY_TARGET = 1e-2
HY_FAST = 0.3
HY_SLOW = 1.5
HY_IN = (HY_ORDER + 1) * HY_CH

RW_HEADS = 6
RW_HD = 64
RW_W = RW_HEADS * RW_HD
RW_LORA = 64
RW_GATE_LORA = 128
RW_GN_EPS = 64e-5
ROPE_BASE = 10000.0
RW_IN = 3 * RW_W + 4 * RW_LORA + RW_GATE_LORA

NA_HEADS = 6
NA_HD = 64
NA_W = NA_HEADS * NA_HD
WIN_ROWS = 8
WIN_COLS = 16
NA_IN = 3 * NA_W
N_ROWS = SEQ // GRID_W

F32 = jnp.float32
BF16 = jnp.bfloat16
HI = lax.Precision.HIGHEST

VMEM_LIMIT = 56 * 1024 * 1024

FFN_TM = 768
FFN_TF = 1408
PROJ_TM = 384
RW_TR = 256
RW_C = 64
FFT_N1 = 128
FFT_N2 = 128
FFT_CT = 16


def _dot(a, b):
    return jnp.dot(a, b, precision=HI, preferred_element_type=F32)


def _cparams(sem, vmem=VMEM_LIMIT):
    return pltpu.CompilerParams(dimension_semantics=sem, vmem_limit_bytes=vmem)


def _mod_kernel(cv_ref, w_ref, b_ref, o_ref):
    cv = cv_ref[...]
    s = cv * jax.nn.sigmoid(cv)
    o_ref[0] = _dot(s, w_ref[0]) + b_ref[0]


def _modulation(c, c_ctx, mod_w, mod_b):
    cv = jnp.concatenate([c, c_ctx[None, :], jnp.zeros((8 - BATCH - 1, D_MODEL), F32)], axis=0)
    tn = 1536
    return pl.pallas_call(
        _mod_kernel,
        grid=(DEPTH, N_MOD * D_MODEL // tn),
        in_specs=[pl.BlockSpec((8, D_MODEL), lambda l, j: (0, 0)),
                  pl.BlockSpec((1, D_MODEL, tn), lambda l, j: (l, 0, j)),
                  pl.BlockSpec((1, 1, tn), lambda l, j: (l, 0, j))],
        out_specs=pl.BlockSpec((1, 8, tn), lambda l, j: (l, 0, j)),
        out_shape=jax.ShapeDtypeStruct((DEPTH, 8, N_MOD * D_MODEL), F32),
        compiler_params=_cparams(("parallel", "parallel")),
        name="adaln_modulation",
    )(cv, mod_w, mod_b.reshape(DEPTH, 1, N_MOD * D_MODEL))


def _mod_rows(m_ref, b, i, tm, idx):
    t = i * tm + lax.broadcasted_iota(jnp.int32, (tm, 1), 0)
    ml = m_ref[pl.ds(b, 1), idx * D_MODEL:(idx + 1) * D_MODEL]
    mc = m_ref[BATCH:BATCH + 1, idx * D_MODEL:(idx + 1) * D_MODEL]
    return jnp.where(t >= SEQ, mc, ml)


def _rms(x, g):
    return x * lax.rsqrt(jnp.mean(x * x, axis=-1, keepdims=True) + NORM_EPS) * g


def _ffn_kernel(h_ref, m_ref, g_ref, wg_ref, wu_ref, wd_ref, o_ref, u_sc, acc_sc, *, mi, gi):
    b, i, k = pl.program_id(0), pl.program_id(1), pl.program_id(2)

    @pl.when(k == 0)
    def _():
        y = _rms(h_ref[0], g_ref[gi:gi + 1, :])
        u = y * (1.0 + _mod_rows(m_ref, b, i, FFN_TM, mi + 1)) + _mod_rows(m_ref, b, i, FFN_TM, mi)
        u_sc[...] = u.astype(BF16)
        acc_sc[...] = jnp.zeros_like(acc_sc)

    u = u_sc[...]
    gate = jnp.dot(u, wg_ref[...], preferred_element_type=F32)
    up = jnp.dot(u, wu_ref[...], preferred_element_type=F32)
    a = (gate * jax.nn.sigmoid(gate) * up).astype(BF16)
    acc_sc[...] += jnp.dot(a, wd_ref[...], preferred_element_type=F32)

    @pl.when(k == pl.num_programs(2) - 1)
    def _():
        y = _rms(acc_sc[...], g_ref[gi + 1:gi + 2, :])
        o_ref[0] = h_ref[0] + MACARON_W * _mod_rows(m_ref, b, i, FFN_TM, mi + 2) * y


def _ffn(h, mods_l, g_l, wgu, wdn, which):
    mi, gi = (0, 0) if which == 0 else (6, 4)
    nk = D_FF // FFN_TF
    return pl.pallas_call(
        functools.partial(_ffn_kernel, mi=mi, gi=gi),
        grid=(BATCH, T_ALL // FFN_TM, nk),
        in_specs=[pl.BlockSpec((1, FFN_TM, D_MODEL), lambda b, i, k: (b, i, 0)),
                  pl.BlockSpec((8, N_MOD * D_MODEL), lambda b, i, k: (0, 0)),
                  pl.BlockSpec((8, D_MODEL), lambda b, i, k: (0, 0)),
                  pl.BlockSpec((D_MODEL, FFN_TF), lambda b, i, k: (0, k)),
                  pl.BlockSpec((D_MODEL, FFN_TF), lambda b, i, k: (0, k + D_FF // FFN_TF)),
                  pl.BlockSpec((FFN_TF, D_MODEL), lambda b, i, k: (k, 0))],
        out_specs=pl.BlockSpec((1, FFN_TM, D_MODEL), lambda b, i, k: (b, i, 0)),
        out_shape=jax.ShapeDtypeStruct((BATCH, T_ALL, D_MODEL), F32),
        scratch_shapes=[pltpu.VMEM((FFN_TM, D_MODEL), BF16), pltpu.VMEM((FFN_TM, D_MODEL), F32)],
        compiler_params=_cparams(("parallel", "parallel", "arbitrary")),
        name="ffn_sublayer",
    )(h, mods_l, g_l, wgu, wgu, wdn)


def _inproj_kernel(h_ref, m_ref, g_ref, w_ref, hy_ref, rw_ref, na_ref):
    b, i = pl.program_id(0), pl.program_id(1)
    y = _rms(h_ref[0], g_ref[2:3, :])
    u = y * (1.0 + _mod_rows(m_ref, b, i, PROJ_TM, 4)) + _mod_rows(m_ref, b, i, PROJ_TM, 3)
    p = jnp.dot(u.astype(BF16), w_ref[...], preferred_element_type=F32)
    hy_ref[0] = p[:, :HY_IN]
    rw_ref[0] = p[:, HY_IN:HY_IN + RW_IN]
    na_ref[0] = p[:, HY_IN + RW_IN:].astype(BF16)


def _in_proj(h, mods_l, g_l, w_in):
    n_in = HY_IN + RW_IN + NA_IN
    return pl.pallas_call(
        _inproj_kernel,
        grid=(BATCH, T_ALL // PROJ_TM),
        in_specs=[pl.BlockSpec((1, PROJ_TM, D_MODEL), lambda b, i: (b, i, 0)),
                  pl.BlockSpec((8, N_MOD * D_MODEL), lambda b, i: (0, 0)),
                  pl.BlockSpec((8, D_MODEL), lambda b, i: (0, 0)),
                  pl.BlockSpec((D_MODEL, n_in), lambda b, i: (0, 0))],
        out_specs=[pl.BlockSpec((1, PROJ_TM, HY_IN), lambda b, i: (b, i, 0)),
                   pl.BlockSpec((1, PROJ_TM, RW_IN), lambda b, i: (b, i, 0)),
                   pl.BlockSpec((1, PROJ_TM, NA_IN), lambda b, i: (b, i, 0))],
        out_shape=[jax.ShapeDtypeStruct((BATCH, T_ALL, HY_IN), F32),
                   jax.ShapeDtypeStruct((BATCH, T_ALL, RW_IN), F32),
                   jax.ShapeDtypeStruct((BATCH, T_ALL, NA_IN), BF16)],
        compiler_params=_cparams(("parallel", "parallel")),
        name="mixer_in_proj",
    )(h, mods_l, g_l, w_in)


def _outproj_kernel(h_ref, hy_ref, rw_ref, na_ref, m_ref, g_ref, w_ref, o_ref):
    b, i = pl.program_id(0), pl.program_id(1)
    cat = jnp.concatenate([hy_ref[0], rw_ref[0], na_ref[0]], axis=-1).astype(BF16)
    o = jnp.dot(cat, w_ref[...], preferred_element_type=F32)
    o_ref[0] = h_ref[0] + _mod_rows(m_ref, b, i, PROJ_TM, 5) * _rms(o, g_ref[3:4, :])


def _out_proj(h, o_hy, o_rw, o_na, mods_l, g_l, w_out):
    return pl.pallas_call(
        _outproj_kernel,
        grid=(BATCH, T_ALL // PROJ_TM),
        in_specs=[pl.BlockSpec((1, PROJ_TM, D_MODEL), lambda b, i: (b, i, 0)),
                  pl.BlockSpec((1, PROJ_TM, HY_CH), lambda b, i: (b, i, 0)),
                  pl.BlockSpec((1, PROJ_TM, RW_W), lambda b, i: (b, i, 0)),
                  pl.BlockSpec((1, PROJ_TM, NA_W), lambda b, i: (b, i, 0)),
                  pl.BlockSpec((8, N_MOD * D_MODEL), lambda b, i: (0, 0)),
                  pl.BlockSpec((8, D_MODEL), lambda b, i: (0, 0)),
                  pl.BlockSpec((D_MODEL, D_MODEL), lambda b, i: (0, 0))],
        out_specs=pl.BlockSpec((1, PROJ_TM, D_MODEL), lambda b, i: (b, i, 0)),
        out_shape=jax.ShapeDtypeStruct((BATCH, T_ALL, D_MODEL), F32),
        compiler_params=_cparams(("parallel", "parallel")),
        name="mixer_out_proj",
    )(h, o_hy, o_rw, o_na, mods_l, g_l, w_out)


def _hy_short_kernel(x_ref, w_ref, b_ref, o_ref):
    x = x_ref[0]
    n = x.shape[0]
    row = lax.broadcasted_iota(jnp.int32, (n, 1), 0)
    prev = jnp.where(row == 0, 0.0, pltpu.roll(x, 1, 0))
    nxt = jnp.where(row == n - 1, 0.0, pltpu.roll(x, n - 1, 0))
    o_ref[0] = prev * w_ref[0:1, :] + x * w_ref[1:2, :] + nxt * w_ref[2:3, :] + b_ref[...]


def _hy_short(p_hy, conv_w, conv_b, seq_len, row_block):
    lanes = 128
    return pl.pallas_call(
        _hy_short_kernel,
        grid=(BATCH, HY_IN // lanes),
        in_specs=[pl.BlockSpec((1, seq_len, lanes), lambda b, j: (b, row_block, j)),
                  pl.BlockSpec((3, lanes), lambda b, j: (0, j)),
                  pl.BlockSpec((1, lanes), lambda b, j: (0, j))],
        out_specs=pl.BlockSpec((1, seq_len, lanes), lambda b, j: (b, 0, j)),
        out_shape=jax.ShapeDtypeStruct((BATCH, seq_len, HY_IN), F32),
        compiler_params=_cparams(("parallel", "parallel")),
        name="hyena_short_conv",
    )(p_hy, conv_w, conv_b.reshape(1, HY_IN))


@functools.lru_cache(maxsize=None)
def _filter_features(seq_len):
    n = 2 * seq_len
    m = np.arange(n)
    d = np.where(m < seq_len, m, n - m).astype(np.float64)
    t = d / (seq_len - 1)
    ang = 2.0 * math.pi * d / seq_len
    bands = (HY_EMB - 1) // 2
    fr = np.linspace(1e-4, bands - 1, bands)
    z = np.zeros((n, 128), np.float64)
    z[:, 0] = t
    z[:, 1:1 + bands] = np.cos(fr[None, :] * ang[:, None])
    z[:, 1 + bands:1 + 2 * bands] = -np.sin(fr[None, :] * ang[:, None])
    z[:, 33] = (m < seq_len)
    z[:, 34] = (m > seq_len)
    z[seq_len, :] = 0.0
    return z.astype(np.float32)


@functools.lru_cache(maxsize=None)
def _filter_deltas():
    d = np.abs(np.linspace(math.log(HY_TARGET) / HY_SLOW, math.log(HY_TARGET) / HY_FAST, HY_CH))
    return d.astype(np.float32).reshape(1, HY_CH)


def _hy_filter_kernel(z_ref, w1_ref, b1_ref, w2_ref, b2_ref, w3_ref, fq_ref, dl_ref, o_ref):
    z = z_ref[...]
    fq = fq_ref[...]
    h = jnp.sin(fq * (_dot(z, w1_ref[...]) + b1_ref[...]))
    h = jnp.sin(fq * (_dot(h, w2_ref[...]) + b2_ref[...]))
    o = _dot(h, w3_ref[...])
    dec = jnp.exp(-z[:, 0:1] * dl_ref[...])
    mf = z[:, 33:34]
    mb = z[:, 34:35]
    outs = []
    for od in range(HY_ORDER):
        fwd = o[:, od * 2 * HY_CH:od * 2 * HY_CH + HY_CH]
        bwd = o[:, od * 2 * HY_CH + HY_CH:(od + 1) * 2 * HY_CH]
        outs.append((mf * fwd + mb * bwd) * dec)
    o_ref[...] = jnp.concatenate(outs, axis=1)


def _hy_filter(seq_len, w1, b1, w2, b2, w3, freq):
    n = 2 * seq_len
    tp = min(n, 1024)
    zt = jnp.asarray(_filter_features(seq_len))
    w1p = jnp.zeros((128, HY_FILT), F32).at[:HY_EMB].set(w1)
    full = lambda shape: pl.BlockSpec(shape, lambda i: (0, 0))
    return pl.pallas_call(
        _hy_filter_kernel,
        grid=(n // tp,),
        in_specs=[pl.BlockSpec((tp, 128), lambda i: (i, 0)),
                  full((128, HY_FILT)), full((1, HY_FILT)), full((HY_FILT, HY_FILT)), full((1, HY_FILT)),
                  full((HY_FILT, HY_ORDER * 2 * HY_CH)), full((1, HY_FILT)), full((1, HY_CH))],
        out_specs=pl.BlockSpec((tp, HY_ORDER * HY_CH), lambda i: (i, 0)),
        out_shape=jax.ShapeDtypeStruct((n, HY_ORDER * HY_CH), F32),
        compiler_params=_cparams(("parallel",)),
        name="hyena_filter_mlp",
    )(zt, w1p, b1.reshape(1, -1), w2, b2.reshape(1, -1), w3, freq.reshape(1, -1), jnp.asarray(_filter_deltas()))


@functools.lru_cache(maxsize=None)
def _fft_tables():
    n1 = np.arange(FFT_N1)[:, None]
    k1 = np.arange(FFT_N1)[None, :]
    th = 2.0 * math.pi * n1 * k1 / FFT_N1
    c, s = np.cos(th), np.sin(th)
    half = FFT_N1 // 2
    big_n = FFT_N1 * FFT_N2
    wa_c = np.block([[c[:half], -s[:half]], [s[:half], c[:half]]])
    wa_r = np.concatenate([c, -s], axis=1)
    tw = 2.0 * math.pi * np.arange(FFT_N2)[:, None] * np.arange(FFT_N1)[None, :] / big_n
    twf = np.concatenate([np.cos(tw), -np.sin(tw)], axis=1)
    twi = np.concatenate([np.cos(tw.T), np.sin(tw.T)], axis=1)
    wb = np.block([[c, -s], [s, c]])
    wc = np.block([[c, s], [-s, c]])
    wd = np.block([[c[:, :half], s[:, :half]], [-s[:, :half], c[:, :half]]]) / big_n
    return tuple(np.asarray(a, np.float32) for a in (wa_c, wa_r, twf, twi, wb, wc, wd))


def _fft_fwd(z, wa, tw, wb):
    ct = z.shape[0]
    a = _dot(z.reshape(ct * FFT_N2, z.shape[2]), wa).reshape(ct, FFT_N2, 2 * FFT_N1)
    ar, ai = a[:, :, :FFT_N1], a[:, :, FFT_N1:]
    tr, ti = tw[:, :FFT_N1], tw[:, FFT_N1:]
    br = ar * tr - ai * ti
    bi = ar * ti + ai * tr
    bt = jnp.concatenate([jnp.swapaxes(br, 1, 2), jnp.swapaxes(bi, 1, 2)], axis=2)
    return _dot(bt.reshape(ct * FFT_N1, 2 * FFT_N2), wb).reshape(ct, FFT_N1, 2 * FFT_N2)


def _hy_kfft_kernel(k_ref, wa_ref, tw_ref, wb_ref, o_ref):
    o_ref[...] = _fft_fwd(k_ref[...], wa_ref[...], tw_ref[...], wb_ref[...])


def _hy_long_kernel(z_ref, g_ref, kh_ref, bias_ref, wa_ref, twf_ref, wb_ref, wc_ref, twi_ref, wd_ref, o_ref):
    z = z_ref[...]
    ct = z.shape[0]
    x = _fft_fwd(z, wa_ref[...], twf_ref[...], wb_ref[...])
    kh = kh_ref[...]
    xr, xi = x[:, :, :FFT_N2], x[:, :, FFT_N2:]
    kr, ki = kh[:, :, :FFT_N2], kh[:, :, FFT_N2:]
    y = jnp.concatenate([xr * kr - xi * ki, xr * ki + xi * kr], axis=2)
    c1 = _dot(y.reshape(ct * FFT_N1, 2 * FFT_N2), wc_ref[...]).reshape(ct, FFT_N1, 2 * FFT_N2)
    cr, ci = c1[:, :, :FFT_N2], c1[:, :, FFT_N2:]
    twi = twi_ref[...]
    tr, ti = twi[:, :FFT_N2], twi[:, FFT_N2:]
    dr = cr * tr - ci * ti
    di = cr * ti + ci * tr
    dt = jnp.concatenate([jnp.swapaxes(dr, 1, 2), jnp.swapaxes(di, 1, 2)], axis=2)
    conv = _dot(dt.reshape(ct * FFT_N2, 2 * FFT_N1), wd_ref[...]).reshape(ct, FFT_N2, FFT_N1)
    o_ref[...] = (conv + bias_ref[...] * z) * g_ref[...]


def _hy_ctx_kernel(x_ref, kc_ref, bias_ref, cf_ref, sf_ref, o_ref):
    x = x_ref[0]
    cf, sf = cf_ref[...], sf_ref[...]
    n = CTX_LEN
    z = x[:, :HY_CH]
    for od in range(HY_ORDER):
        k = kc_ref[:, od * HY_CH:(od + 1) * HY_CH]
        kr, ki = _dot(cf, k), -_dot(sf, k)
        xr, xi = _dot(cf[:, :n], z), -_dot(sf[:, :n], z)
        yr = xr * kr - xi * ki
        yi = xr * ki + xi * kr
        conv = (_dot(cf[:n, :], yr) - _dot(sf[:n, :], yi)) * (1.0 / (2 * n))
        z = x[:, (od + 1) * HY_CH:(od + 2) * HY_CH] * (conv + bias_ref[od:od + 1, :] * z)
    o_ref[0] = z


@functools.lru_cache(maxsize=None)
def _ctx_dft():
    m = np.arange(2 * CTX_LEN)
    th = 2.0 * math.pi * m[:, None] * m[None, :] / (2 * CTX_LEN)
    return np.cos(th).astype(np.float32), np.sin(th).astype(np.float32)


def _hyena(p_hy, conv_w, conv_b, w1, b1, w2, b2, w3, freq, bias):
    wa_c, wa_r, twf, twi, wb, wc, wd = (jnp.asarray(a) for a in _fft_tables())
    nch = HY_ORDER * HY_CH
    kf = _hy_filter(SEQ, w1, b1, w2, b2, w3, freq)
    kt = kf.reshape(FFT_N1, FFT_N2, nch).transpose(2, 1, 0)
    const = lambda shape: pl.BlockSpec(shape, lambda j: (0,) * len(shape))
    khat = pl.pallas_call(
        _hy_kfft_kernel,
        grid=(nch // FFT_CT,),
        in_specs=[pl.BlockSpec((FFT_CT, FFT_N2, FFT_N1), lambda j: (j, 0, 0)),
                  const((FFT_N1, 2 * FFT_N1)), const((FFT_N2, 2 * FFT_N1)), const((2 * FFT_N2, 2 * FFT_N2))],
        out_specs=pl.BlockSpec((FFT_CT, FFT_N1, 2 * FFT_N2), lambda j: (j, 0, 0)),
        out_shape=jax.ShapeDtypeStruct((nch, FFT_N1, 2 * FFT_N2), F32),
        compiler_params=_cparams(("parallel",)),
        name="hyena_filter_fft",
    )(kt, wa_r, twf, wb)

    pc = _hy_short(p_hy, conv_w, conv_b, SEQ, 0)
    half = FFT_N1 // 2
    zall = pc.reshape(BATCH, half, FFT_N2, HY_IN).transpose(3, 2, 0, 1).reshape(HY_IN, FFT_N2, FFT_N1)
    nblk = HY_CH // FFT_CT
    z = zall
    for od in range(HY_ORDER):
        z_spec = pl.BlockSpec((FFT_CT, FFT_N2, FFT_N1), lambda j: (j, 0, 0))
        g_spec = pl.BlockSpec((FFT_CT, FFT_N2, FFT_N1), lambda j, od=od: ((od + 1) * nblk + j, 0, 0))
        z = pl.pallas_call(
            _hy_long_kernel,
            grid=(nblk,),
            in_specs=[z_spec, g_spec,
                      pl.BlockSpec((FFT_CT, FFT_N1, 2 * FFT_N2), lambda j, od=od: (od * nblk + j, 0, 0)),
                      pl.BlockSpec((FFT_CT, 1, 1), lambda j, od=od: (od * nblk + j, 0, 0)),
                      const((FFT_N1, 2 * FFT_N1)), const((FFT_N2, 2 * FFT_N1)), const((2 * FFT_N2, 2 * FFT_N2)),
                      const((2 * FFT_N2, 2 * FFT_N2)), const((FFT_N1, 2 * FFT_N2)), const((2 * FFT_N1, FFT_N1))],
            out_specs=pl.BlockSpec((FFT_CT, FFT_N2, FFT_N1), lambda j: (j, 0, 0)),
            out_shape=jax.ShapeDtypeStruct((HY_CH, FFT_N2, FFT_N1), F32),
            compiler_params=_cparams(("parallel",)),
            name="hyena_long_conv",
        )(z, zall, khat, bias.reshape(nch, 1, 1), wa_c, twf, wb, wc, twi, wd)
    o_lat = z.reshape(HY_CH, FFT_N2, BATCH, half).transpose(2, 3, 1, 0).reshape(BATCH, SEQ, HY_CH)

    kc = _hy_filter(CTX_LEN, w1, b1, w2, b2, w3, freq)
    pcc = _hy_short(p_hy, conv_w, conv_b, CTX_LEN, SEQ // CTX_LEN)
    cf, sf = (jnp.asarray(a) for a in _ctx_dft())
    o_ctx = pl.pallas_call(
        _hy_ctx_kernel,
        grid=(BATCH,),
        in_specs=[pl.BlockSpec((1, CTX_LEN, HY_IN), lambda b: (b, 0, 0)),
                  pl.BlockSpec((2 * CTX_LEN, nch), lambda b: (0, 0)),
                  pl.BlockSpec((HY_ORDER, HY_CH), lambda b: (0, 0)),
                  pl.BlockSpec((2 * CTX_LEN, 2 * CTX_LEN), lambda b: (0, 0)),
                  pl.BlockSpec((2 * CTX_LEN, 2 * CTX_LEN), lambda b: (0, 0))],
        out_specs=pl.BlockSpec((1, CTX_LEN, HY_CH), lambda b: (b, 0, 0)),
        out_shape=jax.ShapeDtypeStruct((BATCH, CTX_LEN, HY_CH), F32),
        compiler_params=_cparams(("parallel",)),
        name="hyena_context",
    )(pcc, kc, bias, cf, sf)
    return jnp.concatenate([o_lat, o_ctx], axis=1)


@functools.lru_cache(maxsize=None)
def _rope_tables():
    nf = RW_HD // 4
    t = np.arange(SEQ)
    row = (t // GRID_W).astype(np.float64)
    col = (t % GRID_W).astype(np.float64)
    inv = ROPE_BASE ** (-np.arange(nf, dtype=np.float64) / nf)
    ang = np.concatenate([row[:, None] * inv] * 2 + [col[:, None] * inv] * 2, axis=-1)
    cos = np.concatenate([np.cos(ang), np.ones((CTX_LEN, RW_HD))], axis=0)
    sin = np.concatenate([np.sin(ang), np.zeros((CTX_LEN, RW_HD))], axis=0)
    return (np.tile(cos, (1, RW_HEADS)).astype(np.float32), np.tile(sin, (1, RW_HEADS)).astype(np.float32))


@functools.lru_cache(maxsize=None)
def _head_blockdiag():
    h = np.arange(RW_W) // RW_HD
    return (h[:, None] == h[None, :]).astype(np.float32)


def _rope(x, cos, sin):
    lane = lax.broadcasted_iota(jnp.int32, (1, RW_W), 1)
    first = (lane % (RW_HD // 2)) < (RW_HD // 4)
    rot = jnp.where(first, -pltpu.roll(x, RW_W - RW_HD // 4, 1), pltpu.roll(x, RW_HD // 4, 1))
    return x * cos + rot * sin


def _rw_prep_kernel(p_ref, pv_ref, nx_ref, mu_ref, w0_ref, w2_ref, a0_ref, a2_ref, g2_ref, kk_ref, ka_ref, rk_ref,
                    cos_ref, sin_ref, bd_ref,
                    r_ref, na_ref, v_ref, lw0_ref, lw1_ref, ks0_ref, ks1_ref, bs0_ref, bs1_ref, bon_ref, g_ref):
    i = pl.program_id(1)
    x = p_ref[0]
    row = lax.broadcasted_iota(jnp.int32, (RW_TR, 1), 0)
    t = i * RW_TR + row
    seq_first = (t == 0) | (t == SEQ)
    seq_last = (t == SEQ - 1) | (t == T_ALL - 1)
    prev = jnp.where(row == 0, pv_ref[0, 7:8, :], pltpu.roll(x, 1, 0))
    prev = jnp.where(seq_first, 0.0, prev)
    nxt = jnp.where(row == RW_TR - 1, nx_ref[0, 0:1, :], pltpu.roll(x, RW_TR - 1, 0))
    nxt = jnp.where(seq_last, 0.0, nxt)
    xs = x + mu_ref[0:1, :] * (prev - x) + mu_ref[1:2, :] * (nxt - x)
    r = xs[:, 0:RW_W]
    k = xs[:, RW_W:2 * RW_W]
    v = xs[:, 2 * RW_W:3 * RW_W]
    i1 = 3 * RW_W
    wd = jnp.tanh(xs[:, i1:i1 + 2 * RW_LORA])
    ad = xs[:, i1 + 2 * RW_LORA:i1 + 4 * RW_LORA]
    gd = xs[:, i1 + 4 * RW_LORA:]
    cos, sin, bd = cos_ref[...], sin_ref[...], bd_ref[...]
    g_ref[0] = _dot(jax.nn.sigmoid(gd), g2_ref[...])
    kk = k * kk_ref[...]
    nrm = jnp.sqrt(_dot(kk * kk, bd))
    kk = kk / jnp.maximum(nrm, 1e-12)
    ksum = jnp.zeros_like(k)
    for d, (lw_ref, ks_ref, bs_ref) in enumerate(((lw0_ref, ks0_ref, bs0_ref), (lw1_ref, ks1_ref, bs1_ref))):
        logit = w0_ref[d:d + 1, :] + _dot(wd, w2_ref[d])
        logw = -jax.nn.softplus(-logit) - 0.5
        lw_ref[0] = -jnp.exp(logw)
        a = jax.nn.sigmoid(a0_ref[d:d + 1, :] + _dot(ad, a2_ref[d]))
        kdir = k * (1.0 + (a - 1.0) * ka_ref[...])
        ksum = ksum + kdir
        ks_ref[0] = _rope(kdir, cos, sin)
        bs_ref[0] = _rope(kk * a, cos, sin)
    bon_ref[0] = _dot(r * ksum * rk_ref[...], bd)
    r_ref[0] = _rope(r, cos, sin)
    na_ref[0] = -_rope(kk, cos, sin)
    v_ref[0] = v


def _rw_scan_kernel(r_ref, a_ref, v_ref, lw_ref, k_ref, b_ref, y_ref, s_sc):
    d = pl.program_id(0)
    step = pl.program_id(1)
    nb = BATCH * RW_HEADS
    c = RW_C

    @pl.when(step == 0)
    def _():
        s_sc[...] = jnp.zeros_like(s_sc)

    sh = (nb, c, RW_HD)
    r = r_ref[...].reshape(sh)
    a = a_ref[...].reshape(sh)
    v = v_ref[...].reshape(sh)
    lw = lw_ref[...].reshape(sh)
    k = k_ref[...].reshape(sh)
    b = b_ref[...].reshape(sh)

    ti = lax.broadcasted_iota(jnp.int32, (c, c), 0)
    si = lax.broadcasted_iota(jnp.int32, (c, c), 1)
    ahead = (ti - si) * jnp.where(d == 0, 1, -1)
    strict = (ahead > 0)[None]
    incl = (ahead >= 0)[None]
    eye = (si == ti).astype(F32)[None]

    bmm = functools.partial(jnp.einsum, precision=HI, preferred_element_type=F32)
    tri = jnp.broadcast_to(incl.astype(F32), (nb, c, c))
    cum = bmm('hts,hsj->htj', tri, lw)
    e_pos = jnp.exp(cum)
    e_neg = jnp.exp(-cum)
    at = a * jnp.exp(cum - lw)
    rt = r * e_pos
    bt = b * e_neg
    kt = k * e_neg
    wc = jnp.exp(jnp.sum(lw, axis=1, keepdims=True))

    lab = jnp.where(strict, bmm('htj,hsj->hts', at, bt), 0.0)
    lak = jnp.where(strict, bmm('htj,hsj->hts', at, kt), 0.0)
    lrb = jnp.where(incl, bmm('htj,hsj->hts', rt, bt), 0.0)
    lrk = jnp.where(incl, bmm('htj,hsj->hts', rt, kt), 0.0)

    tt = eye + lab
    pw = lab
    for _ in range(int(math.log2(c)) - 1):
        pw = bmm('hts,hsu->htu', pw, pw)
        tt = tt + bmm('hts,hsu->htu', tt, pw)

    u0 = bmm('hts,hsv->htv', tt, bmm('hts,hsv->htv', lak, v))
    pa = bmm('hts,hsk->htk', tt, at)
    q = rt + bmm('hts,hsk->htk', lrb, pa)
    y0 = bmm('hts,hsv->htv', lrb, u0) + bmm('hts,hsv->htv', lrk, v)
    s0 = s_sc[...]
    y = bmm('htk,hvk->htv', q, s0) + y0
    y_ref[...] = y.reshape(y_ref.shape)
    g = bmm('htk,htj->hkj', pa, bt)
    hm = bmm('htv,htj->hvj', u0, bt) + bmm('htv,htj->hvj', v, kt)
    s_sc[...] = (s0 + bmm('hvk,hkj->hvj', s0, g) + hm) * wc


def _rw_out_kernel(yf_ref, yb_ref, bon_ref, v_ref, g_ref, lnw_ref, lnb_ref, bd_ref, o_ref):
    y = yf_ref[0, 0] + yb_ref[0, 0]
    bd = bd_ref[...]
    mean = _dot(y, bd) * (1.0 / RW_HD)
    yc = y - mean
    var = _dot(yc * yc, bd) * (1.0 / RW_HD)
    yn = yc * lax.rsqrt(var + RW_GN_EPS) * lnw_ref[...] + lnb_ref[...]
    o_ref[0] = (yn + bon_ref[0] * v_ref[0]) * g_ref[0]


def _rw_chunk_index(d, s):
    n_lat = SEQ // RW_C
    n_all = T_ALL // RW_C
    fwd = jnp.where(s < n_all - n_lat, n_lat + s, s - (n_all - n_lat))
    return jnp.where(d == 0, fwd, n_all - 1 - s)


def _rwkv(p_rw, mu, w0, w2, a0, a2, g2, k_k, k_a, r_k, ln_w, ln_b):
    cos, sin = (jnp.asarray(a) for a in _rope_tables())
    bd = jnp.asarray(_head_blockdiag())
    zpad = lambda w: jnp.stack([jnp.concatenate([w[0], jnp.zeros_like(w[0])], axis=0),
                                jnp.concatenate([jnp.zeros_like(w[1]), w[1]], axis=0)])
    nt = T_ALL // RW_TR
    tok = lambda w: pl.BlockSpec((1, RW_TR, w), lambda b, i: (b, i, 0))
    const = lambda shape: pl.BlockSpec(shape, lambda b, i: (0,) * len(shape))
    row = lambda a: a.reshape(1, RW_W)
    outs = pl.pallas_call(
        _rw_prep_kernel,
        grid=(BATCH, nt),
        in_specs=[tok(RW_IN),
                  pl.BlockSpec((1, 8, RW_IN), lambda b, i: (b, jnp.maximum(i * (RW_TR // 8) - 1, 0), 0)),
                  pl.BlockSpec((1, 8, RW_IN), lambda b, i: (b, jnp.minimum((i + 1) * (RW_TR // 8), T_ALL // 8 - 1), 0)),
                  const((2, RW_IN)), const((2, RW_W)), const((2, 2 * RW_LORA, RW_W)), const((2, RW_W)),
                  const((2, 2 * RW_LORA, RW_W)), const((RW_GATE_LORA, RW_W)),
                  const((1, RW_W)), const((1, RW_W)), const((1, RW_W)),
                  pl.BlockSpec((RW_TR, RW_W), lambda b, i: (i, 0)), pl.BlockSpec((RW_TR, RW_W), lambda b, i: (i, 0)),
                  const((RW_W, RW_W))],
        out_specs=[tok(RW_W)] * 11,
        out_shape=[jax.ShapeDtypeStruct((BATCH, T_ALL, RW_W), F32)] * 11,
        compiler_params=_cparams(("parallel", "parallel")),
        name="rwkv_prepare",
    )(p_rw, p_rw, p_rw, mu, w0, zpad(w2), a0, zpad(a2), g2, row(k_k), row(k_a), row(r_k.reshape(-1)), cos, sin, bd)
    r_s, na_s, v, lw0, lw1, ks0, ks1, bs0, bs1, bonus, g = outs

    heads = lambda x: x.reshape(BATCH, T_ALL, RW_HEADS, RW_HD).transpose(0, 2, 1, 3)
    heads2 = lambda x0, x1: jnp.stack([heads(x0), heads(x1)])
    shared = pl.BlockSpec((BATCH, RW_HEADS, RW_C, RW_HD), lambda d, s: (0, 0, _rw_chunk_index(d, s), 0))
    perdir = pl.BlockSpec((1, BATCH, RW_HEADS, RW_C, RW_HD), lambda d, s: (d, 0, 0, _rw_chunk_index(d, s), 0))
    y = pl.pallas_call(
        _rw_scan_kernel,
        grid=(2, T_ALL // RW_C),
        in_specs=[shared, shared, shared, perdir, perdir, perdir],
        out_specs=perdir,
        out_shape=jax.ShapeDtypeStruct((2, BATCH, RW_HEADS, T_ALL, RW_HD), F32),
        scratch_shapes=[pltpu.VMEM((BATCH * RW_HEADS, RW_HD, RW_HD), F32)],
        compiler_params=_cparams(("parallel", "arbitrary")),
        name="rwkv_chunk_scan",
    )(heads(r_s), heads(na_s), heads(v), heads2(lw0, lw1), heads2(ks0, ks1), heads2(bs0, bs1))
    yt = y.transpose(0, 1, 3, 2, 4).reshape(2, BATCH, T_ALL, RW_W)

    tok2 = lambda w: pl.BlockSpec((1, PROJ_TM, w), lambda b, i: (b, i, 0))
    return pl.pallas_call(
        _rw_out_kernel,
        grid=(BATCH, T_ALL // PROJ_TM),
        in_specs=[pl.BlockSpec((1, 1, PROJ_TM, RW_W), lambda b, i: (0, b, i, 0)),
                  pl.BlockSpec((1, 1, PROJ_TM, RW_W), lambda b, i: (1, b, i, 0)),
                  tok2(RW_W), tok2(RW_W), tok2(RW_W),
                  const((1, RW_W)), const((1, RW_W)), const((RW_W, RW_W))],
        out_specs=tok2(RW_W),
        out_shape=jax.ShapeDtypeStruct((BATCH, T_ALL, RW_W), F32),
        compiler_params=_cparams(("parallel", "parallel")),
        name="rwkv_output",
    )(yt, yt, bonus, v, g, row(ln_w), row(ln_b), bd)


@functools.lru_cache(maxsize=None)
def _na_bias_tables():
    qc = np.arange(GRID_W)[:, None]
    kc = np.arange(GRID_W)[None, :]
    win0 = np.clip(qc - WIN_COLS // 2, 0, GRID_W - WIN_COLS)
    valid = (kc >= win0) & (kc < win0 + WIN_COLS)
    off = np.clip(kc - qc + WIN_COLS - 1, 0, 2 * WIN_COLS - 2)
    onehot = np.zeros((128, GRID_W * GRID_W), np.float32)
    onehot[off.reshape(-1), np.arange(GRID_W * GRID_W)] = valid.reshape(-1)
    mask = np.where(valid, 0.0, NEG_INF).astype(np.float32).reshape(1, GRID_W * GRID_W)
    return onehot, mask


def _na_bias_kernel(rpb_ref, oh_ref, mask_ref, o_ref):
    o_ref[...] = _dot(rpb_ref[...], oh_ref[...]) + mask_ref[...]


def _na_lat_kernel(q_ref, k_ref, v_ref, tp_ref, o_ref):
    i = pl.program_id(1)
    start = jnp.clip(i - WIN_ROWS // 2, 0, N_ROWS - WIN_ROWS)
    dr0 = start - i + WIN_ROWS - 1
    scale = NA_HD ** -0.5
    lane = lax.broadcasted_iota(jnp.int32, (1, NA_W), 1)
    hmask = [(lane // NA_HD) == h for h in range(NA_HEADS)]
    q = q_ref[0]
    qm = jnp.concatenate([jnp.where(hmask[h], q, jnp.zeros_like(q)) for h in range(NA_HEADS)], axis=0)
    row0 = pl.multiple_of(start * GRID_W, GRID_W)
    kl = k_ref[0, pl.ds(row0, WIN_ROWS * GRID_W), :]
    vl = v_ref[0, pl.ds(row0, WIN_ROWS * GRID_W), :]
    kc = k_ref[0, SEQ:T_ALL, :]
    vc = v_ref[0, SEQ:T_ALL, :]
    nt = (((1,), (1,)), ((), ()))
    bias = jnp.concatenate(
        [jnp.concatenate([tp_ref[h, pl.ds(dr0 + 2 * j, 1)][0] for j in range(WIN_ROWS // 2)], axis=1)
         for h in range(NA_HEADS)], axis=0)
    s_loc = lax.dot_general(qm, kl, nt, preferred_element_type=F32) * scale + bias
    s_ctx = lax.dot_general(qm, kc, nt, preferred_element_type=F32) * scale
    m = jnp.maximum(jnp.max(s_loc, axis=-1, keepdims=True), jnp.max(s_ctx, axis=-1, keepdims=True))
    p_loc = jnp.exp(s_loc - m)
    p_ctx = jnp.exp(s_ctx - m)
    inv = 1.0 / (jnp.sum(p_loc, axis=-1, keepdims=True) + jnp.sum(p_ctx, axis=-1, keepdims=True))
    o_all = (jnp.dot((p_loc * inv).astype(BF16), vl, preferred_element_type=F32)
             + jnp.dot((p_ctx * inv).astype(BF16), vc, preferred_element_type=F32))
    o = jnp.zeros((GRID_W, NA_W), F32)
    for h in range(NA_HEADS):
        o = o + jnp.where(hmask[h], o_all[h * GRID_W:(h + 1) * GRID_W, :], 0.0)
    o_ref[0] = o


def _na_ctx_kernel(q_ref, k_ref, v_ref, o_ref):
    scale = NA_HD ** -0.5
    lane = lax.broadcasted_iota(jnp.int32, (1, NA_W), 1)
    hmask = [(lane // NA_HD) == h for h in range(NA_HEADS)]
    q, k, v = q_ref[0], k_ref[0], v_ref[0]
    qm = jnp.concatenate([jnp.where(hmask[h], q, jnp.zeros_like(q)) for h in range(NA_HEADS)], axis=0)
    s = lax.dot_general(qm, k, (((1,), (1,)), ((), ())), preferred_element_type=F32) * scale
    p = jnp.exp(s - jnp.max(s, axis=-1, keepdims=True))
    p = p / jnp.sum(p, axis=-1, keepdims=True)
    o_all = jnp.dot(p.astype(BF16), v, preferred_element_type=F32)
    o = jnp.zeros((CTX_LEN, NA_W), F32)
    for h in range(NA_HEADS):
        o = o + jnp.where(hmask[h], o_all[h * CTX_LEN:(h + 1) * CTX_LEN, :], 0.0)
    o_ref[0] = o


def _natten(p_na, rpb):
    onehot, mask = (jnp.asarray(a) for a in _na_bias_tables())
    n_dr = 2 * WIN_ROWS - 1
    n_rel = 2 * WIN_COLS - 1
    rows = NA_HEADS * n_dr
    rows_p = -(-rows // 8) * 8
    rpb2 = jnp.zeros((rows_p, 128), F32).at[:rows, :n_rel].set(rpb.reshape(rows, n_rel))
    tb = pl.pallas_call(
        _na_bias_kernel,
        out_shape=jax.ShapeDtypeStruct((rows_p, GRID_W * GRID_W), F32),
        name="natten_bias_expand",
    )(rpb2, onehot, mask)
    tb = tb[:rows].reshape(NA_HEADS, n_dr, GRID_W, GRID_W)
    nxt = jnp.concatenate([tb[:, 1:], jnp.full_like(tb[:, :1], NEG_INF)], axis=1)
    tp = jnp.concatenate([tb, nxt], axis=-1)

    o_lat = pl.pallas_call(
        _na_lat_kernel,
        grid=(BATCH, N_ROWS),
        in_specs=[pl.BlockSpec((1, GRID_W, NA_W), lambda b, i: (b, i, 0)),
                  pl.BlockSpec((1, T_ALL, NA_W), lambda b, i: (b, 0, 1)),
                  pl.BlockSpec((1, T_ALL, NA_W), lambda b, i: (b, 0, 2)),
                  pl.BlockSpec((NA_HEADS, n_dr, GRID_W, 2 * GRID_W), lambda b, i: (0, 0, 0, 0))],
        out_specs=pl.BlockSpec((1, GRID_W, NA_W), lambda b, i: (b, i, 0)),
        out_shape=jax.ShapeDtypeStruct((BATCH, SEQ, NA_W), F32),
        compiler_params=_cparams(("parallel", "parallel")),
        name="natten_latent",
    )(p_na, p_na, p_na, tp)

    cblk = SEQ // CTX_LEN
    o_ctx = pl.pallas_call(
        _na_ctx_kernel,
        grid=(BATCH,),
        in_specs=[pl.BlockSpec((1, CTX_LEN, NA_W), lambda b: (b, cblk, 0)),
                  pl.BlockSpec((1, CTX_LEN, NA_W), lambda b: (b, cblk, 1)),
                  pl.BlockSpec((1, CTX_LEN, NA_W), lambda b: (b, cblk, 2))],
        out_specs=pl.BlockSpec((1, CTX_LEN, NA_W), lambda b: (b, 0, 0)),
        out_shape=jax.ShapeDtypeStruct((BATCH, CTX_LEN, NA_W), F32),
        compiler_params=_cparams(("parallel",)),
        name="natten_context",
    )(p_na, p_na, p_na)
    return jnp.concatenate([o_lat, o_ctx], axis=1)


def kernel(x, c, ctx, c_ctx, mod_w, mod_b, norm_g, ffn1_wgu, ffn1_wdn, ffn2_wgu, ffn2_wdn, w_in, w_out, hy_conv_w, hy_conv_b, hy_f_w1, hy_f_b1, hy_f_w2, hy_f_b2, hy_f_w3, hy_freq, hy_bias, rw_mu, rw_w0, rw_w2, rw_a0, rw_a2, rw_g2, rw_k_k, rw_k_a, rw_r_k, rw_ln_w, rw_ln_b, na_rpb):
    assert x.shape == (BATCH, SEQ, D_MODEL) and ctx.shape == (BATCH, CTX_LEN, D_MODEL)
    mods = _modulation(c, c_ctx, mod_w, mod_b)
    h = jnp.concatenate([x, ctx], axis=1)
    gpad = jnp.concatenate([norm_g, jnp.zeros((DEPTH, 2, D_MODEL), F32)], axis=1)
    for l in range(DEPTH):
        m_l, g_l = mods[l], gpad[l]
        h = _ffn(h, m_l, g_l, ffn1_wgu[l].astype(BF16), ffn1_wdn[l].astype(BF16), 0)
        p_hy, p_rw, p_na = _in_proj(h, m_l, g_l, w_in[l].astype(BF16))
        o_hy = _hyena(p_hy, hy_conv_w[l], hy_conv_b[l], hy_f_w1[l], hy_f_b1[l], hy_f_w2[l], hy_f_b2[l],
                      hy_f_w3[l], hy_freq[l], hy_bias[l])
        o_rw = _rwkv(p_rw, rw_mu[l], rw_w0[l], rw_w2[l], rw_a0[l], rw_a2[l], rw_g2[l], rw_k_k[l], rw_k_a[l],
                     rw_r_k[l], rw_ln_w[l], rw_ln_b[l])
        o_na = _natten(p_na, na_rpb[l])
        h = _out_proj(h, o_hy, o_rw, o_na, m_l, g_l, w_out[l].astype(BF16))
        h = _ffn(h, m_l, g_l, ffn2_wgu[l].astype(BF16), ffn2_wdn[l].astype(BF16), 1)
    return h[:, :SEQ]
```

```python
import functools
import math

import numpy as np
import jax
import jax.numpy as jnp
from jax import lax
from jax.experimental import pallas as pl
from jax.experimental.pallas import tpu as pltpu

D_MODEL = 1024
BATCH = 2
SEQ = 8192
DEPTH = 4
GRID_W = 64
CTX_LEN = 256
T_ALL = SEQ + CTX_LEN
N_MOD = 9
D_FF = 2816
MACARON_W = 0.5
NORM_EPS = 1e-6
NEG_INF = -1e30

HY_CH = 256
HY_ORDER = 2
HY_EMB = 33
HY_FILT = 64
HY_TARGET = 1e-2
HY_FAST = 0.3
HY_SLOW = 1.5
HY_IN = (HY_ORDER + 1) * HY_CH

RW_HEADS = 6
RW_HD = 64
RW_W = RW_HEADS * RW_HD
RW_LORA = 64
RW_GATE_LORA = 128
RW_GN_EPS = 64e-5
ROPE_BASE = 10000.0
RW_IN = 3 * RW_W + 4 * RW_LORA + RW_GATE_LORA

NA_HEADS = 6
NA_HD = 64
NA_W = NA_HEADS * NA_HD
WIN_ROWS = 8
WIN_COLS = 16
NA_IN = 3 * NA_W
N_ROWS = SEQ // GRID_W

F32 = jnp.float32
BF16 = jnp.bfloat16
HI = lax.Precision.HIGHEST

VMEM_LIMIT = 56 * 1024 * 1024

FFN_TM = 768
FFN_TF = 1408
PROJ_TM = 384
RW_TR = 256
RW_C = 64
FFT_N1 = 128
FFT_N2 = 128
FFT_CT = 16


def _dot(a, b):
    return jnp.dot(a, b, precision=HI, preferred_element_type=F32)


def _cparams(sem, vmem=VMEM_LIMIT):
    return pltpu.CompilerParams(dimension_semantics=sem, vmem_limit_bytes=vmem)


def _mod_kernel(cv_ref, w_ref, b_ref, o_ref):
    cv = cv_ref[...]
    s = cv * jax.nn.sigmoid(cv)
    o_ref[0] = _dot(s, w_ref[0]) + b_ref[0]


def _modulation(c, c_ctx, mod_w, mod_b):
    cv = jnp.concatenate([c, c_ctx[None, :], jnp.zeros((8 - BATCH - 1, D_MODEL), F32)], axis=0)
    tn = 1536
    return pl.pallas_call(
        _mod_kernel,
        grid=(DEPTH, N_MOD * D_MODEL // tn),
        in_specs=[pl.BlockSpec((8, D_MODEL), lambda l, j: (0, 0)),
                  pl.BlockSpec((1, D_MODEL, tn), lambda l, j: (l, 0, j)),
                  pl.BlockSpec((1, 1, tn), lambda l, j: (l, 0, j))],
        out_specs=pl.BlockSpec((1, 8, tn), lambda l, j: (l, 0, j)),
        out_shape=jax.ShapeDtypeStruct((DEPTH, 8, N_MOD * D_MODEL), F32),
        compiler_params=_cparams(("parallel", "parallel")),
        name="adaln_modulation",
    )(cv, mod_w, mod_b.reshape(DEPTH, 1, N_MOD * D_MODEL))


def _mod_rows(m_ref, b, i, tm, idx):
    t = i * tm + lax.broadcasted_iota(jnp.int32, (tm, 1), 0)
    ml = m_ref[pl.ds(b, 1), idx * D_MODEL:(idx + 1) * D_MODEL]
    mc = m_ref[BATCH:BATCH + 1, idx * D_MODEL:(idx + 1) * D_MODEL]
    return jnp.where(t >= SEQ, mc, ml)


def _rms(x, g):
    return x * lax.rsqrt(jnp.mean(x * x, axis=-1, keepdims=True) + NORM_EPS) * g


def _ffn_kernel(h_ref, m_ref, g_ref, wg_ref, wu_ref, wd_ref, o_ref, u_sc, acc_sc, *, mi, gi):
    b, i, k = pl.program_id(0), pl.program_id(1), pl.program_id(2)

    @pl.when(k == 0)
    def _():
        y = _rms(h_ref[0], g_ref[gi:gi + 1, :])
        u = y * (1.0 + _mod_rows(m_ref, b, i, FFN_TM, mi + 1)) + _mod_rows(m_ref, b, i, FFN_TM, mi)
        u_sc[...] = u.astype(BF16)
        acc_sc[...] = jnp.zeros_like(acc_sc)

    u = u_sc[...]
    gate = jnp.dot(u, wg_ref[...], preferred_element_type=F32)
    up = jnp.dot(u, wu_ref[...], preferred_element_type=F32)
    a = (gate * jax.nn.sigmoid(gate) * up).astype(BF16)
    acc_sc[...] += jnp.dot(a, wd_ref[...], preferred_element_type=F32)

    @pl.when(k == pl.num_programs(2) - 1)
    def _():
        y = _rms(acc_sc[...], g_ref[gi + 1:gi + 2, :])
        o_ref[0] = h_ref[0] + MACARON_W * _mod_rows(m_ref, b, i, FFN_TM, mi + 2) * y


def _ffn(h, mods_l, g_l, wgu, wdn, which):
    mi, gi = (0, 0) if which == 0 else (6, 4)
    nk = D_FF // FFN_TF
    return pl.pallas_call(
        functools.partial(_ffn_kernel, mi=mi, gi=gi),
        grid=(BATCH, T_ALL // FFN_TM, nk),
        in_specs=[pl.BlockSpec((1, FFN_TM, D_MODEL), lambda b, i, k: (b, i, 0)),
                  pl.BlockSpec((8, N_MOD * D_MODEL), lambda b, i, k: (0, 0)),
                  pl.BlockSpec((8, D_MODEL), lambda b, i, k: (0, 0)),
                  pl.BlockSpec((D_MODEL, FFN_TF), lambda b, i, k: (0, k)),
                  pl.BlockSpec((D_MODEL, FFN_TF), lambda b, i, k: (0, k + D_FF // FFN_TF)),
                  pl.BlockSpec((FFN_TF, D_MODEL), lambda b, i, k: (k, 0))],
        out_specs=pl.BlockSpec((1, FFN_TM, D_MODEL), lambda b, i, k: (b, i, 0)),
        out_shape=jax.ShapeDtypeStruct((BATCH, T_ALL, D_MODEL), F32),
        scratch_shapes=[pltpu.VMEM((FFN_TM, D_MODEL), BF16), pltpu.VMEM((FFN_TM, D_MODEL), F32)],
        compiler_params=_cparams(("parallel", "parallel", "arbitrary")),
        name="ffn_sublayer",
    )(h, mods_l, g_l, wgu, wgu, wdn)


def _inproj_kernel(h_ref, m_ref, g_ref, w_ref, hy_ref, rw_ref, na_ref):
    b, i = pl.program_id(0), pl.program_id(1)
    y = _rms(h_ref[0], g_ref[2:3, :])
    u = y * (1.0 + _mod_rows(m_ref, b, i, PROJ_TM, 4)) + _mod_rows(m_ref, b, i, PROJ_TM, 3)
    p = jnp.dot(u.astype(BF16), w_ref[...], preferred_element_type=F32)
    hy_ref[0] = p[:, :HY_IN]
    rw_ref[0] = p[:, HY_IN:HY_IN + RW_IN]
    na_ref[0] = p[:, HY_IN + RW_IN:].astype(BF16)


def _in_proj(h, mods_l, g_l, w_in):
    n_in = HY_IN + RW_IN + NA_IN
    return pl.pallas_call(
        _inproj_kernel,
        grid=(BATCH, T_ALL // PROJ_TM),
        in_specs=[pl.BlockSpec((1, PROJ_TM, D_MODEL), lambda b, i: (b, i, 0)),
                  pl.BlockSpec((8, N_MOD * D_MODEL), lambda b, i: (0, 0)),
                  pl.BlockSpec((8, D_MODEL), lambda b, i: (0, 0)),
                  pl.BlockSpec((D_MODEL, n_in), lambda b, i: (0, 0))],
        out_specs=[pl.BlockSpec((1, PROJ_TM, HY_IN), lambda b, i: (b, i, 0)),
                   pl.BlockSpec((1, PROJ_TM, RW_IN), lambda b, i: (b, i, 0)),
                   pl.BlockSpec((1, PROJ_TM, NA_IN), lambda b, i: (b, i, 0))],
        out_shape=[jax.ShapeDtypeStruct((BATCH, T_ALL, HY_IN), F32),
                   jax.ShapeDtypeStruct((BATCH, T_ALL, RW_IN), F32),
                   jax.ShapeDtypeStruct((BATCH, T_ALL, NA_IN), BF16)],
        compiler_params=_cparams(("parallel", "parallel")),
        name="mixer_in_proj",
    )(h, mods_l, g_l, w_in)


def _outproj_kernel(h_ref, hy_ref, rw_ref, na_ref, m_ref, g_ref, w_ref, o_ref):
    b, i = pl.program_id(0), pl.program_id(1)
    cat = jnp.concatenate([hy_ref[0], rw_ref[0], na_ref[0]], axis=-1).astype(BF16)
    o = jnp.dot(cat, w_ref[...], preferred_element_type=F32)
    o_ref[0] = h_ref[0] + _mod_rows(m_ref, b, i, PROJ_TM, 5) * _rms(o, g_ref[3:4, :])


def _out_proj(h, o_hy, o_rw, o_na, mods_l, g_l, w_out):
    return pl.pallas_call(
        _outproj_kernel,
        grid=(BATCH, T_ALL // PROJ_TM),
        in_specs=[pl.BlockSpec((1, PROJ_TM, D_MODEL), lambda b, i: (b, i, 0)),
                  pl.BlockSpec((1, PROJ_TM, HY_CH), lambda b, i: (b, i, 0)),
                  pl.BlockSpec((1, PROJ_TM, RW_W), lambda b, i: (b, i, 0)),
                  pl.BlockSpec((1, PROJ_TM, NA_W), lambda b, i: (b, i, 0)),
                  pl.BlockSpec((8, N_MOD * D_MODEL), lambda b, i: (0, 0)),
                  pl.BlockSpec((8, D_MODEL), lambda b, i: (0, 0)),
                  pl.BlockSpec((D_MODEL, D_MODEL), lambda b, i: (0, 0))],
        out_specs=pl.BlockSpec((1, PROJ_TM, D_MODEL), lambda b, i: (b, i, 0)),
        out_shape=jax.ShapeDtypeStruct((BATCH, T_ALL, D_MODEL), F32),
        compiler_params=_cparams(("parallel", "parallel")),
        name="mixer_out_proj",
    )(h, o_hy, o_rw, o_na, mods_l, g_l, w_out)


def _hy_short_kernel(x_ref, w_ref, b_ref, o_ref):
    x = x_ref[0]
    n = x.shape[0]
    row = lax.broadcasted_iota(jnp.int32, (n, 1), 0)
    prev = jnp.where(row == 0, 0.0, pltpu.roll(x, 1, 0))
    nxt = jnp.where(row == n - 1, 0.0, pltpu.roll(x, n - 1, 0))
    o_ref[0] = prev * w_ref[0:1, :] + x * w_ref[1:2, :] + nxt * w_ref[2:3, :] + b_ref[...]


def _hy_short(p_hy, conv_w, conv_b, seq_len, row_block):
    lanes = 128
    return pl.pallas_call(
        _hy_short_kernel,
        grid=(BATCH, HY_IN // lanes),
        in_specs=[pl.BlockSpec((1, seq_len, lanes), lambda b, j: (b, row_block, j)),
                  pl.BlockSpec((3, lanes), lambda b, j: (0, j)),
                  pl.BlockSpec((1, lanes), lambda b, j: (0, j))],
        out_specs=pl.BlockSpec((1, seq_len, lanes), lambda b, j: (b, 0, j)),
        out_shape=jax.ShapeDtypeStruct((BATCH, seq_len, HY_IN), F32),
        compiler_params=_cparams(("parallel", "parallel")),
        name="hyena_short_conv",
    )(p_hy, conv_w, conv_b.reshape(1, HY_IN))


@functools.lru_cache(maxsize=None)
def _filter_features(seq_len):
    n = 2 * seq_len
    m = np.arange(n)
    d = np.where(m < seq_len, m, n - m).astype(np.float64)
    t = d / (seq_len - 1)
    ang = 2.0 * math.pi * d / seq_len
    bands = (HY_EMB - 1) // 2
    fr = np.linspace(1e-4, bands - 1, bands)
    z = np.zeros((n, 128), np.float64)
    z[:, 0] = t
    z[:, 1:1 + bands] = np.cos(fr[None, :] * ang[:, None])
    z[:, 1 + bands:1 + 2 * bands] = -np.sin(fr[None, :] * ang[:, None])
    z[:, 33] = (m < seq_len)
    z[:, 34] = (m > seq_len)
    z[seq_len, :] = 0.0
    return z.astype(np.float32)


@functools.lru_cache(maxsize=None)
def _filter_deltas():
    d = np.abs(np.linspace(math.log(HY_TARGET) / HY_SLOW, math.log(HY_TARGET) / HY_FAST, HY_CH))
    return d.astype(np.float32).reshape(1, HY_CH)


def _hy_filter_kernel(z_ref, w1_ref, b1_ref, w2_ref, b2_ref, w3_ref, fq_ref, dl_ref, o_ref):
    z = z_ref[...]
    fq = fq_ref[...]
    h = jnp.sin(fq * (_dot(z, w1_ref[...]) + b1_ref[...]))
    h = jnp.sin(fq * (_dot(h, w2_ref[...]) + b2_ref[...]))
    o = _dot(h, w3_ref[...])
    dec = jnp.exp(-z[:, 0:1] * dl_ref[...])
    mf = z[:, 33:34]
    mb = z[:, 34:35]
    outs = []
    for od in range(HY_ORDER):
        fwd = o[:, od * 2 * HY_CH:od * 2 * HY_CH + HY_CH]
        bwd = o[:, od * 2 * HY_CH + HY_CH:(od + 1) * 2 * HY_CH]
        outs.append((mf * fwd + mb * bwd) * dec)
    o_ref[...] = jnp.concatenate(outs, axis=1)


def _hy_filter(seq_len, w1, b1, w2, b2, w3, freq):
    n = 2 * seq_len
    tp = min(n, 1024)
    zt = jnp.asarray(_filter_features(seq_len))
    w1p = jnp.zeros((128, HY_FILT), F32).at[:HY_EMB].set(w1)
    full = lambda shape: pl.BlockSpec(shape, lambda i: (0, 0))
    return pl.pallas_call(
        _hy_filter_kernel,
        grid=(n // tp,),
        in_specs=[pl.BlockSpec((tp, 128), lambda i: (i, 0)),
                  full((128, HY_FILT)), full((1, HY_FILT)), full((HY_FILT, HY_FILT)), full((1, HY_FILT)),
                  full((HY_FILT, HY_ORDER * 2 * HY_CH)), full((1, HY_FILT)), full((1, HY_CH))],
        out_specs=pl.BlockSpec((tp, HY_ORDER * HY_CH), lambda i: (i, 0)),
        out_shape=jax.ShapeDtypeStruct((n, HY_ORDER * HY_CH), F32),
        compiler_params=_cparams(("parallel",)),
        name="hyena_filter_mlp",
    )(zt, w1p, b1.reshape(1, -1), w2, b2.reshape(1, -1), w3, freq.reshape(1, -1), jnp.asarray(_filter_deltas()))


@functools.lru_cache(maxsize=None)
def _fft_tables():
    n1 = np.arange(FFT_N1)[:, None]
    k1 = np.arange(FFT_N1)[None, :]
    th = 2.0 * math.pi * n1 * k1 / FFT_N1
    c, s = np.cos(th), np.sin(th)
    half = FFT_N1 // 2
    big_n = FFT_N1 * FFT_N2
    wa_c = np.block([[c[:half], -s[:half]], [s[:half], c[:half]]])
    wa_r = np.concatenate([c, -s], axis=1)
    tw = 2.0 * math.pi * np.arange(FFT_N2)[:, None] * np.arange(FFT_N1)[None, :] / big_n
    twf = np.concatenate([np.cos(tw), -np.sin(tw)], axis=1)
    twi = np.concatenate([np.cos(tw.T), np.sin(tw.T)], axis=1)
    wb = np.block([[c, -s], [s, c]])
    wc = np.block([[c, s], [-s, c]])
    wd = np.block([[c[:, :half], s[:, :half]], [-s[:, :half], c[:, :half]]]) / big_n
    return tuple(np.asarray(a, np.float32) for a in (wa_c, wa_r, twf, twi, wb, wc, wd))


def _fft_fwd(z, wa, tw, wb):
    ct = z.shape[0]
    a = _dot(z.reshape(ct * FFT_N2, z.shape[2]), wa).reshape(ct, FFT_N2, 2 * FFT_N1)
    ar, ai = a[:, :, :FFT_N1], a[:, :, FFT_N1:]
    tr, ti = tw[:, :FFT_N1], tw[:, FFT_N1:]
    br = ar * tr - ai * ti
    bi = ar * ti + ai * tr
    bt = jnp.concatenate([jnp.swapaxes(br, 1, 2), jnp.swapaxes(bi, 1, 2)], axis=2)
    return _dot(bt.reshape(ct * FFT_N1, 2 * FFT_N2), wb).reshape(ct, FFT_N1, 2 * FFT_N2)


def _hy_kfft_kernel(k_ref, wa_ref, tw_ref, wb_ref, o_ref):
    o_ref[...] = _fft_fwd(k_ref[...], wa_ref[...], tw_ref[...], wb_ref[...])


def _hy_long_kernel(z_ref, g_ref, kh_ref, bias_ref, wa_ref, twf_ref, wb_ref, wc_ref, twi_ref, wd_ref, o_ref):
    z = z_ref[...]
    ct = z.shape[0]
    x = _fft_fwd(z, wa_ref[...], twf_ref[...], wb_ref[...])
    kh = kh_ref[...]
    xr, xi = x[:, :, :FFT_N2], x[:, :, FFT_N2:]
    kr, ki = kh[:, :, :FFT_N2], kh[:, :, FFT_N2:]
    y = jnp.concatenate([xr * kr - xi * ki, xr * ki + xi * kr], axis=2)
    c1 = _dot(y.reshape(ct * FFT_N1, 2 * FFT_N2), wc_ref[...]).reshape(ct, FFT_N1, 2 * FFT_N2)
    cr, ci = c1[:, :, :FFT_N2], c1[:, :, FFT_N2:]
    twi = twi_ref[...]
    tr, ti = twi[:, :FFT_N2], twi[:, FFT_N2:]
    dr = cr * tr - ci * ti
    di = cr * ti + ci * tr
    dt = jnp.concatenate([jnp.swapaxes(dr, 1, 2), jnp.swapaxes(di, 1, 2)], axis=2)
    conv = _dot(dt.reshape(ct * FFT_N2, 2 * FFT_N1), wd_ref[...]).reshape(ct, FFT_N2, FFT_N1)
    o_ref[...] = (conv + bias_ref[...] * z) * g_ref[...]


def _hy_ctx_kernel(x_ref, kc_ref, bias_ref, cf_ref, sf_ref, o_ref):
    x = x_ref[0]
    cf, sf = cf_ref[...], sf_ref[...]
    n = CTX_LEN
    z = x[:, :HY_CH]
    for od in range(HY_ORDER):
        k = kc_ref[:, od * HY_CH:(od + 1) * HY_CH]
        kr, ki = _dot(cf, k), -_dot(sf, k)
        xr, xi = _dot(cf[:, :n], z), -_dot(sf[:, :n], z)
        yr = xr * kr - xi * ki
        yi = xr * ki + xi * kr
        conv = (_dot(cf[:n, :], yr) - _dot(sf[:n, :], yi)) * (1.0 / (2 * n))
        z = x[:, (od + 1) * HY_CH:(od + 2) * HY_CH] * (conv + bias_ref[od:od + 1, :] * z)
    o_ref[0] = z


@functools.lru_cache(maxsize=None)
def _ctx_dft():
    m = np.arange(2 * CTX_LEN)
    th = 2.0 * math.pi * m[:, None] * m[None, :] / (2 * CTX_LEN)
    return np.cos(th).astype(np.float32), np.sin(th).astype(np.float32)


def _hyena(p_hy, conv_w, conv_b, w1, b1, w2, b2, w3, freq, bias):
    wa_c, wa_r, twf, twi, wb, wc, wd = (jnp.asarray(a) for a in _fft_tables())
    nch = HY_ORDER * HY_CH
    kf = _hy_filter(SEQ, w1, b1, w2, b2, w3, freq)
    kt = kf.reshape(FFT_N1, FFT_N2, nch).transpose(2, 1, 0)
    const = lambda shape: pl.BlockSpec(shape, lambda j: (0,) * len(shape))
    khat = pl.pallas_call(
        _hy_kfft_kernel,
        grid=(nch // FFT_CT,),
        in_specs=[pl.BlockSpec((FFT_CT, FFT_N2, FFT_N1), lambda j: (j, 0, 0)),
                  const((FFT_N1, 2 * FFT_N1)), const((FFT_N2, 2 * FFT_N1)), const((2 * FFT_N2, 2 * FFT_N2))],
        out_specs=pl.BlockSpec((FFT_CT, FFT_N1, 2 * FFT_N2), lambda j: (j, 0, 0)),
        out_shape=jax.ShapeDtypeStruct((nch, FFT_N1, 2 * FFT_N2), F32),
        compiler_params=_cparams(("parallel",)),
        name="hyena_filter_fft",
    )(kt, wa_r, twf, wb)

    pc = _hy_short(p_hy, conv_w, conv_b, SEQ, 0)
    half = FFT_N1 // 2
    zall = pc.reshape(BATCH, half, FFT_N2, HY_IN).transpose(3, 2, 0, 1).reshape(HY_IN, FFT_N2, FFT_N1)
    nblk = HY_CH // FFT_CT
    z = zall
    for od in range(HY_ORDER):
        z_spec = pl.BlockSpec((FFT_CT, FFT_N2, FFT_N1), lambda j: (j, 0, 0))
        g_spec = pl.BlockSpec((FFT_CT, FFT_N2, FFT_N1), lambda j, od=od: ((od + 1) * nblk + j, 0, 0))
        z = pl.pallas_call(
            _hy_long_kernel,
            grid=(nblk,),
            in_specs=[z_spec, g_spec,
                      pl.BlockSpec((FFT_CT, FFT_N1, 2 * FFT_N2), lambda j, od=od: (od * nblk + j, 0, 0)),
                      pl.BlockSpec((FFT_CT, 1, 1), lambda j, od=od: (od * nblk + j, 0, 0)),
                      const((FFT_N1, 2 * FFT_N1)), const((FFT_N2, 2 * FFT_N1)), const((2 * FFT_N2, 2 * FFT_N2)),
                      const((2 * FFT_N2, 2 * FFT_N2)), const((FFT_N1, 2 * FFT_N2)), const((2 * FFT_N1, FFT_N1))],
            out_specs=pl.BlockSpec((FFT_CT, FFT_N2, FFT_N1), lambda j: (j, 0, 0)),
            out_shape=jax.ShapeDtypeStruct((HY_CH, FFT_N2, FFT_N1), F32),
            compiler_params=_cparams(("parallel",)),
            name="hyena_long_conv",
        )(z, zall, khat, bias.reshape(nch, 1, 1), wa_c, twf, wb, wc, twi, wd)
    o_lat = z.reshape(HY_CH, FFT_N2, BATCH, half).transpose(2, 3, 1, 0).reshape(BATCH, SEQ, HY_CH)

    kc = _hy_filter(CTX_LEN, w1, b1, w2, b2, w3, freq)
    pcc = _hy_short(p_hy, conv_w, conv_b, CTX_LEN, SEQ // CTX_LEN)
    cf, sf = (jnp.asarray(a) for a in _ctx_dft())
    o_ctx = pl.pallas_call(
        _hy_ctx_kernel,
        grid=(BATCH,),
        in_specs=[pl.BlockSpec((1, CTX_LEN, HY_IN), lambda b: (b, 0, 0)),
                  pl.BlockSpec((2 * CTX_LEN, nch), lambda b: (0, 0)),
                  pl.BlockSpec((HY_ORDER, HY_CH), lambda b: (0, 0)),
                  pl.BlockSpec((2 * CTX_LEN, 2 * CTX_LEN), lambda b: (0, 0)),
                  pl.BlockSpec((2 * CTX_LEN, 2 * CTX_LEN), lambda b: (0, 0))],
        out_specs=pl.BlockSpec((1, CTX_LEN, HY_CH), lambda b: (b, 0, 0)),
        out_shape=jax.ShapeDtypeStruct((BATCH, CTX_LEN, HY_CH), F32),
        compiler_params=_cparams(("parallel",)),
        name="hyena_context",
    )(pcc, kc, bias, cf, sf)
    return jnp.concatenate([o_lat, o_ctx], axis=1)


@functools.lru_cache(maxsize=None)
def _rope_tables():
    nf = RW_HD // 4
    t = np.arange(SEQ)
    row = (t // GRID_W).astype(np.float64)
    col = (t % GRID_W).astype(np.float64)
    inv = ROPE_BASE ** (-np.arange(nf, dtype=np.float64) / nf)
    ang = np.concatenate([row[:, None] * inv] * 2 + [col[:, None] * inv] * 2, axis=-1)
    cos = np.concatenate([np.cos(ang), np.ones((CTX_LEN, RW_HD))], axis=0)
    sin = np.concatenate([np.sin(ang), np.zeros((CTX_LEN, RW_HD))], axis=0)
    return (np.tile(cos, (1, RW_HEADS)).astype(np.float32), np.tile(sin, (1, RW_HEADS)).astype(np.float32))


@functools.lru_cache(maxsize=None)
def _head_blockdiag():
    h = np.arange(RW_W) // RW_HD
    return (h[:, None] == h[None, :]).astype(np.float32)


@functools.lru_cache(maxsize=None)
def _chunk_tri():
    t = np.arange(RW_TR)[:, None]
    s = np.arange(RW_TR)[None, :]
    same = (t // RW_C) == (s // RW_C)
    return np.stack([same & (s <= t), same & (s >= t)]).astype(np.float32)


def _rope(x, cos, sin):
    lane = lax.broadcasted_iota(jnp.int32, (1, RW_W), 1)
    first = (lane % (RW_HD // 2)) < (RW_HD // 4)
    rot = jnp.where(first, -pltpu.roll(x, RW_W - RW_HD // 4, 1), pltpu.roll(x, RW_HD // 4, 1))
    return x * cos + rot * sin


def _rw_prep_kernel(p_ref, pv_ref, nx_ref, mu_ref, w0_ref, w2_ref, a0_ref, a2_ref, g2_ref, kk_ref, ka_ref, rk_ref,
                    cos_ref, sin_ref, bd_ref, tri_ref,
                    v_ref, bon_ref, g_ref, at0_ref, rt0_ref, bt0_ref, kt0_ref, wc0_ref,
                    at1_ref, rt1_ref, bt1_ref, kt1_ref, wc1_ref):
    i = pl.program_id(1)
    x = p_ref[0]
    row = lax.broadcasted_iota(jnp.int32, (RW_TR, 1), 0)
    t = i * RW_TR + row
    seq_first = (t == 0) | (t == SEQ)
    seq_last = (t == SEQ - 1) | (t == T_ALL - 1)
    prev = jnp.where(row == 0, pv_ref[0, 7:8, :], pltpu.roll(x, 1, 0))
    prev = jnp.where(seq_first, 0.0, prev)
    nxt = jnp.where(row == RW_TR - 1, nx_ref[0, 0:1, :], pltpu.roll(x, RW_TR - 1, 0))
    nxt = jnp.where(seq_last, 0.0, nxt)
    xs = x + mu_ref[0:1, :] * (prev - x) + mu_ref[1:2, :] * (nxt - x)
    r = xs[:, 0:RW_W]
    k = xs[:, RW_W:2 * RW_W]
    v = xs[:, 2 * RW_W:3 * RW_W]
    i1 = 3 * RW_W
    wd = jnp.tanh(xs[:, i1:i1 + 2 * RW_LORA])
    ad = xs[:, i1 + 2 * RW_LORA:i1 + 4 * RW_LORA]
    gd = xs[:, i1 + 4 * RW_LORA:]
    cos, sin, bd = cos_ref[...], sin_ref[...], bd_ref[...]
    g_ref[0] = _dot(jax.nn.sigmoid(gd), g2_ref[...])
    kk = k * kk_ref[...]
    nrm = jnp.sqrt(_dot(kk * kk, bd))
    kk = kk / jnp.maximum(nrm, 1e-12)
    ksum = jnp.zeros_like(k)
    r_s = _rope(r, cos, sin)
    na_s = -_rope(kk, cos, sin)
    n_chunks = RW_TR // RW_C
    outs = ((at0_ref, rt0_ref, bt0_ref, kt0_ref, wc0_ref), (at1_ref, rt1_ref, bt1_ref, kt1_ref, wc1_ref))
    for d, (at_ref, rt_ref, bt_ref, kt_ref, wc_ref) in enumerate(outs):
        logit = w0_ref[d:d + 1, :] + _dot(wd, w2_ref[d])
        logw = -jax.nn.softplus(-logit) - 0.5
        lw = -jnp.exp(logw)
        a = jax.nn.sigmoid(a0_ref[d:d + 1, :] + _dot(ad, a2_ref[d]))
        kdir = k * (1.0 + (a - 1.0) * ka_ref[...])
        ksum = ksum + kdir
        cum = _dot(tri_ref[d], lw)
        e_neg = jnp.exp(-cum)
        at_ref[0] = na_s * jnp.exp(cum - lw)
        rt_ref[0] = r_s * jnp.exp(cum)
        bt_ref[0] = _rope(kk * a, cos, sin) * e_neg
        kt_ref[0] = _rope(kdir, cos, sin) * e_neg
        tot = jnp.sum(lw.reshape(n_chunks, RW_C, RW_W), axis=1, keepdims=True)
        wc_ref[0] = jnp.broadcast_to(jnp.exp(tot), (n_chunks, 8, RW_W))
    bon_ref[0] = _dot(r * ksum * rk_ref[...], bd)
    v_ref[0] = v


def _split(x):
    hi = x.astype(BF16)
    return hi, (x - hi.astype(F32)).astype(BF16)


def _bmm3(spec, a, b):
    (ah, am), (bh, bm) = a, b
    e = functools.partial(jnp.einsum, spec, preferred_element_type=F32)
    return e(ah, bh) + e(am, bh) + e(ah, bm)


def _rw_scan_kernel(v_ref, at_ref, rt_ref, bt_ref, kt_ref, wc_ref, y_ref, s_sc):
    d = pl.program_id(0)
    step = pl.program_id(1)
    nb = BATCH * RW_HEADS
    c = RW_C

    @pl.when(step == 0)
    def _():
        s_sc[...] = jnp.zeros_like(s_sc)

    sh = (nb, c, RW_HD)
    v = v_ref[...].reshape(sh)
    ar = jnp.concatenate([at_ref[...].reshape(sh), rt_ref[...].reshape(sh)], axis=1)
    bt = bt_ref[...].reshape(sh)
    kt = kt_ref[...].reshape(sh)
    wc = wc_ref[...].reshape(nb, 8, RW_HD)[:, 0:1, :]

    ti = lax.broadcasted_iota(jnp.int32, (c, c), 0)
    si = lax.broadcasted_iota(jnp.int32, (c, c), 1)
    ahead = (ti - si) * jnp.where(d == 0, 1, -1)
    strict = (ahead > 0)[None]
    incl = (ahead >= 0)[None]
    eye = (si == ti).astype(F32)[None]

    ar_s, bt_s, kt_s, v_s = _split(ar), _split(bt), _split(kt), _split(v)
    lb = _bmm3('htj,hsj->hts', ar_s, bt_s)
    lk = _bmm3('htj,hsj->hts', ar_s, kt_s)
    lab = jnp.where(strict, lb[:, :c], 0.0)
    lrb = jnp.where(incl, lb[:, c:], 0.0)
    lak = jnp.where(strict, lk[:, :c], 0.0)
    lrk = jnp.where(incl, lk[:, c:], 0.0)

    tt = eye + lab
    pw = lab
    for _ in range(int(math.log2(c)) - 1):
        pw_s = _split(pw)
        pw = _bmm3('hts,hsu->htu', pw_s, pw_s)
        tt = tt + _bmm3('hts,hsu->htu', _split(tt), _split(pw))

    s0 = s_sc[...]
    ars = _bmm3('htk,hvk->htv', ar_s, _split(s0))
    lv = _bmm3('hts,hsv->htv', _split(jnp.concatenate([lak, lrk], axis=1)), v_s)
    u = _bmm3('hts,hsv->htv', _split(tt), _split(lv[:, :c] + ars[:, :c]))
    y = ars[:, c:] + _bmm3('hts,hsv->htv', _split(lrb), _split(u)) + lv[:, c:]
    y_ref[...] = y.reshape(y_ref.shape)
    uv = jnp.concatenate([u, v], axis=1)
    bk = jnp.concatenate([bt, kt], axis=1)
    s_sc[...] = (s0 + _bmm3('htv,htj->hvj', _split(uv), _split(bk))) * wc


def _rw_out_kernel(yf_ref, yb_ref, bon_ref, v_ref, g_ref, lnw_ref, lnb_ref, bd_ref, o_ref):
    y = yf_ref[0, 0] + yb_ref[0, 0]
    bd = bd_ref[...]
    mean = _dot(y, bd) * (1.0 / RW_HD)
    yc = y - mean
    var = _dot(yc * yc, bd) * (1.0 / RW_HD)
    yn = yc * lax.rsqrt(var + RW_GN_EPS) * lnw_ref[...] + lnb_ref[...]
    o_ref[0] = (yn + bon_ref[0] * v_ref[0]) * g_ref[0]


def _rw_chunk_index(d, s):
    n_lat = SEQ // RW_C
    n_all = T_ALL // RW_C
    fwd = jnp.where(s < n_all - n_lat, n_lat + s, s - (n_all - n_lat))
    return jnp.where(d == 0, fwd, n_all - 1 - s)


def _rwkv(p_rw, mu, w0, w2, a0, a2, g2, k_k, k_a, r_k, ln_w, ln_b):
    cos, sin = (jnp.asarray(a) for a in _rope_tables())
    bd = jnp.asarray(_head_blockdiag())
    zpad = lambda w: jnp.stack([jnp.concatenate([w[0], jnp.zeros_like(w[0])], axis=0),
                                jnp.concatenate([jnp.zeros_like(w[1]), w[1]], axis=0)])
    nt = T_ALL // RW_TR
    n_ch = T_ALL // RW_C
    tok_shape = jax.ShapeDtypeStruct((BATCH, T_ALL, RW_W), F32)
    wc_shape = jax.ShapeDtypeStruct((BATCH, n_ch, 8, RW_W), F32)
    wc_spec = pl.BlockSpec((1, RW_TR // RW_C, 8, RW_W), lambda b, i: (b, i, 0, 0))
    tok = lambda w: pl.BlockSpec((1, RW_TR, w), lambda b, i: (b, i, 0))
    const = lambda shape: pl.BlockSpec(shape, lambda b, i: (0,) * len(shape))
    row = lambda a: a.reshape(1, RW_W)
    outs = pl.pallas_call(
        _rw_prep_kernel,
        grid=(BATCH, nt),
        in_specs=[tok(RW_IN),
                  pl.BlockSpec((1, 8, RW_IN), lambda b, i: (b, jnp.maximum(i * (RW_TR // 8) - 1, 0), 0)),
                  pl.BlockSpec((1, 8, RW_IN), lambda b, i: (b, jnp.minimum((i + 1) * (RW_TR // 8), T_ALL // 8 - 1), 0)),
                  const((2, RW_IN)), const((2, RW_W)), const((2, 2 * RW_LORA, RW_W)), const((2, RW_W)),
                  const((2, 2 * RW_LORA, RW_W)), const((RW_GATE_LORA, RW_W)),
                  const((1, RW_W)), const((1, RW_W)), const((1, RW_W)),
                  pl.BlockSpec((RW_TR, RW_W), lambda b, i: (i, 0)), pl.BlockSpec((RW_TR, RW_W), lambda b, i: (i, 0)),
                  const((RW_W, RW_W)), const((2, RW_TR, RW_TR))],
        out_specs=[tok(RW_W)] * 3 + ([tok(RW_W)] * 4 + [wc_spec]) * 2,
        out_shape=[tok_shape] * 3 + ([tok_shape] * 4 + [wc_shape]) * 2,
        compiler_params=_cparams(("parallel", "parallel")),
        name="rwkv_prepare",
    )(p_rw, p_rw, p_rw, mu, w0, zpad(w2), a0, zpad(a2), g2, row(k_k), row(k_a), row(r_k.reshape(-1)), cos, sin, bd,
      jnp.asarray(_chunk_tri()))
    v, bonus, g, at0, rt0, bt0, kt0, wc0, at1, rt1, bt1, kt1, wc1 = outs

    heads = lambda x: x.reshape(BATCH, T_ALL, RW_HEADS, RW_HD).transpose(0, 2, 1, 3)
    heads2 = lambda x0, x1: jnp.stack([heads(x0), heads(x1)])
    wch = lambda w: w.reshape(BATCH, n_ch, 8, RW_HEADS, RW_HD).transpose(0, 3, 1, 2, 4)
    shared = pl.BlockSpec((BATCH, RW_HEADS, RW_C, RW_HD), lambda d, s: (0, 0, _rw_chunk_index(d, s), 0))
    perdir = pl.BlockSpec((1, BATCH, RW_HEADS, RW_C, RW_HD), lambda d, s: (d, 0, 0, _rw_chunk_index(d, s), 0))
    y = pl.pallas_call(
        _rw_scan_kernel,
        grid=(2, n_ch),
        in_specs=[shared, perdir, perdir, perdir, perdir,
                  pl.BlockSpec((1, BATCH, RW_HEADS, 1, 8, RW_HD), lambda d, s: (d, 0, 0, _rw_chunk_index(d, s), 0, 0))],
        out_specs=perdir,
        out_shape=jax.ShapeDtypeStruct((2, BATCH, RW_HEADS, T_ALL, RW_HD), F32),
        scratch_shapes=[pltpu.VMEM((BATCH * RW_HEADS, RW_HD, RW_HD), F32)],
        compiler_params=_cparams(("parallel", "arbitrary")),
        name="rwkv_chunk_scan",
    )(heads(v), heads2(at0, at1), heads2(rt0, rt1), heads2(bt0, bt1), heads2(kt0, kt1),
      jnp.stack([wch(wc0), wch(wc1)]))
    yt = y.transpose(0, 1, 3, 2, 4).reshape(2, BATCH, T_ALL, RW_W)

    tok2 = lambda w: pl.BlockSpec((1, PROJ_TM, w), lambda b, i: (b, i, 0))
    return pl.pallas_call(
        _rw_out_kernel,
        grid=(BATCH, T_ALL // PROJ_TM),
        in_specs=[pl.BlockSpec((1, 1, PROJ_TM, RW_W), lambda b, i: (0, b, i, 0)),
                  pl.BlockSpec((1, 1, PROJ_TM, RW_W), lambda b, i: (1, b, i, 0)),
                  tok2(RW_W), tok2(RW_W), tok2(RW_W),
                  const((1, RW_W)), const((1, RW_W)), const((RW_W, RW_W))],
        out_specs=tok2(RW_W),
        out_shape=jax.ShapeDtypeStruct((BATCH, T_ALL, RW_W), F32),
        compiler_params=_cparams(("parallel", "parallel")),
        name="rwkv_output",
    )(yt, yt, bonus, v, g, row(ln_w), row(ln_b), bd)


@functools.lru_cache(maxsize=None)
def _na_bias_tables():
    qc = np.arange(GRID_W)[:, None]
    kc = np.arange(GRID_W)[None, :]
    win0 = np.clip(qc - WIN_COLS // 2, 0, GRID_W - WIN_COLS)
    valid = (kc >= win0) & (kc < win0 + WIN_COLS)
    off = np.clip(kc - qc + WIN_COLS - 1, 0, 2 * WIN_COLS - 2)
    onehot = np.zeros((128, GRID_W * GRID_W), np.float32)
    onehot[off.reshape(-1), np.arange(GRID_W * GRID_W)] = valid.reshape(-1)
    mask = np.where(valid, 0.0, NEG_INF).astype(np.float32).reshape(1, GRID_W * GRID_W)
    return onehot, mask


def _na_bias_kernel(rpb_ref, oh_ref, mask_ref, o_ref):
    o_ref[...] = _dot(rpb_ref[...], oh_ref[...]) + mask_ref[...]


def _na_lat_kernel(q_ref, k_ref, v_ref, tp_ref, o_ref):
    i = pl.program_id(1)
    start = jnp.clip(i - WIN_ROWS // 2, 0, N_ROWS - WIN_ROWS)
    dr0 = start - i + WIN_ROWS - 1
    scale = NA_HD ** -0.5
    lane = lax.broadcasted_iota(jnp.int32, (1, NA_W), 1)
    hmask = [(lane // NA_HD) == h for h in range(NA_HEADS)]
    q = q_ref[0]
    qm = jnp.concatenate([jnp.where(hmask[h], q, jnp.zeros_like(q)) for h in range(NA_HEADS)], axis=0)
    row0 = pl.multiple_of(start * GRID_W, GRID_W)
    kl = k_ref[0, pl.ds(row0, WIN_ROWS * GRID_W), :]
    vl = v_ref[0, pl.ds(row0, WIN_ROWS * GRID_W), :]
    kc = k_ref[0, SEQ:T_ALL, :]
    vc = v_ref[0, SEQ:T_ALL, :]
    nt = (((1,), (1,)), ((), ()))
    bias = jnp.concatenate(
        [jnp.concatenate([tp_ref[h, pl.ds(dr0 + 2 * j, 1)][0] for j in range(WIN_ROWS // 2)], axis=1)
         for h in range(NA_HEADS)], axis=0)
    s_loc = lax.dot_general(qm, kl, nt, preferred_element_type=F32) * scale + bias
    s_ctx = lax.dot_general(qm, kc, nt, preferred_element_type=F32) * scale
    m = jnp.maximum(jnp.max(s_loc, axis=-1, keepdims=True), jnp.max(s_ctx, axis=-1, keepdims=True))
    p_loc = jnp.exp(s_loc - m)
    p_ctx = jnp.exp(s_ctx - m)
    inv = 1.0 / (jnp.sum(p_loc, axis=-1, keepdims=True) + jnp.sum(p_ctx, axis=-1, keepdims=True))
    o_all = (jnp.dot((p_loc * inv).astype(BF16), vl, preferred_element_type=F32)
             + jnp.dot((p_ctx * inv).astype(BF16), vc, preferred_element_type=F32))
    o = jnp.zeros((GRID_W, NA_W), F32)
    for h in range(NA_HEADS):
        o = o + jnp.where(hmask[h], o_all[h * GRID_W:(h + 1) * GRID_W, :], 0.0)
    o_ref[0] = o


def _na_ctx_kernel(q_ref, k_ref, v_ref, o_ref):
    scale = NA_HD ** -0.5
    lane = lax.broadcasted_iota(jnp.int32, (1, NA_W), 1)
    hmask = [(lane // NA_HD) == h for h in range(NA_HEADS)]
    q, k, v = q_ref[0], k_ref[0], v_ref[0]
    qm = jnp.concatenate([jnp.where(hmask[h], q, jnp.zeros_like(q)) for h in range(NA_HEADS)], axis=0)
    s = lax.dot_general(qm, k, (((1,), (1,)), ((), ())), preferred_element_type=F32) * scale
    p = jnp.exp(s - jnp.max(s, axis=-1, keepdims=True))
    p = p / jnp.sum(p, axis=-1, keepdims=True)
    o_all = jnp.dot(p.astype(BF16), v, preferred_element_type=F32)
    o = jnp.zeros((CTX_LEN, NA_W), F32)
    for h in range(NA_HEADS):
        o = o + jnp.where(hmask[h], o_all[h * CTX_LEN:(h + 1) * CTX_LEN, :], 0.0)
    o_ref[0] = o


def _natten(p_na, rpb):
    onehot, mask = (jnp.asarray(a) for a in _na_bias_tables())
    n_dr = 2 * WIN_ROWS - 1
    n_rel = 2 * WIN_COLS - 1
    rows = NA_HEADS * n_dr
    rows_p = -(-rows // 8) * 8
    rpb2 = jnp.zeros((rows_p, 128), F32).at[:rows, :n_rel].set(rpb.reshape(rows, n_rel))
    tb = pl.pallas_call(
        _na_bias_kernel,
        out_shape=jax.ShapeDtypeStruct((rows_p, GRID_W * GRID_W), F32),
        name="natten_bias_expand",
    )(rpb2, onehot, mask)
    tb = tb[:rows].reshape(NA_HEADS, n_dr, GRID_W, GRID_W)
    nxt = jnp.concatenate([tb[:, 1:], jnp.full_like(tb[:, :1], NEG_INF)], axis=1)
    tp = jnp.concatenate([tb, nxt], axis=-1)

    o_lat = pl.pallas_call(
        _na_lat_kernel,
        grid=(BATCH, N_ROWS),
        in_specs=[pl.BlockSpec((1, GRID_W, NA_W), lambda b, i: (b, i, 0)),
                  pl.BlockSpec((1, T_ALL, NA_W), lambda b, i: (b, 0, 1)),
                  pl.BlockSpec((1, T_ALL, NA_W), lambda b, i: (b, 0, 2)),
                  pl.BlockSpec((NA_HEADS, n_dr, GRID_W, 2 * GRID_W), lambda b, i: (0, 0, 0, 0))],
        out_specs=pl.BlockSpec((1, GRID_W, NA_W), lambda b, i: (b, i, 0)),
        out_shape=jax.ShapeDtypeStruct((BATCH, SEQ, NA_W), F32),
        compiler_params=_cparams(("parallel", "parallel")),
        name="natten_latent",
    )(p_na, p_na, p_na, tp)

    cblk = SEQ // CTX_LEN
    o_ctx = pl.pallas_call(
        _na_ctx_kernel,
        grid=(BATCH,),
        in_specs=[pl.BlockSpec((1, CTX_LEN, NA_W), lambda b: (b, cblk, 0)),
                  pl.BlockSpec((1, CTX_LEN, NA_W), lambda b: (b, cblk, 1)),
                  pl.BlockSpec((1, CTX_LEN, NA_W), lambda b: (b, cblk, 2))],
        out_specs=pl.BlockSpec((1, CTX_LEN, NA_W), lambda b: (b, 0, 0)),
        out_shape=jax.ShapeDtypeStruct((BATCH, CTX_LEN, NA_W), F32),
        compiler_params=_cparams(("parallel",)),
        name="natten_context",
    )(p_na, p_na, p_na)
    return jnp.concatenate([o_lat, o_ctx], axis=1)


def kernel(x, c, ctx, c_ctx, mod_w, mod_b, norm_g, ffn1_wgu, ffn1_wdn, ffn2_wgu, ffn2_wdn, w_in, w_out, hy_conv_w, hy_conv_b, hy_f_w1, hy_f_b1, hy_f_w2, hy_f_b2, hy_f_w3, hy_freq, hy_bias, rw_mu, rw_w0, rw_w2, rw_a0, rw_a2, rw_g2, rw_k_k, rw_k_a, rw_r_k, rw_ln_w, rw_ln_b, na_rpb):
    assert x.shape == (BATCH, SEQ, D_MODEL) and ctx.shape == (BATCH, CTX_LEN, D_MODEL)
    mods = _modulation(c, c_ctx, mod_w, mod_b)
    h = jnp.concatenate([x, ctx], axis=1)
    gpad = jnp.concatenate([norm_g, jnp.zeros((DEPTH, 2, D_MODEL), F32)], axis=1)
    for l in range(DEPTH):
        m_l, g_l = mods[l], gpad[l]
        h = _ffn(h, m_l, g_l, ffn1_wgu[l].astype(BF16), ffn1_wdn[l].astype(BF16), 0)
        p_hy, p_rw, p_na = _in_proj(h, m_l, g_l, w_in[l].astype(BF16))
        o_hy = _hyena(p_hy, hy_conv_w[l], hy_conv_b[l], hy_f_w1[l], hy_f_b1[l], hy_f_w2[l], hy_f_b2[l],
                      hy_f_w3[l], hy_freq[l], hy_bias[l])
        o_rw = _rwkv(p_rw, rw_mu[l], rw_w0[l], rw_w2[l], rw_a0[l], rw_a2[l], rw_g2[l], rw_k_k[l], rw_k_a[l],
                     rw_r_k[l], rw_ln_w[l], rw_ln_b[l])
        o_na = _natten(p_na, na_rpb[l])
        h = _out_proj(h, o_hy, o_rw, o_na, m_l, g_l, w_out[l].astype(BF16))
        h = _ffn(h, m_l, g_l, ffn2_wgu[l].astype(BF16), ffn2_wdn[l].astype(BF16), 1)
    return h[:, :SEQ]
```

```python
import functools
import math

import numpy as np
import jax
import jax.numpy as jnp
from jax import lax
from jax.experimental import pallas as pl
from jax.experimental.pallas import tpu as pltpu

D_MODEL = 1024
BATCH = 2
SEQ = 8192
DEPTH = 4
GRID_W = 64
CTX_LEN = 256
T_ALL = SEQ + CTX_LEN
N_MOD = 9
D_FF = 2816
MACARON_W = 0.5
NORM_EPS = 1e-6
NEG_INF = -1e30

HY_CH = 256
HY_ORDER = 2
HY_EMB = 33
HY_FILT = 64
HY_TARGET = 1e-2
HY_FAST = 0.3
HY_SLOW = 1.5
HY_IN = (HY_ORDER + 1) * HY_CH

RW_HEADS = 6
RW_HD = 64
RW_W = RW_HEADS * RW_HD
RW_LORA = 64
RW_GATE_LORA = 128
RW_GN_EPS = 64e-5
ROPE_BASE = 10000.0
RW_IN = 3 * RW_W + 4 * RW_LORA + RW_GATE_LORA

NA_HEADS = 6
NA_HD = 64
NA_W = NA_HEADS * NA_HD
WIN_ROWS = 8
WIN_COLS = 16
NA_IN = 3 * NA_W
N_ROWS = SEQ // GRID_W

F32 = jnp.float32
BF16 = jnp.bfloat16
HI = lax.Precision.HIGHEST

VMEM_LIMIT = 56 * 1024 * 1024

FFN_TM = 768
FFN_TF = 1408
PROJ_TM = 384
RW_TR = 256
RW_C = 64
FFT_N1 = 128
FFT_N2 = 128
FFT_CT = 16


def _dot(a, b):
    return jnp.dot(a, b, precision=HI, preferred_element_type=F32)


def _split(x):
    hi = x.astype(BF16)
    return hi, (x - hi.astype(F32)).astype(BF16)


def _dot3(x, w_ref):
    xh, xm = _split(x)
    wh, wm = w_ref[0], w_ref[1]
    d = functools.partial(jnp.dot, preferred_element_type=F32)
    return d(xh, wh) + d(xm, wh) + d(xh, wm)


def _cparams(sem, vmem=VMEM_LIMIT):
    return pltpu.CompilerParams(dimension_semantics=sem, vmem_limit_bytes=vmem)


def _mod_kernel(cv_ref, w_ref, b_ref, o_ref):
    cv = cv_ref[...]
    s = cv * jax.nn.sigmoid(cv)
    o_ref[0] = _dot(s, w_ref[0]) + b_ref[0]


def _modulation(c, c_ctx, mod_w, mod_b):
    cv = jnp.concatenate([c, c_ctx[None, :], jnp.zeros((8 - BATCH - 1, D_MODEL), F32)], axis=0)
    tn = 1536
    return pl.pallas_call(
        _mod_kernel,
        grid=(DEPTH, N_MOD * D_MODEL // tn),
        in_specs=[pl.BlockSpec((8, D_MODEL), lambda l, j: (0, 0)),
                  pl.BlockSpec((1, D_MODEL, tn), lambda l, j: (l, 0, j)),
                  pl.BlockSpec((1, 1, tn), lambda l, j: (l, 0, j))],
        out_specs=pl.BlockSpec((1, 8, tn), lambda l, j: (l, 0, j)),
        out_shape=jax.ShapeDtypeStruct((DEPTH, 8, N_MOD * D_MODEL), F32),
        compiler_params=_cparams(("parallel", "parallel")),
        name="adaln_modulation",
    )(cv, mod_w, mod_b.reshape(DEPTH, 1, N_MOD * D_MODEL))


def _mod_rows(m_ref, b, i, tm, idx):
    t = i * tm + lax.broadcasted_iota(jnp.int32, (tm, 1), 0)
    ml = m_ref[pl.ds(b, 1), idx * D_MODEL:(idx + 1) * D_MODEL]
    mc = m_ref[BATCH:BATCH + 1, idx * D_MODEL:(idx + 1) * D_MODEL]
    return jnp.where(t >= SEQ, mc, ml)


def _rms(x, g):
    return x * lax.rsqrt(jnp.mean(x * x, axis=-1, keepdims=True) + NORM_EPS) * g


def _ffn_kernel(h_ref, m_ref, g_ref, wg_ref, wu_ref, wd_ref, o_ref, u_sc, acc_sc, *, mi, gi):
    b, i, k = pl.program_id(0), pl.program_id(1), pl.program_id(2)

    @pl.when(k == 0)
    def _():
        y = _rms(h_ref[0], g_ref[gi:gi + 1, :])
        u = y * (1.0 + _mod_rows(m_ref, b, i, FFN_TM, mi + 1)) + _mod_rows(m_ref, b, i, FFN_TM, mi)
        u_sc[...] = u.astype(BF16)
        acc_sc[...] = jnp.zeros_like(acc_sc)

    u = u_sc[...]
    gate = jnp.dot(u, wg_ref[...], preferred_element_type=F32)
    up = jnp.dot(u, wu_ref[...], preferred_element_type=F32)
    a = (gate * jax.nn.sigmoid(gate) * up).astype(BF16)
    acc_sc[...] += jnp.dot(a, wd_ref[...], preferred_element_type=F32)

    @pl.when(k == pl.num_programs(2) - 1)
    def _():
        y = _rms(acc_sc[...], g_ref[gi + 1:gi + 2, :])
        o_ref[0] = h_ref[0] + MACARON_W * _mod_rows(m_ref, b, i, FFN_TM, mi + 2) * y


def _ffn(h, mods_l, g_l, wgu, wdn, which):
    mi, gi = (0, 0) if which == 0 else (6, 4)
    nk = D_FF // FFN_TF
    return pl.pallas_call(
        functools.partial(_ffn_kernel, mi=mi, gi=gi),
        grid=(BATCH, T_ALL // FFN_TM, nk),
        in_specs=[pl.BlockSpec((1, FFN_TM, D_MODEL), lambda b, i, k: (b, i, 0)),
                  pl.BlockSpec((8, N_MOD * D_MODEL), lambda b, i, k: (0, 0)),
                  pl.BlockSpec((8, D_MODEL), lambda b, i, k: (0, 0)),
                  pl.BlockSpec((D_MODEL, FFN_TF), lambda b, i, k: (0, k)),
                  pl.BlockSpec((D_MODEL, FFN_TF), lambda b, i, k: (0, k + D_FF // FFN_TF)),
                  pl.BlockSpec((FFN_TF, D_MODEL), lambda b, i, k: (k, 0))],
        out_specs=pl.BlockSpec((1, FFN_TM, D_MODEL), lambda b, i, k: (b, i, 0)),
        out_shape=jax.ShapeDtypeStruct((BATCH, T_ALL, D_MODEL), F32),
        scratch_shapes=[pltpu.VMEM((FFN_TM, D_MODEL), BF16), pltpu.VMEM((FFN_TM, D_MODEL), F32)],
        compiler_params=_cparams(("parallel", "parallel", "arbitrary")),
        name="ffn_sublayer",
    )(h, mods_l, g_l, wgu, wgu, wdn)


def _inproj_kernel(h_ref, m_ref, g_ref, w_ref, hy_ref, rw_ref, na_ref):
    b, i = pl.program_id(0), pl.program_id(1)
    y = _rms(h_ref[0], g_ref[2:3, :])
    u = y * (1.0 + _mod_rows(m_ref, b, i, PROJ_TM, 4)) + _mod_rows(m_ref, b, i, PROJ_TM, 3)
    p = jnp.dot(u.astype(BF16), w_ref[...], preferred_element_type=F32)
    hy_ref[0] = p[:, :HY_IN]
    rw_ref[0] = p[:, HY_IN:HY_IN + RW_IN]
    na_ref[0] = p[:, HY_IN + RW_IN:].astype(BF16)


def _in_proj(h, mods_l, g_l, w_in):
    n_in = HY_IN + RW_IN + NA_IN
    return pl.pallas_call(
        _inproj_kernel,
        grid=(BATCH, T_ALL // PROJ_TM),
        in_specs=[pl.BlockSpec((1, PROJ_TM, D_MODEL), lambda b, i: (b, i, 0)),
                  pl.BlockSpec((8, N_MOD * D_MODEL), lambda b, i: (0, 0)),
                  pl.BlockSpec((8, D_MODEL), lambda b, i: (0, 0)),
                  pl.BlockSpec((D_MODEL, n_in), lambda b, i: (0, 0))],
        out_specs=[pl.BlockSpec((1, PROJ_TM, HY_IN), lambda b, i: (b, i, 0)),
                   pl.BlockSpec((1, PROJ_TM, RW_IN), lambda b, i: (b, i, 0)),
                   pl.BlockSpec((1, PROJ_TM, NA_IN), lambda b, i: (b, i, 0))],
        out_shape=[jax.ShapeDtypeStruct((BATCH, T_ALL, HY_IN), F32),
                   jax.ShapeDtypeStruct((BATCH, T_ALL, RW_IN), F32),
                   jax.ShapeDtypeStruct((BATCH, T_ALL, NA_IN), BF16)],
        compiler_params=_cparams(("parallel", "parallel")),
        name="mixer_in_proj",
    )(h, mods_l, g_l, w_in)


def _outproj_kernel(h_ref, hy_ref, rw_ref, na_ref, m_ref, g_ref, w_ref, o_ref):
    b, i = pl.program_id(0), pl.program_id(1)
    cat = jnp.concatenate([hy_ref[0], rw_ref[0], na_ref[0]], axis=-1).astype(BF16)
    o = jnp.dot(cat, w_ref[...], preferred_element_type=F32)
    o_ref[0] = h_ref[0] + _mod_rows(m_ref, b, i, PROJ_TM, 5) * _rms(o, g_ref[3:4, :])


def _out_proj(h, o_hy, o_rw, o_na, mods_l, g_l, w_out):
    return pl.pallas_call(
        _outproj_kernel,
        grid=(BATCH, T_ALL // PROJ_TM),
        in_specs=[pl.BlockSpec((1, PROJ_TM, D_MODEL), lambda b, i: (b, i, 0)),
                  pl.BlockSpec((1, PROJ_TM, HY_CH), lambda b, i: (b, i, 0)),
                  pl.BlockSpec((1, PROJ_TM, RW_W), lambda b, i: (b, i, 0)),
                  pl.BlockSpec((1, PROJ_TM, NA_W), lambda b, i: (b, i, 0)),
                  pl.BlockSpec((8, N_MOD * D_MODEL), lambda b, i: (0, 0)),
                  pl.BlockSpec((8, D_MODEL), lambda b, i: (0, 0)),
                  pl.BlockSpec((D_MODEL, D_MODEL), lambda b, i: (0, 0))],
        out_specs=pl.BlockSpec((1, PROJ_TM, D_MODEL), lambda b, i: (b, i, 0)),
        out_shape=jax.ShapeDtypeStruct((BATCH, T_ALL, D_MODEL), F32),
        compiler_params=_cparams(("parallel", "parallel")),
        name="mixer_out_proj",
    )(h, o_hy, o_rw, o_na, mods_l, g_l, w_out)


def _hy_short_kernel(x_ref, w_ref, b_ref, o_ref):
    x = x_ref[0]
    n = x.shape[0]
    row = lax.broadcasted_iota(jnp.int32, (n, 1), 0)
    prev = jnp.where(row == 0, 0.0, pltpu.roll(x, 1, 0))
    nxt = jnp.where(row == n - 1, 0.0, pltpu.roll(x, n - 1, 0))
    o_ref[0] = prev * w_ref[0:1, :] + x * w_ref[1:2, :] + nxt * w_ref[2:3, :] + b_ref[...]


def _hy_short(p_hy, conv_w, conv_b, seq_len, row_block):
    lanes = 128
    return pl.pallas_call(
        _hy_short_kernel,
        grid=(BATCH, HY_IN // lanes),
        in_specs=[pl.BlockSpec((1, seq_len, lanes), lambda b, j: (b, row_block, j)),
                  pl.BlockSpec((3, lanes), lambda b, j: (0, j)),
                  pl.BlockSpec((1, lanes), lambda b, j: (0, j))],
        out_specs=pl.BlockSpec((1, seq_len, lanes), lambda b, j: (b, 0, j)),
        out_shape=jax.ShapeDtypeStruct((BATCH, seq_len, HY_IN), F32),
        compiler_params=_cparams(("parallel", "parallel")),
        name="hyena_short_conv",
    )(p_hy, conv_w, conv_b.reshape(1, HY_IN))


@functools.lru_cache(maxsize=None)
def _filter_features(seq_len):
    n = 2 * seq_len
    m = np.arange(n)
    d = np.where(m < seq_len, m, n - m).astype(np.float64)
    t = d / (seq_len - 1)
    ang = 2.0 * math.pi * d / seq_len
    bands = (HY_EMB - 1) // 2
    fr = np.linspace(1e-4, bands - 1, bands)
    z = np.zeros((n, 128), np.float64)
    z[:, 0] = t
    z[:, 1:1 + bands] = np.cos(fr[None, :] * ang[:, None])
    z[:, 1 + bands:1 + 2 * bands] = -np.sin(fr[None, :] * ang[:, None])
    z[:, 33] = (m < seq_len)
    z[:, 34] = (m > seq_len)
    z[seq_len, :] = 0.0
    return z.astype(np.float32)


@functools.lru_cache(maxsize=None)
def _filter_deltas():
    d = np.abs(np.linspace(math.log(HY_TARGET) / HY_SLOW, math.log(HY_TARGET) / HY_FAST, HY_CH))
    return d.astype(np.float32).reshape(1, HY_CH)


def _hy_filter_kernel(z_ref, w1_ref, b1_ref, w2_ref, b2_ref, w3_ref, fq_ref, dl_ref, o_ref):
    z = z_ref[...]
    fq = fq_ref[...]
    h = jnp.sin(fq * (_dot(z, w1_ref[...]) + b1_ref[...]))
    h = jnp.sin(fq * (_dot(h, w2_ref[...]) + b2_ref[...]))
    o = _dot(h, w3_ref[...])
    dec = jnp.exp(-z[:, 0:1] * dl_ref[...])
    mf = z[:, 33:34]
    mb = z[:, 34:35]
    outs = []
    for od in range(HY_ORDER):
        fwd = o[:, od * 2 * HY_CH:od * 2 * HY_CH + HY_CH]
        bwd = o[:, od * 2 * HY_CH + HY_CH:(od + 1) * 2 * HY_CH]
        outs.append((mf * fwd + mb * bwd) * dec)
    o_ref[...] = jnp.concatenate(outs, axis=1)


def _hy_filter(seq_len, w1, b1, w2, b2, w3, freq):
    n = 2 * seq_len
    tp = min(n, 1024)
    zt = jnp.asarray(_filter_features(seq_len))
    w1p = jnp.zeros((128, HY_FILT), F32).at[:HY_EMB].set(w1)
    full = lambda shape: pl.BlockSpec(shape, lambda i: (0, 0))
    return pl.pallas_call(
        _hy_filter_kernel,
        grid=(n // tp,),
        in_specs=[pl.BlockSpec((tp, 128), lambda i: (i, 0)),
                  full((128, HY_FILT)), full((1, HY_FILT)), full((HY_FILT, HY_FILT)), full((1, HY_FILT)),
                  full((HY_FILT, HY_ORDER * 2 * HY_CH)), full((1, HY_FILT)), full((1, HY_CH))],
        out_specs=pl.BlockSpec((tp, HY_ORDER * HY_CH), lambda i: (i, 0)),
        out_shape=jax.ShapeDtypeStruct((n, HY_ORDER * HY_CH), F32),
        compiler_params=_cparams(("parallel",)),
        name="hyena_filter_mlp",
    )(zt, w1p, b1.reshape(1, -1), w2, b2.reshape(1, -1), w3, freq.reshape(1, -1), jnp.asarray(_filter_deltas()))


@functools.lru_cache(maxsize=None)
def _fft_tables():
    n1 = np.arange(FFT_N1)[:, None]
    k1 = np.arange(FFT_N1)[None, :]
    th = 2.0 * math.pi * n1 * k1 / FFT_N1
    c, s = np.cos(th), np.sin(th)
    half = FFT_N1 // 2
    big_n = FFT_N1 * FFT_N2
    wa_c = np.block([[c[:half], -s[:half]], [s[:half], c[:half]]])
    wa_r = np.concatenate([c, -s], axis=1)
    tw = 2.0 * math.pi * np.arange(FFT_N2)[:, None] * np.arange(FFT_N1)[None, :] / big_n
    twf = np.concatenate([np.cos(tw), -np.sin(tw)], axis=1)
    twi = np.concatenate([np.cos(tw.T), np.sin(tw.T)], axis=1)
    wb = np.block([[c, -s], [s, c]])
    wc = np.block([[c, s], [-s, c]])
    wd = np.block([[c[:, :half], s[:, :half]], [-s[:, :half], c[:, :half]]]) / big_n
    return tuple(np.asarray(a, np.float32) for a in (wa_c, wa_r, twf, twi, wb, wc, wd))


def _fft_fwd(z, wa_ref, tw, wb_ref):
    ct = z.shape[0]
    a = _dot3(z.reshape(ct * FFT_N2, z.shape[2]), wa_ref).reshape(ct, FFT_N2, 2 * FFT_N1)
    ar, ai = a[:, :, :FFT_N1], a[:, :, FFT_N1:]
    tr, ti = tw[:, :FFT_N1], tw[:, FFT_N1:]
    br = ar * tr - ai * ti
    bi = ar * ti + ai * tr
    bt = jnp.concatenate([jnp.swapaxes(br, 1, 2), jnp.swapaxes(bi, 1, 2)], axis=2)
    return _dot3(bt.reshape(ct * FFT_N1, 2 * FFT_N2), wb_ref).reshape(ct, FFT_N1, 2 * FFT_N2)


def _hy_kfft_kernel(k_ref, wa_ref, tw_ref, wb_ref, o_ref):
    o_ref[...] = _fft_fwd(k_ref[...], wa_ref, tw_ref[...], wb_ref)


def _hy_long_kernel(z_ref, g_ref, kh_ref, bias_ref, wa_ref, twf_ref, wb_ref, wc_ref, twi_ref, wd_ref, o_ref):
    z = z_ref[...]
    ct = z.shape[0]
    x = _fft_fwd(z, wa_ref, twf_ref[...], wb_ref)
    kh = kh_ref[...]
    xr, xi = x[:, :, :FFT_N2], x[:, :, FFT_N2:]
    kr, ki = kh[:, :, :FFT_N2], kh[:, :, FFT_N2:]
    y = jnp.concatenate([xr * kr - xi * ki, xr * ki + xi * kr], axis=2)
    c1 = _dot3(y.reshape(ct * FFT_N1, 2 * FFT_N2), wc_ref).reshape(ct, FFT_N1, 2 * FFT_N2)
    cr, ci = c1[:, :, :FFT_N2], c1[:, :, FFT_N2:]
    twi = twi_ref[...]
    tr, ti = twi[:, :FFT_N2], twi[:, FFT_N2:]
    dr = cr * tr - ci * ti
    di = cr * ti + ci * tr
    dt = jnp.concatenate([jnp.swapaxes(dr, 1, 2), jnp.swapaxes(di, 1, 2)], axis=2)
    conv = _dot3(dt.reshape(ct * FFT_N2, 2 * FFT_N1), wd_ref).reshape(ct, FFT_N2, FFT_N1)
    o_ref[...] = (conv + bias_ref[...] * z) * g_ref[...]


def _hy_ctx_kernel(x_ref, kc_ref, bias_ref, cf_ref, sf_ref, o_ref):
    x = x_ref[0]
    cf, sf = cf_ref[...], sf_ref[...]
    n = CTX_LEN
    z = x[:, :HY_CH]
    for od in range(HY_ORDER):
        k = kc_ref[:, od * HY_CH:(od + 1) * HY_CH]
        kr, ki = _dot(cf, k), -_dot(sf, k)
        xr, xi = _dot(cf[:, :n], z), -_dot(sf[:, :n], z)
        yr = xr * kr - xi * ki
        yi = xr * ki + xi * kr
        conv = (_dot(cf[:n, :], yr) - _dot(sf[:n, :], yi)) * (1.0 / (2 * n))
        z = x[:, (od + 1) * HY_CH:(od + 2) * HY_CH] * (conv + bias_ref[od:od + 1, :] * z)
    o_ref[0] = z


@functools.lru_cache(maxsize=None)
def _ctx_dft():
    m = np.arange(2 * CTX_LEN)
    th = 2.0 * math.pi * m[:, None] * m[None, :] / (2 * CTX_LEN)
    return np.cos(th).astype(np.float32), np.sin(th).astype(np.float32)


def _hyena(p_hy, conv_w, conv_b, w1, b1, w2, b2, w3, freq, bias):
    wa_c, wa_r, twf, twi, wb, wc, wd = (jnp.asarray(a) for a in _fft_tables())
    wa_c, wa_r, wb, wc, wd = (jnp.stack(_split(a)) for a in (wa_c, wa_r, wb, wc, wd))
    nch = HY_ORDER * HY_CH
    kf = _hy_filter(SEQ, w1, b1, w2, b2, w3, freq)
    kt = kf.reshape(FFT_N1, FFT_N2, nch).transpose(2, 1, 0)
    const = lambda shape: pl.BlockSpec(shape, lambda j: (0,) * len(shape))
    khat = pl.pallas_call(
        _hy_kfft_kernel,
        grid=(nch // FFT_CT,),
        in_specs=[pl.BlockSpec((FFT_CT, FFT_N2, FFT_N1), lambda j: (j, 0, 0)),
                  const((2, FFT_N1, 2 * FFT_N1)), const((FFT_N2, 2 * FFT_N1)), const((2, 2 * FFT_N2, 2 * FFT_N2))],
        out_specs=pl.BlockSpec((FFT_CT, FFT_N1, 2 * FFT_N2), lambda j: (j, 0, 0)),
        out_shape=jax.ShapeDtypeStruct((nch, FFT_N1, 2 * FFT_N2), F32),
        compiler_params=_cparams(("parallel",)),
        name="hyena_filter_fft",
    )(kt, wa_r, twf, wb)

    pc = _hy_short(p_hy, conv_w, conv_b, SEQ, 0)
    half = FFT_N1 // 2
    zall = pc.reshape(BATCH, half, FFT_N2, HY_IN).transpose(3, 2, 0, 1).reshape(HY_IN, FFT_N2, FFT_N1)
    nblk = HY_CH // FFT_CT
    z = zall
    for od in range(HY_ORDER):
        z_spec = pl.BlockSpec((FFT_CT, FFT_N2, FFT_N1), lambda j: (j, 0, 0))
        g_spec = pl.BlockSpec((FFT_CT, FFT_N2, FFT_N1), lambda j, od=od: ((od + 1) * nblk + j, 0, 0))
        z = pl.pallas_call(
            _hy_long_kernel,
            grid=(nblk,),
            in_specs=[z_spec, g_spec,
                      pl.BlockSpec((FFT_CT, FFT_N1, 2 * FFT_N2), lambda j, od=od: (od * nblk + j, 0, 0)),
                      pl.BlockSpec((FFT_CT, 1, 1), lambda j, od=od: (od * nblk + j, 0, 0)),
                      const((2, FFT_N1, 2 * FFT_N1)), const((FFT_N2, 2 * FFT_N1)), const((2, 2 * FFT_N2, 2 * FFT_N2)),
                      const((2, 2 * FFT_N2, 2 * FFT_N2)), const((FFT_N1, 2 * FFT_N2)), const((2, 2 * FFT_N1, FFT_N1))],
            out_specs=pl.BlockSpec((FFT_CT, FFT_N2, FFT_N1), lambda j: (j, 0, 0)),
            out_shape=jax.ShapeDtypeStruct((HY_CH, FFT_N2, FFT_N1), F32),
            compiler_params=_cparams(("parallel",)),
            name="hyena_long_conv",
        )(z, zall, khat, bias.reshape(nch, 1, 1), wa_c, twf, wb, wc, twi, wd)
    o_lat = z.reshape(HY_CH, FFT_N2, BATCH, half).transpose(2, 3, 1, 0).reshape(BATCH, SEQ, HY_CH)

    kc = _hy_filter(CTX_LEN, w1, b1, w2, b2, w3, freq)
    pcc = _hy_short(p_hy, conv_w, conv_b, CTX_LEN, SEQ // CTX_LEN)
    cf, sf = (jnp.asarray(a) for a in _ctx_dft())
    o_ctx = pl.pallas_call(
        _hy_ctx_kernel,
        grid=(BATCH,),
        in_specs=[pl.BlockSpec((1, CTX_LEN, HY_IN), lambda b: (b, 0, 0)),
                  pl.BlockSpec((2 * CTX_LEN, nch), lambda b: (0, 0)),
                  pl.BlockSpec((HY_ORDER, HY_CH), lambda b: (0, 0)),
                  pl.BlockSpec((2 * CTX_LEN, 2 * CTX_LEN), lambda b: (0, 0)),
                  pl.BlockSpec((2 * CTX_LEN, 2 * CTX_LEN), lambda b: (0, 0))],
        out_specs=pl.BlockSpec((1, CTX_LEN, HY_CH), lambda b: (b, 0, 0)),
        out_shape=jax.ShapeDtypeStruct((BATCH, CTX_LEN, HY_CH), F32),
        compiler_params=_cparams(("parallel",)),
        name="hyena_context",
    )(pcc, kc, bias, cf, sf)
    return jnp.concatenate([o_lat, o_ctx], axis=1)


@functools.lru_cache(maxsize=None)
def _rope_tables():
    nf = RW_HD // 4
    t = np.arange(SEQ)
    row = (t // GRID_W).astype(np.float64)
    col = (t % GRID_W).astype(np.float64)
    inv = ROPE_BASE ** (-np.arange(nf, dtype=np.float64) / nf)
    ang = np.concatenate([row[:, None] * inv] * 2 + [col[:, None] * inv] * 2, axis=-1)
    cos = np.concatenate([np.cos(ang), np.ones((CTX_LEN, RW_HD))], axis=0)
    sin = np.concatenate([np.sin(ang), np.zeros((CTX_LEN, RW_HD))], axis=0)
    return (np.tile(cos, (1, RW_HEADS)).astype(np.float32), np.tile(sin, (1, RW_HEADS)).astype(np.float32))


@functools.lru_cache(maxsize=None)
def _head_blockdiag():
    h = np.arange(RW_W) // RW_HD
    return (h[:, None] == h[None, :]).astype(np.float32)


@functools.lru_cache(maxsize=None)
def _chunk_tri():
    t = np.arange(RW_TR)[:, None]
    s = np.arange(RW_TR)[None, :]
    same = (t // RW_C) == (s // RW_C)
    return np.stack([same & (s <= t), same & (s >= t)]).astype(np.float32)


def _rope(x, cos, sin):
    lane = lax.broadcasted_iota(jnp.int32, (1, RW_W), 1)
    first = (lane % (RW_HD // 2)) < (RW_HD // 4)
    rot = jnp.where(first, -pltpu.roll(x, RW_W - RW_HD // 4, 1), pltpu.roll(x, RW_HD // 4, 1))
    return x * cos + rot * sin


def _rw_prep_kernel(p_ref, pv_ref, nx_ref, mu_ref, w0_ref, w2_ref, a0_ref, a2_ref, g2_ref, kk_ref, ka_ref, rk_ref,
                    cos_ref, sin_ref, bd_ref, tri_ref,
                    v_ref, bon_ref, g_ref, at_ref, rt_ref, bt_ref, kt_ref, wc_ref):
    i = pl.program_id(1)
    x = p_ref[0]
    row = lax.broadcasted_iota(jnp.int32, (RW_TR, 1), 0)
    t = i * RW_TR + row
    seq_first = (t == 0) | (t == SEQ)
    seq_last = (t == SEQ - 1) | (t == T_ALL - 1)
    prev = jnp.where(row == 0, pv_ref[0, 7:8, :], pltpu.roll(x, 1, 0))
    prev = jnp.where(seq_first, 0.0, prev)
    nxt = jnp.where(row == RW_TR - 1, nx_ref[0, 0:1, :], pltpu.roll(x, RW_TR - 1, 0))
    nxt = jnp.where(seq_last, 0.0, nxt)
    xs = x + mu_ref[0:1, :] * (prev - x) + mu_ref[1:2, :] * (nxt - x)
    r = xs[:, 0:RW_W]
    k = xs[:, RW_W:2 * RW_W]
    v = xs[:, 2 * RW_W:3 * RW_W]
    i1 = 3 * RW_W
    wd = jnp.tanh(xs[:, i1:i1 + 2 * RW_LORA])
    ad = xs[:, i1 + 2 * RW_LORA:i1 + 4 * RW_LORA]
    gd = xs[:, i1 + 4 * RW_LORA:]
    cos, sin, bd = cos_ref[...], sin_ref[...], bd_ref[...]
    g_ref[0] = _dot(jax.nn.sigmoid(gd), g2_ref[...])
    kk = k * kk_ref[...]
    nrm = jnp.sqrt(_dot(kk * kk, bd))
    kk = kk / jnp.maximum(nrm, 1e-12)
    ksum = jnp.zeros_like(k)
    r_s = _rope(r, cos, sin)
    na_s = -_rope(kk, cos, sin)
    n_chunks = RW_TR // RW_C
    for d in range(2):
        logit = w0_ref[d:d + 1, :] + _dot(wd, w2_ref[d])
        logw = -jax.nn.softplus(-logit) - 0.5
        lw = -jnp.exp(logw)
        a = jax.nn.sigmoid(a0_ref[d:d + 1, :] + _dot(ad, a2_ref[d]))
        kdir = k * (1.0 + (a - 1.0) * ka_ref[...])
        ksum = ksum + kdir
        cum = _dot(tri_ref[d], lw)
        e_neg = jnp.exp(-cum)
        at_ref[d, 0] = na_s * jnp.exp(cum - lw)
        rt_ref[d, 0] = r_s * jnp.exp(cum)
        bt_ref[d, 0] = _rope(kk * a, cos, sin) * e_neg
        kt_ref[d, 0] = _rope(kdir, cos, sin) * e_neg
        tot = jnp.sum(lw.reshape(n_chunks, RW_C, RW_W), axis=1, keepdims=True)
        wc_ref[d, 0] = jnp.broadcast_to(jnp.exp(tot), (n_chunks, 8, RW_W))
    bon_ref[0] = _dot(r * ksum * rk_ref[...], bd)
    v_ref[0] = v


def _bmm3(spec, a, b):
    (ah, am), (bh, bm) = a, b
    e = functools.partial(jnp.einsum, spec, preferred_element_type=F32)
    return e(ah, bh) + e(am, bh) + e(ah, bm)


def _rw_scan_kernel(v_ref, at_ref, rt_ref, bt_ref, kt_ref, wc_ref, y_ref, s_sc):
    d = pl.program_id(0)
    step = pl.program_id(1)
    nb = BATCH * RW_HEADS
    c = RW_C

    @pl.when(step == 0)
    def _():
        s_sc[...] = jnp.zeros_like(s_sc)

    def heads(x):
        return jnp.stack([x[b][:, h * RW_HD:(h + 1) * RW_HD] for b in range(BATCH) for h in range(RW_HEADS)])

    v = heads(v_ref[...])
    ar = jnp.concatenate([heads(at_ref[0]), heads(rt_ref[0])], axis=1)
    bt = heads(bt_ref[0])
    kt = heads(kt_ref[0])
    wc = heads(wc_ref[0, :, 0])[:, 0:1, :]

    ti = lax.broadcasted_iota(jnp.int32, (c, c), 0)
    si = lax.broadcasted_iota(jnp.int32, (c, c), 1)
    ahead = (ti - si) * jnp.where(d == 0, 1, -1)
    strict = (ahead > 0)[None]
    incl = (ahead >= 0)[None]
    eye = (si == ti).astype(F32)[None]

    ar_s, bt_s, kt_s, v_s = _split(ar), _split(bt), _split(kt), _split(v)
    lb = _bmm3('htj,hsj->hts', ar_s, bt_s)
    lk = _bmm3('htj,hsj->hts', ar_s, kt_s)
    lab = jnp.where(strict, lb[:, :c], 0.0)
    lrb = jnp.where(incl, lb[:, c:], 0.0)
    lak = jnp.where(strict, lk[:, :c], 0.0)
    lrk = jnp.where(incl, lk[:, c:], 0.0)

    tt = eye + lab
    pw = lab
    for _ in range(int(math.log2(c)) - 1):
        pw_s = _split(pw)
        pw = _bmm3('hts,hsu->htu', pw_s, pw_s)
        tt = tt + _bmm3('hts,hsu->htu', _split(tt), _split(pw))

    s0 = s_sc[...]
    ars = _bmm3('htk,hvk->htv', ar_s, _split(s0))
    lv = _bmm3('hts,hsv->htv', _split(jnp.concatenate([lak, lrk], axis=1)), v_s)
    u = _bmm3('hts,hsv->htv', _split(tt), _split(lv[:, :c] + ars[:, :c]))
    y = ars[:, c:] + _bmm3('hts,hsv->htv', _split(lrb), _split(u)) + lv[:, c:]
    for b in range(BATCH):
        y_ref[0, b] = jnp.concatenate([y[b * RW_HEADS + h] for h in range(RW_HEADS)], axis=-1)
    uv = jnp.concatenate([u, v], axis=1)
    bk = jnp.concatenate([bt, kt], axis=1)
    s_sc[...] = (s0 + _bmm3('htv,htj->hvj', _split(uv), _split(bk))) * wc


def _rw_out_kernel(yf_ref, yb_ref, bon_ref, v_ref, g_ref, lnw_ref, lnb_ref, bd_ref, o_ref):
    y = yf_ref[0, 0] + yb_ref[0, 0]
    bd = bd_ref[...]
    mean = _dot(y, bd) * (1.0 / RW_HD)
    yc = y - mean
    var = _dot(yc * yc, bd) * (1.0 / RW_HD)
    yn = yc * lax.rsqrt(var + RW_GN_EPS) * lnw_ref[...] + lnb_ref[...]
    o_ref[0] = (yn + bon_ref[0] * v_ref[0]) * g_ref[0]


def _rw_chunk_index(d, s):
    n_lat = SEQ // RW_C
    n_all = T_ALL // RW_C
    fwd = jnp.where(s < n_all - n_lat, n_lat + s, s - (n_all - n_lat))
    return jnp.where(d == 0, fwd, n_all - 1 - s)


def _rwkv(p_rw, mu, w0, w2, a0, a2, g2, k_k, k_a, r_k, ln_w, ln_b):
    cos, sin = (jnp.asarray(a) for a in _rope_tables())
    bd = jnp.asarray(_head_blockdiag())
    zpad = lambda w: jnp.stack([jnp.concatenate([w[0], jnp.zeros_like(w[0])], axis=0),
                                jnp.concatenate([jnp.zeros_like(w[1]), w[1]], axis=0)])
    nt = T_ALL // RW_TR
    n_ch = T_ALL // RW_C
    tok_shape = jax.ShapeDtypeStruct((BATCH, T_ALL, RW_W), F32)
    dir_shape = jax.ShapeDtypeStruct((2, BATCH, T_ALL, RW_W), F32)
    dir_spec = pl.BlockSpec((2, 1, RW_TR, RW_W), lambda b, i: (0, b, i, 0))
    wc_shape = jax.ShapeDtypeStruct((2, BATCH, n_ch, 8, RW_W), F32)
    wc_spec = pl.BlockSpec((2, 1, RW_TR // RW_C, 8, RW_W), lambda b, i: (0, b, i, 0, 0))
    tok = lambda w: pl.BlockSpec((1, RW_TR, w), lambda b, i: (b, i, 0))
    const = lambda shape: pl.BlockSpec(shape, lambda b, i: (0,) * len(shape))
    row = lambda a: a.reshape(1, RW_W)
    outs = pl.pallas_call(
        _rw_prep_kernel,
        grid=(BATCH, nt),
        in_specs=[tok(RW_IN),
                  pl.BlockSpec((1, 8, RW_IN), lambda b, i: (b, jnp.maximum(i * (RW_TR // 8) - 1, 0), 0)),
                  pl.BlockSpec((1, 8, RW_IN), lambda b, i: (b, jnp.minimum((i + 1) * (RW_TR // 8), T_ALL // 8 - 1), 0)),
                  const((2, RW_IN)), const((2, RW_W)), const((2, 2 * RW_LORA, RW_W)), const((2, RW_W)),
                  const((2, 2 * RW_LORA, RW_W)), const((RW_GATE_LORA, RW_W)),
                  const((1, RW_W)), const((1, RW_W)), const((1, RW_W)),
                  pl.BlockSpec((RW_TR, RW_W), lambda b, i: (i, 0)), pl.BlockSpec((RW_TR, RW_W), lambda b, i: (i, 0)),
                  const((RW_W, RW_W)), const((2, RW_TR, RW_TR))],
        out_specs=[tok(RW_W)] * 3 + [dir_spec] * 4 + [wc_spec],
        out_shape=[tok_shape] * 3 + [dir_shape] * 4 + [wc_shape],
        compiler_params=_cparams(("parallel", "parallel")),
        name="rwkv_prepare",
    )(p_rw, p_rw, p_rw, mu, w0, zpad(w2), a0, zpad(a2), g2, row(k_k), row(k_a), row(r_k.reshape(-1)), cos, sin, bd,
      jnp.asarray(_chunk_tri()))
    v, bonus, g, at, rt, bt, kt, wc = outs

    perdir = pl.BlockSpec((1, BATCH, RW_C, RW_W), lambda d, s: (d, 0, _rw_chunk_index(d, s), 0))
    yt = pl.pallas_call(
        _rw_scan_kernel,
        grid=(2, n_ch),
        in_specs=[pl.BlockSpec((BATCH, RW_C, RW_W), lambda d, s: (0, _rw_chunk_index(d, s), 0)),
                  perdir, perdir, perdir, perdir,
                  pl.BlockSpec((1, BATCH, 1, 8, RW_W), lambda d, s: (d, 0, _rw_chunk_index(d, s), 0, 0))],
        out_specs=perdir,
        out_shape=dir_shape,
        scratch_shapes=[pltpu.VMEM((BATCH * RW_HEADS, RW_HD, RW_HD), F32)],
        compiler_params=_cparams(("parallel", "arbitrary")),
        name="rwkv_chunk_scan",
    )(v, at, rt, bt, kt, wc)

    tok2 = lambda w: pl.BlockSpec((1, PROJ_TM, w), lambda b, i: (b, i, 0))
    return pl.pallas_call(
        _rw_out_kernel,
        grid=(BATCH, T_ALL // PROJ_TM),
        in_specs=[pl.BlockSpec((1, 1, PROJ_TM, RW_W), lambda b, i: (0, b, i, 0)),
                  pl.BlockSpec((1, 1, PROJ_TM, RW_W), lambda b, i: (1, b, i, 0)),
                  tok2(RW_W), tok2(RW_W), tok2(RW_W),
                  const((1, RW_W)), const((1, RW_W)), const((RW_W, RW_W))],
        out_specs=tok2(RW_W),
        out_shape=jax.ShapeDtypeStruct((BATCH, T_ALL, RW_W), F32),
        compiler_params=_cparams(("parallel", "parallel")),
        name="rwkv_output",
    )(yt, yt, bonus, v, g, row(ln_w), row(ln_b), bd)


@functools.lru_cache(maxsize=None)
def _na_bias_tables():
    qc = np.arange(GRID_W)[:, None]
    kc = np.arange(GRID_W)[None, :]
    win0 = np.clip(qc - WIN_COLS // 2, 0, GRID_W - WIN_COLS)
    valid = (kc >= win0) & (kc < win0 + WIN_COLS)
    off = np.clip(kc - qc + WIN_COLS - 1, 0, 2 * WIN_COLS - 2)
    onehot = np.zeros((128, GRID_W * GRID_W), np.float32)
    onehot[off.reshape(-1), np.arange(GRID_W * GRID_W)] = valid.reshape(-1)
    mask = np.where(valid, 0.0, NEG_INF).astype(np.float32).reshape(1, GRID_W * GRID_W)
    return onehot, mask


def _na_bias_kernel(rpb_ref, oh_ref, mask_ref, o_ref):
    o_ref[...] = _dot(rpb_ref[...], oh_ref[...]) + mask_ref[...]


def _na_lat_kernel(q_ref, k_ref, v_ref, tp_ref, o_ref):
    i = pl.program_id(1)
    start = jnp.clip(i - WIN_ROWS // 2, 0, N_ROWS - WIN_ROWS)
    dr0 = start - i + WIN_ROWS - 1
    scale = NA_HD ** -0.5
    lane = lax.broadcasted_iota(jnp.int32, (1, NA_W), 1)
    hmask = [(lane // NA_HD) == h for h in range(NA_HEADS)]
    q = q_ref[0]
    qm = jnp.concatenate([jnp.where(hmask[h], q, jnp.zeros_like(q)) for h in range(NA_HEADS)], axis=0)
    row0 = pl.multiple_of(start * GRID_W, GRID_W)
    kl = k_ref[0, pl.ds(row0, WIN_ROWS * GRID_W), :]
    vl = v_ref[0, pl.ds(row0, WIN_ROWS * GRID_W), :]
    kc = k_ref[0, SEQ:T_ALL, :]
    vc = v_ref[0, SEQ:T_ALL, :]
    nt = (((1,), (1,)), ((), ()))
    bias = jnp.concatenate(
        [jnp.concatenate([tp_ref[h, pl.ds(dr0 + 2 * j, 1)][0] for j in range(WIN_ROWS // 2)], axis=1)
         for h in range(NA_HEADS)], axis=0)
    s_loc = lax.dot_general(qm, kl, nt, preferred_element_type=F32) * scale + bias
    s_ctx = lax.dot_general(qm, kc, nt, preferred_element_type=F32) * scale
    m = jnp.maximum(jnp.max(s_loc, axis=-1, keepdims=True), jnp.max(s_ctx, axis=-1, keepdims=True))
    p_loc = jnp.exp(s_loc - m)
    p_ctx = jnp.exp(s_ctx - m)
    inv = 1.0 / (jnp.sum(p_loc, axis=-1, keepdims=True) + jnp.sum(p_ctx, axis=-1, keepdims=True))
    o_all = (jnp.dot((p_loc * inv).astype(BF16), vl, preferred_element_type=F32)
             + jnp.dot((p_ctx * inv).astype(BF16), vc, preferred_element_type=F32))
    o = jnp.zeros((GRID_W, NA_W), F32)
    for h in range(NA_HEADS):
        o = o + jnp.where(hmask[h], o_all[h * GRID_W:(h + 1) * GRID_W, :], 0.0)
    o_ref[0] = o


def _na_ctx_kernel(q_ref, k_ref, v_ref, o_ref):
    scale = NA_HD ** -0.5
    lane = lax.broadcasted_iota(jnp.int32, (1, NA_W), 1)
    hmask = [(lane // NA_HD) == h for h in range(NA_HEADS)]
    q, k, v = q_ref[0], k_ref[0], v_ref[0]
    qm = jnp.concatenate([jnp.where(hmask[h], q, jnp.zeros_like(q)) for h in range(NA_HEADS)], axis=0)
    s = lax.dot_general(qm, k, (((1,), (1,)), ((), ())), preferred_element_type=F32) * scale
    p = jnp.exp(s - jnp.max(s, axis=-1, keepdims=True))
    p = p / jnp.sum(p, axis=-1, keepdims=True)
    o_all = jnp.dot(p.astype(BF16), v, preferred_element_type=F32)
    o = jnp.zeros((CTX_LEN, NA_W), F32)
    for h in range(NA_HEADS):
        o = o + jnp.where(hmask[h], o_all[h * CTX_LEN:(h + 1) * CTX_LEN, :], 0.0)
    o_ref[0] = o


def _natten(p_na, rpb):
    onehot, mask = (jnp.asarray(a) for a in _na_bias_tables())
    n_dr = 2 * WIN_ROWS - 1
    n_rel = 2 * WIN_COLS - 1
    rows = NA_HEADS * n_dr
    rows_p = -(-rows // 8) * 8
    rpb2 = jnp.zeros((rows_p, 128), F32).at[:rows, :n_rel].set(rpb.reshape(rows, n_rel))
    tb = pl.pallas_call(
        _na_bias_kernel,
        out_shape=jax.ShapeDtypeStruct((rows_p, GRID_W * GRID_W), F32),
        name="natten_bias_expand",
    )(rpb2, onehot, mask)
    tb = tb[:rows].reshape(NA_HEADS, n_dr, GRID_W, GRID_W)
    nxt = jnp.concatenate([tb[:, 1:], jnp.full_like(tb[:, :1], NEG_INF)], axis=1)
    tp = jnp.concatenate([tb, nxt], axis=-1)

    o_lat = pl.pallas_call(
        _na_lat_kernel,
        grid=(BATCH, N_ROWS),
        in_specs=[pl.BlockSpec((1, GRID_W, NA_W), lambda b, i: (b, i, 0)),
                  pl.BlockSpec((1, T_ALL, NA_W), lambda b, i: (b, 0, 1)),
                  pl.BlockSpec((1, T_ALL, NA_W), lambda b, i: (b, 0, 2)),
                  pl.BlockSpec((NA_HEADS, n_dr, GRID_W, 2 * GRID_W), lambda b, i: (0, 0, 0, 0))],
        out_specs=pl.BlockSpec((1, GRID_W, NA_W), lambda b, i: (b, i, 0)),
        out_shape=jax.ShapeDtypeStruct((BATCH, SEQ, NA_W), F32),
        compiler_params=_cparams(("parallel", "parallel")),
        name="natten_latent",
    )(p_na, p_na, p_na, tp)

    cblk = SEQ // CTX_LEN
    o_ctx = pl.pallas_call(
        _na_ctx_kernel,
        grid=(BATCH,),
        in_specs=[pl.BlockSpec((1, CTX_LEN, NA_W), lambda b: (b, cblk, 0)),
                  pl.BlockSpec((1, CTX_LEN, NA_W), lambda b: (b, cblk, 1)),
                  pl.BlockSpec((1, CTX_LEN, NA_W), lambda b: (b, cblk, 2))],
        out_specs=pl.BlockSpec((1, CTX_LEN, NA_W), lambda b: (b, 0, 0)),
        out_shape=jax.ShapeDtypeStruct((BATCH, CTX_LEN, NA_W), F32),
        compiler_params=_cparams(("parallel",)),
        name="natten_context",
    )(p_na, p_na, p_na)
    return jnp.concatenate([o_lat, o_ctx], axis=1)


def kernel(x, c, ctx, c_ctx, mod_w, mod_b, norm_g, ffn1_wgu, ffn1_wdn, ffn2_wgu, ffn2_wdn, w_in, w_out, hy_conv_w, hy_conv_b, hy_f_w1, hy_f_b1, hy_f_w2, hy_f_b2, hy_f_w3, hy_freq, hy_bias, rw_mu, rw_w0, rw_w2, rw_a0, rw_a2, rw_g2, rw_k_k, rw_k_a, rw_r_k, rw_ln_w, rw_ln_b, na_rpb):
    assert x.shape == (BATCH, SEQ, D_MODEL) and ctx.shape == (BATCH, CTX_LEN, D_MODEL)
    mods = _modulation(c, c_ctx, mod_w, mod_b)
    h = jnp.concatenate([x, ctx], axis=1)
    gpad = jnp.concatenate([norm_g, jnp.zeros((DEPTH, 2, D_MODEL), F32)], axis=1)
    for l in range(DEPTH):
        m_l, g_l = mods[l], gpad[l]
        h = _ffn(h, m_l, g_l, ffn1_wgu[l].astype(BF16), ffn1_wdn[l].astype(BF16), 0)
        p_hy, p_rw, p_na = _in_proj(h, m_l, g_l, w_in[l].astype(BF16))
        o_hy = _hyena(p_hy, hy_conv_w[l], hy_conv_b[l], hy_f_w1[l], hy_f_b1[l], hy_f_w2[l], hy_f_b2[l],
                      hy_f_w3[l], hy_freq[l], hy_bias[l])
        o_rw = _rwkv(p_rw, rw_mu[l], rw_w0[l], rw_w2[l], rw_a0[l], rw_a2[l], rw_g2[l], rw_k_k[l], rw_k_a[l],
                     rw_r_k[l], rw_ln_w[l], rw_ln_b[l])
        o_na = _natten(p_na, na_rpb[l])
        h = _out_proj(h, o_hy, o_rw, o_na, m_l, g_l, w_out[l].astype(BF16))
        h = _ffn(h, m_l, g_l, ffn2_wgu[l].astype(BF16), ffn2_wdn[l].astype(BF16), 1)
    return h[:, :SEQ]
```

```python
import functools
import math

import numpy as np
import jax
import jax.numpy as jnp
from jax import lax
from jax.experimental import pallas as pl
from jax.experimental.pallas import tpu as pltpu

D_MODEL = 1024
BATCH = 2
SEQ = 8192
DEPTH = 4
GRID_W = 64
CTX_LEN = 256
T_ALL = SEQ + CTX_LEN
N_MOD = 9
D_FF = 2816
MACARON_W = 0.5
NORM_EPS = 1e-6
NEG_INF = -1e30

HY_CH = 256
HY_ORDER = 2
HY_EMB = 33
HY_FILT = 64
HY_TARGET = 1e-2
HY_FAST = 0.3
HY_SLOW = 1.5
HY_IN = (HY_ORDER + 1) * HY_CH

RW_HEADS = 6
RW_HD = 64
RW_W = RW_HEADS * RW_HD
RW_LORA = 64
RW_GATE_LORA = 128
RW_GN_EPS = 64e-5
ROPE_BASE = 10000.0
RW_IN = 3 * RW_W + 4 * RW_LORA + RW_GATE_LORA

NA_HEADS = 6
NA_HD = 64
NA_W = NA_HEADS * NA_HD
WIN_ROWS = 8
WIN_COLS = 16
NA_IN = 3 * NA_W
N_ROWS = SEQ // GRID_W

F32 = jnp.float32
BF16 = jnp.bfloat16
HI = lax.Precision.HIGHEST

VMEM_LIMIT = 56 * 1024 * 1024

FFN_TM = 768
FFN_TF = 1408
PROJ_TM = 384
RW_TR = 256
RW_C = 64
FFT_N1 = 128
FFT_N2 = 128
FFT_CT = 16


def _dot(a, b):
    return jnp.dot(a, b, precision=HI, preferred_element_type=F32)


def _split(x):
    hi = x.astype(BF16)
    return hi, (x - hi.astype(F32)).astype(BF16)


def _dot3(x, w_ref, idx=()):
    xh, xm = _split(x)
    wh, wm = w_ref[idx + (0,)], w_ref[idx + (1,)]
    d = functools.partial(jnp.dot, preferred_element_type=F32)
    return d(xh, wh) + d(xm, wh) + d(xh, wm)


def _split3(x):
    hi = x.astype(BF16)
    r1 = x - hi.astype(F32)
    mid = r1.astype(BF16)
    return hi, mid, (r1 - mid.astype(F32)).astype(BF16)


def _dot01(x, m, const_left=False):
    d = functools.partial(jnp.dot, preferred_element_type=F32)
    parts = _split3(x)
    if const_left:
        return d(m, parts[0]) + d(m, parts[1]) + d(m, parts[2])
    return d(parts[0], m) + d(parts[1], m) + d(parts[2], m)


def _cparams(sem, vmem=VMEM_LIMIT):
    return pltpu.CompilerParams(dimension_semantics=sem, vmem_limit_bytes=vmem)


def _mod_kernel(cv_ref, w_ref, b_ref, o_ref):
    cv = cv_ref[...]
    s = cv * jax.nn.sigmoid(cv)
    o_ref[0] = _dot(s, w_ref[0]) + b_ref[0]


def _modulation(c, c_ctx, mod_w, mod_b):
    cv = jnp.concatenate([c, c_ctx[None, :], jnp.zeros((8 - BATCH - 1, D_MODEL), F32)], axis=0)
    tn = 1536
    return pl.pallas_call(
        _mod_kernel,
        grid=(DEPTH, N_MOD * D_MODEL // tn),
        in_specs=[pl.BlockSpec((8, D_MODEL), lambda l, j: (0, 0)),
                  pl.BlockSpec((1, D_MODEL, tn), lambda l, j: (l, 0, j)),
                  pl.BlockSpec((1, 1, tn), lambda l, j: (l, 0, j))],
        out_specs=pl.BlockSpec((1, 8, tn), lambda l, j: (l, 0, j)),
        out_shape=jax.ShapeDtypeStruct((DEPTH, 8, N_MOD * D_MODEL), F32),
        compiler_params=_cparams(("parallel", "parallel")),
        name="adaln_modulation",
    )(cv, mod_w, mod_b.reshape(DEPTH, 1, N_MOD * D_MODEL))


def _mod_rows(m_ref, b, i, tm, idx):
    t = i * tm + lax.broadcasted_iota(jnp.int32, (tm, 1), 0)
    ml = m_ref[pl.ds(b, 1), idx * D_MODEL:(idx + 1) * D_MODEL]
    mc = m_ref[BATCH:BATCH + 1, idx * D_MODEL:(idx + 1) * D_MODEL]
    return jnp.where(t >= SEQ, mc, ml)


def _rms(x, g):
    return x * lax.rsqrt(jnp.mean(x * x, axis=-1, keepdims=True) + NORM_EPS) * g


def _ffn_kernel(h_ref, m_ref, g_ref, wg_ref, wu_ref, wd_ref, o_ref, u_sc, acc_sc, *, mi, gi):
    b, i, k = pl.program_id(0), pl.program_id(1), pl.program_id(2)

    @pl.when(k == 0)
    def _():
        y = _rms(h_ref[0], g_ref[gi:gi + 1, :])
        u = y * (1.0 + _mod_rows(m_ref, b, i, FFN_TM, mi + 1)) + _mod_rows(m_ref, b, i, FFN_TM, mi)
        u_sc[...] = u.astype(BF16)
        acc_sc[...] = jnp.zeros_like(acc_sc)

    u = u_sc[...]
    gate = jnp.dot(u, wg_ref[...], preferred_element_type=F32)
    up = jnp.dot(u, wu_ref[...], preferred_element_type=F32)
    a = (gate * jax.nn.sigmoid(gate) * up).astype(BF16)
    acc_sc[...] += jnp.dot(a, wd_ref[...], preferred_element_type=F32)

    @pl.when(k == pl.num_programs(2) - 1)
    def _():
        y = _rms(acc_sc[...], g_ref[gi + 1:gi + 2, :])
        o_ref[0] = h_ref[0] + MACARON_W * _mod_rows(m_ref, b, i, FFN_TM, mi + 2) * y


def _ffn(h, mods_l, g_l, wgu, wdn, which):
    mi, gi = (0, 0) if which == 0 else (6, 4)
    nk = D_FF // FFN_TF
    return pl.pallas_call(
        functools.partial(_ffn_kernel, mi=mi, gi=gi),
        grid=(BATCH, T_ALL // FFN_TM, nk),
        in_specs=[pl.BlockSpec((1, FFN_TM, D_MODEL), lambda b, i, k: (b, i, 0)),
                  pl.BlockSpec((8, N_MOD * D_MODEL), lambda b, i, k: (0, 0)),
                  pl.BlockSpec((8, D_MODEL), lambda b, i, k: (0, 0)),
                  pl.BlockSpec((D_MODEL, FFN_TF), lambda b, i, k: (0, k)),
                  pl.BlockSpec((D_MODEL, FFN_TF), lambda b, i, k: (0, k + D_FF // FFN_TF)),
                  pl.BlockSpec((FFN_TF, D_MODEL), lambda b, i, k: (k, 0))],
        out_specs=pl.BlockSpec((1, FFN_TM, D_MODEL), lambda b, i, k: (b, i, 0)),
        out_shape=jax.ShapeDtypeStruct((BATCH, T_ALL, D_MODEL), F32),
        scratch_shapes=[pltpu.VMEM((FFN_TM, D_MODEL), BF16), pltpu.VMEM((FFN_TM, D_MODEL), F32)],
        compiler_params=_cparams(("parallel", "parallel", "arbitrary")),
        name="ffn_sublayer",
    )(h, mods_l, g_l, wgu, wgu, wdn)


def _inproj_kernel(h_ref, m_ref, g_ref, w_ref, hy_ref, rw_ref, na_ref):
    b, i = pl.program_id(0), pl.program_id(1)
    y = _rms(h_ref[0], g_ref[2:3, :])
    u = y * (1.0 + _mod_rows(m_ref, b, i, PROJ_TM, 4)) + _mod_rows(m_ref, b, i, PROJ_TM, 3)
    p = jnp.dot(u.astype(BF16), w_ref[...], preferred_element_type=F32)
    hy_ref[0] = p[:, :HY_IN]
    rw_ref[0] = p[:, HY_IN:HY_IN + RW_IN]
    na_ref[0] = p[:, HY_IN + RW_IN:].astype(BF16)


def _in_proj(h, mods_l, g_l, w_in):
    n_in = HY_IN + RW_IN + NA_IN
    return pl.pallas_call(
        _inproj_kernel,
        grid=(BATCH, T_ALL // PROJ_TM),
        in_specs=[pl.BlockSpec((1, PROJ_TM, D_MODEL), lambda b, i: (b, i, 0)),
                  pl.BlockSpec((8, N_MOD * D_MODEL), lambda b, i: (0, 0)),
                  pl.BlockSpec((8, D_MODEL), lambda b, i: (0, 0)),
                  pl.BlockSpec((D_MODEL, n_in), lambda b, i: (0, 0))],
        out_specs=[pl.BlockSpec((1, PROJ_TM, HY_IN), lambda b, i: (b, i, 0)),
                   pl.BlockSpec((1, PROJ_TM, RW_IN), lambda b, i: (b, i, 0)),
                   pl.BlockSpec((1, PROJ_TM, NA_IN), lambda b, i: (b, i, 0))],
        out_shape=[jax.ShapeDtypeStruct((BATCH, T_ALL, HY_IN), F32),
                   jax.ShapeDtypeStruct((BATCH, T_ALL, RW_IN), F32),
                   jax.ShapeDtypeStruct((BATCH, T_ALL, NA_IN), BF16)],
        compiler_params=_cparams(("parallel", "parallel")),
        name="mixer_in_proj",
    )(h, mods_l, g_l, w_in)


def _outproj_kernel(h_ref, hy_ref, rw_ref, na_ref, m_ref, g_ref, w_ref, o_ref):
    b, i = pl.program_id(0), pl.program_id(1)
    cat = jnp.concatenate([hy_ref[0], rw_ref[0], na_ref[0]], axis=-1).astype(BF16)
    o = jnp.dot(cat, w_ref[...], preferred_element_type=F32)
    o_ref[0] = h_ref[0] + _mod_rows(m_ref, b, i, PROJ_TM, 5) * _rms(o, g_ref[3:4, :])


def _out_proj(h, o_hy, o_rw, o_na, mods_l, g_l, w_out):
    return pl.pallas_call(
        _outproj_kernel,
        grid=(BATCH, T_ALL // PROJ_TM),
        in_specs=[pl.BlockSpec((1, PROJ_TM, D_MODEL), lambda b, i: (b, i, 0)),
                  pl.BlockSpec((1, PROJ_TM, HY_CH), lambda b, i: (b, i, 0)),
                  pl.BlockSpec((1, PROJ_TM, RW_W), lambda b, i: (b, i, 0)),
                  pl.BlockSpec((1, PROJ_TM, NA_W), lambda b, i: (b, i, 0)),
                  pl.BlockSpec((8, N_MOD * D_MODEL), lambda b, i: (0, 0)),
                  pl.BlockSpec((8, D_MODEL), lambda b, i: (0, 0)),
                  pl.BlockSpec((D_MODEL, D_MODEL), lambda b, i: (0, 0))],
        out_specs=pl.BlockSpec((1, PROJ_TM, D_MODEL), lambda b, i: (b, i, 0)),
        out_shape=jax.ShapeDtypeStruct((BATCH, T_ALL, D_MODEL), F32),
        compiler_params=_cparams(("parallel", "parallel")),
        name="mixer_out_proj",
    )(h, o_hy, o_rw, o_na, mods_l, g_l, w_out)


def _hy_short_kernel(x_ref, w_ref, b_ref, o_ref):
    x = x_ref[0]
    n = x.shape[0]
    row = lax.broadcasted_iota(jnp.int32, (n, 1), 0)
    prev = jnp.where(row == 0, 0.0, pltpu.roll(x, 1, 0))
    nxt = jnp.where(row == n - 1, 0.0, pltpu.roll(x, n - 1, 0))
    o_ref[0] = prev * w_ref[0:1, :] + x * w_ref[1:2, :] + nxt * w_ref[2:3, :] + b_ref[...]


def _hy_short(p_hy, conv_w, conv_b, seq_len, row_block):
    lanes = 128
    return pl.pallas_call(
        _hy_short_kernel,
        grid=(BATCH, HY_IN // lanes),
        in_specs=[pl.BlockSpec((1, seq_len, lanes), lambda b, j: (b, row_block, j)),
                  pl.BlockSpec((3, lanes), lambda b, j: (0, j)),
                  pl.BlockSpec((1, lanes), lambda b, j: (0, j))],
        out_specs=pl.BlockSpec((1, seq_len, lanes), lambda b, j: (b, 0, j)),
        out_shape=jax.ShapeDtypeStruct((BATCH, seq_len, HY_IN), F32),
        compiler_params=_cparams(("parallel", "parallel")),
        name="hyena_short_conv",
    )(p_hy, conv_w, conv_b.reshape(1, HY_IN))


@functools.lru_cache(maxsize=None)
def _filter_features(seq_len):
    n = 2 * seq_len
    m = np.arange(n)
    d = np.where(m < seq_len, m, n - m).astype(np.float64)
    t = d / (seq_len - 1)
    ang = 2.0 * math.pi * d / seq_len
    bands = (HY_EMB - 1) // 2
    fr = np.linspace(1e-4, bands - 1, bands)
    z = np.zeros((n, 128), np.float64)
    z[:, 0] = t
    z[:, 1:1 + bands] = np.cos(fr[None, :] * ang[:, None])
    z[:, 1 + bands:1 + 2 * bands] = -np.sin(fr[None, :] * ang[:, None])
    z[:, 33] = (m < seq_len)
    z[:, 34] = (m > seq_len)
    z[seq_len, :] = 0.0
    return z.astype(np.float32)


@functools.lru_cache(maxsize=None)
def _filter_deltas():
    d = np.abs(np.linspace(math.log(HY_TARGET) / HY_SLOW, math.log(HY_TARGET) / HY_FAST, HY_CH))
    return d.astype(np.float32).reshape(1, HY_CH)


def _hy_filter_kernel(z_ref, w1_ref, b1_ref, w2_ref, b2_ref, w3_ref, fq_ref, dl_ref, o_ref):
    z = z_ref[...]
    fq = fq_ref[...]
    h = jnp.sin(fq * (_dot(z, w1_ref[...]) + b1_ref[...]))
    h = jnp.sin(fq * (_dot(h, w2_ref[...]) + b2_ref[...]))
    o = _dot3(h, w3_ref)
    dec = jnp.exp(-z[:, 0:1] * dl_ref[...])
    mf = z[:, 33:34]
    mb = z[:, 34:35]
    outs = []
    for od in range(HY_ORDER):
        fwd = o[:, od * 2 * HY_CH:od * 2 * HY_CH + HY_CH]
        bwd = o[:, od * 2 * HY_CH + HY_CH:(od + 1) * 2 * HY_CH]
        outs.append((mf * fwd + mb * bwd) * dec)
    o_ref[...] = jnp.concatenate(outs, axis=1)


def _hy_filter(seq_len, w1, b1, w2, b2, w3, freq):
    n = 2 * seq_len
    tp = min(n, 1024)
    zt = jnp.asarray(_filter_features(seq_len))
    w1p = jnp.zeros((128, HY_FILT), F32).at[:HY_EMB].set(w1)
    full = lambda shape: pl.BlockSpec(shape, lambda i: (0,) * len(shape))
    return pl.pallas_call(
        _hy_filter_kernel,
        grid=(n // tp,),
        in_specs=[pl.BlockSpec((tp, 128), lambda i: (i, 0)),
                  full((128, HY_FILT)), full((1, HY_FILT)), full((HY_FILT, HY_FILT)), full((1, HY_FILT)),
                  full((2, HY_FILT, HY_ORDER * 2 * HY_CH)), full((1, HY_FILT)), full((1, HY_CH))],
        out_specs=pl.BlockSpec((tp, HY_ORDER * HY_CH), lambda i: (i, 0)),
        out_shape=jax.ShapeDtypeStruct((n, HY_ORDER * HY_CH), F32),
        compiler_params=_cparams(("parallel",)),
        name="hyena_filter_mlp",
    )(zt, w1p, b1.reshape(1, -1), w2, b2.reshape(1, -1), jnp.stack(_split(w3)), freq.reshape(1, -1),
      jnp.asarray(_filter_deltas()))


@functools.lru_cache(maxsize=None)
def _fft_tables():
    n1 = np.arange(FFT_N1)[:, None]
    k1 = np.arange(FFT_N1)[None, :]
    th = 2.0 * math.pi * n1 * k1 / FFT_N1
    c, s = np.cos(th), np.sin(th)
    half = FFT_N1 // 2
    big_n = FFT_N1 * FFT_N2
    wa_c = np.block([[c[:half], -s[:half]], [s[:half], c[:half]]])
    wa_r = np.concatenate([c, -s], axis=1)
    tw = 2.0 * math.pi * np.arange(FFT_N2)[:, None] * np.arange(FFT_N1)[None, :] / big_n
    twf = np.concatenate([np.cos(tw), -np.sin(tw)], axis=1)
    twi = np.concatenate([np.cos(tw.T), np.sin(tw.T)], axis=1)
    wb = np.block([[c, -s], [s, c]])
    wc = np.block([[c, s], [-s, c]])
    wd = np.block([[c[:, :half], s[:, :half]], [-s[:, :half], c[:, :half]]]) / big_n
    return tuple(np.asarray(a, np.float32) for a in (wa_c, wa_r, twf, twi, wb, wc, wd))


def _fft_fwd(z, wa_ref, tw, wb_ref):
    ct = z.shape[0]
    a = _dot3(z.reshape(ct * FFT_N2, z.shape[2]), wa_ref).reshape(ct, FFT_N2, 2 * FFT_N1)
    ar, ai = a[:, :, :FFT_N1], a[:, :, FFT_N1:]
    tr, ti = tw[:, :FFT_N1], tw[:, FFT_N1:]
    br = ar * tr - ai * ti
    bi = ar * ti + ai * tr
    bt = jnp.concatenate([jnp.swapaxes(br, 1, 2), jnp.swapaxes(bi, 1, 2)], axis=2)
    return _dot3(bt.reshape(ct * FFT_N1, 2 * FFT_N2), wb_ref).reshape(ct, FFT_N1, 2 * FFT_N2)


def _hy_kfft_kernel(k_ref, wa_ref, tw_ref, wb_ref, o_ref):
    o_ref[...] = _fft_fwd(k_ref[...], wa_ref, tw_ref[...], wb_ref)


def _hy_long_kernel(z_ref, g_ref, kh_ref, bias_ref, wa_ref, twf_ref, wb_ref, wc_ref, twi_ref, wd_ref, o_ref):
    z = z_ref[...]
    ct = z.shape[0]
    x = _fft_fwd(z, wa_ref, twf_ref[...], wb_ref)
    kh = kh_ref[...]
    xr, xi = x[:, :, :FFT_N2], x[:, :, FFT_N2:]
    kr, ki = kh[:, :, :FFT_N2], kh[:, :, FFT_N2:]
    y = jnp.concatenate([xr * kr - xi * ki, xr * ki + xi * kr], axis=2)
    c1 = _dot3(y.reshape(ct * FFT_N1, 2 * FFT_N2), wc_ref).reshape(ct, FFT_N1, 2 * FFT_N2)
    cr, ci = c1[:, :, :FFT_N2], c1[:, :, FFT_N2:]
    twi = twi_ref[...]
    tr, ti = twi[:, :FFT_N2], twi[:, FFT_N2:]
    dr = cr * tr - ci * ti
    di = cr * ti + ci * tr
    dt = jnp.concatenate([jnp.swapaxes(dr, 1, 2), jnp.swapaxes(di, 1, 2)], axis=2)
    conv = _dot3(dt.reshape(ct * FFT_N2, 2 * FFT_N1), wd_ref).reshape(ct, FFT_N2, FFT_N1)
    o_ref[...] = (conv + bias_ref[...] * z) * g_ref[...]


def _hy_ctx_kernel(x_ref, kc_ref, bias_ref, cf_ref, sf_ref, o_ref):
    x = x_ref[0]
    cf, sf = cf_ref[...], sf_ref[...]
    n = CTX_LEN
    z = x[:, :HY_CH]
    for od in range(HY_ORDER):
        k = kc_ref[:, od * HY_CH:(od + 1) * HY_CH]
        kr, ki = _dot(cf, k), -_dot(sf, k)
        xr, xi = _dot(cf[:, :n], z), -_dot(sf[:, :n], z)
        yr = xr * kr - xi * ki
        yi = xr * ki + xi * kr
        conv = (_dot(cf[:n, :], yr) - _dot(sf[:n, :], yi)) * (1.0 / (2 * n))
        z = x[:, (od + 1) * HY_CH:(od + 2) * HY_CH] * (conv + bias_ref[od:od + 1, :] * z)
    o_ref[0] = z


@functools.lru_cache(maxsize=None)
def _ctx_dft():
    m = np.arange(2 * CTX_LEN)
    th = 2.0 * math.pi * m[:, None] * m[None, :] / (2 * CTX_LEN)
    return np.cos(th).astype(np.float32), np.sin(th).astype(np.float32)


def _hyena(p_hy, conv_w, conv_b, w1, b1, w2, b2, w3, freq, bias):
    wa_c, wa_r, twf, twi, wb, wc, wd = (jnp.asarray(a) for a in _fft_tables())
    wa_c, wa_r, wb, wc, wd = (jnp.stack(_split(a)) for a in (wa_c, wa_r, wb, wc, wd))
    nch = HY_ORDER * HY_CH
    kf = _hy_filter(SEQ, w1, b1, w2, b2, w3, freq)
    kt = kf.reshape(FFT_N1, FFT_N2, nch).transpose(2, 1, 0)
    const = lambda shape: pl.BlockSpec(shape, lambda j: (0,) * len(shape))
    khat = pl.pallas_call(
        _hy_kfft_kernel,
        grid=(nch // FFT_CT,),
        in_specs=[pl.BlockSpec((FFT_CT, FFT_N2, FFT_N1), lambda j: (j, 0, 0)),
                  const((2, FFT_N1, 2 * FFT_N1)), const((FFT_N2, 2 * FFT_N1)), const((2, 2 * FFT_N2, 2 * FFT_N2))],
        out_specs=pl.BlockSpec((FFT_CT, FFT_N1, 2 * FFT_N2), lambda j: (j, 0, 0)),
        out_shape=jax.ShapeDtypeStruct((nch, FFT_N1, 2 * FFT_N2), F32),
        compiler_params=_cparams(("parallel",)),
        name="hyena_filter_fft",
    )(kt, wa_r, twf, wb)

    pc = _hy_short(p_hy, conv_w, conv_b, SEQ, 0)
    half = FFT_N1 // 2
    zall = pc.reshape(BATCH, half, FFT_N2, HY_IN).transpose(3, 2, 0, 1).reshape(HY_IN, FFT_N2, FFT_N1)
    nblk = HY_CH // FFT_CT
    z = zall
    for od in range(HY_ORDER):
        z_spec = pl.BlockSpec((FFT_CT, FFT_N2, FFT_N1), lambda j: (j, 0, 0))
        g_spec = pl.BlockSpec((FFT_CT, FFT_N2, FFT_N1), lambda j, od=od: ((od + 1) * nblk + j, 0, 0))
        z = pl.pallas_call(
            _hy_long_kernel,
            grid=(nblk,),
            in_specs=[z_spec, g_spec,
                      pl.BlockSpec((FFT_CT, FFT_N1, 2 * FFT_N2), lambda j, od=od: (od * nblk + j, 0, 0)),
                      pl.BlockSpec((FFT_CT, 1, 1), lambda j, od=od: (od * nblk + j, 0, 0)),
                      const((2, FFT_N1, 2 * FFT_N1)), const((FFT_N2, 2 * FFT_N1)), const((2, 2 * FFT_N2, 2 * FFT_N2)),
                      const((2, 2 * FFT_N2, 2 * FFT_N2)), const((FFT_N1, 2 * FFT_N2)), const((2, 2 * FFT_N1, FFT_N1))],
            out_specs=pl.BlockSpec((FFT_CT, FFT_N2, FFT_N1), lambda j: (j, 0, 0)),
            out_shape=jax.ShapeDtypeStruct((HY_CH, FFT_N2, FFT_N1), F32),
            compiler_params=_cparams(("parallel",)),
            name="hyena_long_conv",
        )(z, zall, khat, bias.reshape(nch, 1, 1), wa_c, twf, wb, wc, twi, wd)
    o_lat = z.reshape(HY_CH, FFT_N2, BATCH, half).transpose(2, 3, 1, 0).reshape(BATCH, SEQ, HY_CH)

    kc = _hy_filter(CTX_LEN, w1, b1, w2, b2, w3, freq)
    pcc = _hy_short(p_hy, conv_w, conv_b, CTX_LEN, SEQ // CTX_LEN)
    cf, sf = (jnp.asarray(a) for a in _ctx_dft())
    o_ctx = pl.pallas_call(
        _hy_ctx_kernel,
        grid=(BATCH,),
        in_specs=[pl.BlockSpec((1, CTX_LEN, HY_IN), lambda b: (b, 0, 0)),
                  pl.BlockSpec((2 * CTX_LEN, nch), lambda b: (0, 0)),
                  pl.BlockSpec((HY_ORDER, HY_CH), lambda b: (0, 0)),
                  pl.BlockSpec((2 * CTX_LEN, 2 * CTX_LEN), lambda b: (0, 0)),
                  pl.BlockSpec((2 * CTX_LEN, 2 * CTX_LEN), lambda b: (0, 0))],
        out_specs=pl.BlockSpec((1, CTX_LEN, HY_CH), lambda b: (b, 0, 0)),
        out_shape=jax.ShapeDtypeStruct((BATCH, CTX_LEN, HY_CH), F32),
        compiler_params=_cparams(("parallel",)),
        name="hyena_context",
    )(pcc, kc, bias, cf, sf)
    return jnp.concatenate([o_lat, o_ctx], axis=1)


@functools.lru_cache(maxsize=None)
def _rope_tables():
    nf = RW_HD // 4
    t = np.arange(SEQ)
    row = (t // GRID_W).astype(np.float64)
    col = (t % GRID_W).astype(np.float64)
    inv = ROPE_BASE ** (-np.arange(nf, dtype=np.float64) / nf)
    ang = np.concatenate([row[:, None] * inv] * 2 + [col[:, None] * inv] * 2, axis=-1)
    cos = np.concatenate([np.cos(ang), np.ones((CTX_LEN, RW_HD))], axis=0)
    sin = np.concatenate([np.sin(ang), np.zeros((CTX_LEN, RW_HD))], axis=0)
    return (np.tile(cos, (1, RW_HEADS)).astype(np.float32), np.tile(sin, (1, RW_HEADS)).astype(np.float32))


@functools.lru_cache(maxsize=None)
def _head_blockdiag():
    h = np.arange(RW_W) // RW_HD
    return (h[:, None] == h[None, :]).astype(np.float32)


@functools.lru_cache(maxsize=None)
def _chunk_tri():
    t = np.arange(RW_TR)[:, None]
    s = np.arange(RW_TR)[None, :]
    same = (t // RW_C) == (s // RW_C)
    return np.stack([same & (s <= t), same & (s >= t)]).astype(np.float32)


def _rope(x, cos, sin):
    lane = lax.broadcasted_iota(jnp.int32, (1, RW_W), 1)
    first = (lane % (RW_HD // 2)) < (RW_HD // 4)
    rot = jnp.where(first, -pltpu.roll(x, RW_W - RW_HD // 4, 1), pltpu.roll(x, RW_HD // 4, 1))
    return x * cos + rot * sin


def _rw_prep_kernel(p_ref, pv_ref, nx_ref, mu_ref, w0_ref, w2_ref, a0_ref, a2_ref, g2_ref, kk_ref, ka_ref, rk_ref,
                    cos_ref, sin_ref, bd_ref, tri_ref,
                    v_ref, bon_ref, g_ref, at_ref, rt_ref, bt_ref, kt_ref, wc_ref):
    i = pl.program_id(1)
    x = p_ref[0]
    row = lax.broadcasted_iota(jnp.int32, (RW_TR, 1), 0)
    t = i * RW_TR + row
    seq_first = (t == 0) | (t == SEQ)
    seq_last = (t == SEQ - 1) | (t == T_ALL - 1)
    prev = jnp.where(row == 0, pv_ref[0, 7:8, :], pltpu.roll(x, 1, 0))
    prev = jnp.where(seq_first, 0.0, prev)
    nxt = jnp.where(row == RW_TR - 1, nx_ref[0, 0:1, :], pltpu.roll(x, RW_TR - 1, 0))
    nxt = jnp.where(seq_last, 0.0, nxt)
    xs = x + mu_ref[0:1, :] * (prev - x) + mu_ref[1:2, :] * (nxt - x)
    r = xs[:, 0:RW_W]
    k = xs[:, RW_W:2 * RW_W]
    v = xs[:, 2 * RW_W:3 * RW_W]
    i1 = 3 * RW_W
    wd = jnp.tanh(xs[:, i1:i1 + 2 * RW_LORA])
    ad = xs[:, i1 + 2 * RW_LORA:i1 + 4 * RW_LORA]
    gd = xs[:, i1 + 4 * RW_LORA:]
    cos, sin, bd = cos_ref[...], sin_ref[...], bd_ref[...]
    g_ref[0] = _dot3(jax.nn.sigmoid(gd), g2_ref)
    kk = k * kk_ref[...]
    nrm = jnp.sqrt(_dot01(kk * kk, bd))
    kk = kk / jnp.maximum(nrm, 1e-12)
    ksum = jnp.zeros_like(k)
    r_s = _rope(r, cos, sin)
    na_s = -_rope(kk, cos, sin)
    n_chunks = RW_TR // RW_C
    for d in range(2):
        logit = w0_ref[d:d + 1, :] + _dot3(wd, w2_ref, (d,))
        logw = -jax.nn.softplus(-logit) - 0.5
        lw = -jnp.exp(logw)
        a = jax.nn.sigmoid(a0_ref[d:d + 1, :] + _dot3(ad, a2_ref, (d,)))
        kdir = k * (1.0 + (a - 1.0) * ka_ref[...])
        ksum = ksum + kdir
        cum = _dot01(lw, tri_ref[d], const_left=True)
        e_neg = jnp.exp(-cum)
        at_ref[d, 0] = na_s * jnp.exp(cum - lw)
        rt_ref[d, 0] = r_s * jnp.exp(cum)
        bt_ref[d, 0] = _rope(kk * a, cos, sin) * e_neg
        kt_ref[d, 0] = _rope(kdir, cos, sin) * e_neg
        tot = jnp.sum(lw.reshape(n_chunks, RW_C, RW_W), axis=1, keepdims=True)
        wc_ref[d, 0] = jnp.broadcast_to(jnp.exp(tot), (n_chunks, 8, RW_W))
    bon_ref[0] = _dot01(r * ksum * rk_ref[...], bd)
    v_ref[0] = v


def _bmm3(spec, a, b):
    (ah, am), (bh, bm) = a, b
    e = functools.partial(jnp.einsum, spec, preferred_element_type=F32)
    return e(ah, bh) + e(am, bh) + e(ah, bm)


def _rw_scan_kernel(v_ref, at_ref, rt_ref, bt_ref, kt_ref, wc_ref, y_ref, s_sc):
    d = pl.program_id(0)
    step = pl.program_id(1)
    nb = BATCH * RW_HEADS
    c = RW_C

    @pl.when(step == 0)
    def _():
        s_sc[...] = jnp.zeros_like(s_sc)

    def heads(x):
        return jnp.stack([x[b][:, h * RW_HD:(h + 1) * RW_HD] for b in range(BATCH) for h in range(RW_HEADS)])

    v = heads(v_ref[...])
    ar = jnp.concatenate([heads(at_ref[0]), heads(rt_ref[0])], axis=1)
    bt = heads(bt_ref[0])
    kt = heads(kt_ref[0])
    wc = heads(wc_ref[0, :, 0])[:, 0:1, :]

    ti = lax.broadcasted_iota(jnp.int32, (c, c), 0)
    si = lax.broadcasted_iota(jnp.int32, (c, c), 1)
    ahead = (ti - si) * jnp.where(d == 0, 1, -1)
    strict = (ahead > 0)[None]
    incl = (ahead >= 0)[None]
    eye = (si == ti).astype(F32)[None]

    ar_s, bt_s, kt_s, v_s = _split(ar), _split(bt), _split(kt), _split(v)
    lb = _bmm3('htj,hsj->hts', ar_s, bt_s)
    lk = _bmm3('htj,hsj->hts', ar_s, kt_s)
    lab = jnp.where(strict, lb[:, :c], 0.0)
    lrb = jnp.where(incl, lb[:, c:], 0.0)
    lak = jnp.where(strict, lk[:, :c], 0.0)
    lrk = jnp.where(incl, lk[:, c:], 0.0)

    tt = eye + lab
    pw = lab
    for _ in range(int(math.log2(c)) - 1):
        pw_s = _split(pw)
        pw = _bmm3('hts,hsu->htu', pw_s, pw_s)
        tt = tt + _bmm3('hts,hsu->htu', _split(tt), _split(pw))

    s0 = s_sc[...]
    ars = _bmm3('htk,hvk->htv', ar_s, _split(s0))
    lv = _bmm3('hts,hsv->htv', _split(jnp.concatenate([lak, lrk], axis=1)), v_s)
    u = _bmm3('hts,hsv->htv', _split(tt), _split(lv[:, :c] + ars[:, :c]))
    y = ars[:, c:] + _bmm3('hts,hsv->htv', _split(lrb), _split(u)) + lv[:, c:]
    for b in range(BATCH):
        y_ref[0, b] = jnp.concatenate([y[b * RW_HEADS + h] for h in range(RW_HEADS)], axis=-1)
    uv = jnp.concatenate([u, v], axis=1)
    bk = jnp.concatenate([bt, kt], axis=1)
    s_sc[...] = (s0 + _bmm3('htv,htj->hvj', _split(uv), _split(bk))) * wc


def _rw_out_kernel(yf_ref, yb_ref, bon_ref, v_ref, g_ref, lnw_ref, lnb_ref, bd_ref, o_ref):
    y = yf_ref[0, 0] + yb_ref[0, 0]
    bd = bd_ref[...]
    mean = _dot01(y, bd) * (1.0 / RW_HD)
    yc = y - mean
    var = _dot01(yc * yc, bd) * (1.0 / RW_HD)
    yn = yc * lax.rsqrt(var + RW_GN_EPS) * lnw_ref[...] + lnb_ref[...]
    o_ref[0] = (yn + bon_ref[0] * v_ref[0]) * g_ref[0]


def _rw_chunk_index(d, s):
    n_lat = SEQ // RW_C
    n_all = T_ALL // RW_C
    fwd = jnp.where(s < n_all - n_lat, n_lat + s, s - (n_all - n_lat))
    return jnp.where(d == 0, fwd, n_all - 1 - s)


def _rwkv(p_rw, mu, w0, w2, a0, a2, g2, k_k, k_a, r_k, ln_w, ln_b):
    cos, sin = (jnp.asarray(a) for a in _rope_tables())
    bd = jnp.asarray(_head_blockdiag()).astype(BF16)
    zpad = lambda w: jnp.stack([jnp.concatenate([w[0], jnp.zeros_like(w[0])], axis=0),
                                jnp.concatenate([jnp.zeros_like(w[1]), w[1]], axis=0)])
    split = lambda w: jnp.stack(_split(w), axis=1)
    nt = T_ALL // RW_TR
    n_ch = T_ALL // RW_C
    tok_shape = jax.ShapeDtypeStruct((BATCH, T_ALL, RW_W), F32)
    dir_shape = jax.ShapeDtypeStruct((2, BATCH, T_ALL, RW_W), F32)
    dir_spec = pl.BlockSpec((2, 1, RW_TR, RW_W), lambda b, i: (0, b, i, 0))
    wc_shape = jax.ShapeDtypeStruct((2, BATCH, n_ch, 8, RW_W), F32)
    wc_spec = pl.BlockSpec((2, 1, RW_TR // RW_C, 8, RW_W), lambda b, i: (0, b, i, 0, 0))
    tok = lambda w: pl.BlockSpec((1, RW_TR, w), lambda b, i: (b, i, 0))
    const = lambda shape: pl.BlockSpec(shape, lambda b, i: (0,) * len(shape))
    row = lambda a: a.reshape(1, RW_W)
    outs = pl.pallas_call(
        _rw_prep_kernel,
        grid=(BATCH, nt),
        in_specs=[tok(RW_IN),
                  pl.BlockSpec((1, 8, RW_IN), lambda b, i: (b, jnp.maximum(i * (RW_TR // 8) - 1, 0), 0)),
                  pl.BlockSpec((1, 8, RW_IN), lambda b, i: (b, jnp.minimum((i + 1) * (RW_TR // 8), T_ALL // 8 - 1), 0)),
                  const((2, RW_IN)), const((2, RW_W)), const((2, 2, 2 * RW_LORA, RW_W)), const((2, RW_W)),
                  const((2, 2, 2 * RW_LORA, RW_W)), const((2, RW_GATE_LORA, RW_W)),
                  const((1, RW_W)), const((1, RW_W)), const((1, RW_W)),
                  pl.BlockSpec((RW_TR, RW_W), lambda b, i: (i, 0)), pl.BlockSpec((RW_TR, RW_W), lambda b, i: (i, 0)),
                  const((RW_W, RW_W)), const((2, RW_TR, RW_TR))],
        out_specs=[tok(RW_W)] * 3 + [dir_spec] * 4 + [wc_spec],
        out_shape=[tok_shape] * 3 + [dir_shape] * 4 + [wc_shape],
        compiler_params=_cparams(("parallel", "parallel")),
        name="rwkv_prepare",
    )(p_rw, p_rw, p_rw, mu, w0, split(zpad(w2)), a0, split(zpad(a2)), jnp.stack(_split(g2)),
      row(k_k), row(k_a), row(r_k.reshape(-1)), cos, sin, bd, jnp.asarray(_chunk_tri()).astype(BF16))
    v, bonus, g, at, rt, bt, kt, wc = outs

    perdir = pl.BlockSpec((1, BATCH, RW_C, RW_W), lambda d, s: (d, 0, _rw_chunk_index(d, s), 0))
    yt = pl.pallas_call(
        _rw_scan_kernel,
        grid=(2, n_ch),
        in_specs=[pl.BlockSpec((BATCH, RW_C, RW_W), lambda d, s: (0, _rw_chunk_index(d, s), 0)),
                  perdir, perdir, perdir, perdir,
                  pl.BlockSpec((1, BATCH, 1, 8, RW_W), lambda d, s: (d, 0, _rw_chunk_index(d, s), 0, 0))],
        out_specs=perdir,
        out_shape=dir_shape,
        scratch_shapes=[pltpu.VMEM((BATCH * RW_HEADS, RW_HD, RW_HD), F32)],
        compiler_params=_cparams(("parallel", "arbitrary")),
        name="rwkv_chunk_scan",
    )(v, at, rt, bt, kt, wc)

    tok2 = lambda w: pl.BlockSpec((1, PROJ_TM, w), lambda b, i: (b, i, 0))
    return pl.pallas_call(
        _rw_out_kernel,
        grid=(BATCH, T_ALL // PROJ_TM),
        in_specs=[pl.BlockSpec((1, 1, PROJ_TM, RW_W), lambda b, i: (0, b, i, 0)),
                  pl.BlockSpec((1, 1, PROJ_TM, RW_W), lambda b, i: (1, b, i, 0)),
                  tok2(RW_W), tok2(RW_W), tok2(RW_W),
                  const((1, RW_W)), const((1, RW_W)), const((RW_W, RW_W))],
        out_specs=tok2(RW_W),
        out_shape=jax.ShapeDtypeStruct((BATCH, T_ALL, RW_W), F32),
        compiler_params=_cparams(("parallel", "parallel")),
        name="rwkv_output",
    )(yt, yt, bonus, v, g, row(ln_w), row(ln_b), bd)


@functools.lru_cache(maxsize=None)
def _na_bias_tables():
    qc = np.arange(GRID_W)[:, None]
    kc = np.arange(GRID_W)[None, :]
    win0 = np.clip(qc - WIN_COLS // 2, 0, GRID_W - WIN_COLS)
    valid = (kc >= win0) & (kc < win0 + WIN_COLS)
    off = np.clip(kc - qc + WIN_COLS - 1, 0, 2 * WIN_COLS - 2)
    onehot = np.zeros((128, GRID_W * GRID_W), np.float32)
    onehot[off.reshape(-1), np.arange(GRID_W * GRID_W)] = valid.reshape(-1)
    mask = np.where(valid, 0.0, NEG_INF).astype(np.float32).reshape(1, GRID_W * GRID_W)
    return onehot, mask


def _na_bias_kernel(rpb_ref, oh_ref, mask_ref, o_ref):
    o_ref[...] = _dot(rpb_ref[...], oh_ref[...]) + mask_ref[...]


def _na_lat_kernel(q_ref, k_ref, v_ref, tp_ref, o_ref):
    i = pl.program_id(1)
    start = jnp.clip(i - WIN_ROWS // 2, 0, N_ROWS - WIN_ROWS)
    dr0 = start - i + WIN_ROWS - 1
    scale = NA_HD ** -0.5
    lane = lax.broadcasted_iota(jnp.int32, (1, NA_W), 1)
    hmask = [(lane // NA_HD) == h for h in range(NA_HEADS)]
    q = q_ref[0]
    qm = jnp.concatenate([jnp.where(hmask[h], q, jnp.zeros_like(q)) for h in range(NA_HEADS)], axis=0)
    row0 = pl.multiple_of(start * GRID_W, GRID_W)
    kl = k_ref[0, pl.ds(row0, WIN_ROWS * GRID_W), :]
    vl = v_ref[0, pl.ds(row0, WIN_ROWS * GRID_W), :]
    kc = k_ref[0, SEQ:T_ALL, :]
    vc = v_ref[0, SEQ:T_ALL, :]
    nt = (((1,), (1,)), ((), ()))
    bias = jnp.concatenate(
        [jnp.concatenate([tp_ref[h, pl.ds(dr0 + 2 * j, 1)][0] for j in range(WIN_ROWS // 2)], axis=1)
         for h in range(NA_HEADS)], axis=0)
    s_loc = lax.dot_general(qm, kl, nt, preferred_element_type=F32) * scale + bias
    s_ctx = lax.dot_general(qm, kc, nt, preferred_element_type=F32) * scale
    m = jnp.maximum(jnp.max(s_loc, axis=-1, keepdims=True), jnp.max(s_ctx, axis=-1, keepdims=True))
    p_loc = jnp.exp(s_loc - m)
    p_ctx = jnp.exp(s_ctx - m)
    inv = 1.0 / (jnp.sum(p_loc, axis=-1, keepdims=True) + jnp.sum(p_ctx, axis=-1, keepdims=True))
    o_all = (jnp.dot((p_loc * inv).astype(BF16), vl, preferred_element_type=F32)
             + jnp.dot((p_ctx * inv).astype(BF16), vc, preferred_element_type=F32))
    o = jnp.zeros((GRID_W, NA_W), F32)
    for h in range(NA_HEADS):
        o = o + jnp.where(hmask[h], o_all[h * GRID_W:(h + 1) * GRID_W, :], 0.0)
    o_ref[0] = o


def _na_ctx_kernel(q_ref, k_ref, v_ref, o_ref):
    scale = NA_HD ** -0.5
    lane = lax.broadcasted_iota(jnp.int32, (1, NA_W), 1)
    hmask = [(lane // NA_HD) == h for h in range(NA_HEADS)]
    q, k, v = q_ref[0], k_ref[0], v_ref[0]
    qm = jnp.concatenate([jnp.where(hmask[h], q, jnp.zeros_like(q)) for h in range(NA_HEADS)], axis=0)
    s = lax.dot_general(qm, k, (((1,), (1,)), ((), ())), preferred_element_type=F32) * scale
    p = jnp.exp(s - jnp.max(s, axis=-1, keepdims=True))
    p = p / jnp.sum(p, axis=-1, keepdims=True)
    o_all = jnp.dot(p.astype(BF16), v, preferred_element_type=F32)
    o = jnp.zeros((CTX_LEN, NA_W), F32)
    for h in range(NA_HEADS):
        o = o + jnp.where(hmask[h], o_all[h * CTX_LEN:(h + 1) * CTX_LEN, :], 0.0)
    o_ref[0] = o


def _natten(p_na, rpb):
    onehot, mask = (jnp.asarray(a) for a in _na_bias_tables())
    n_dr = 2 * WIN_ROWS - 1
    n_rel = 2 * WIN_COLS - 1
    rows = NA_HEADS * n_dr
    rows_p = -(-rows // 8) * 8
    rpb2 = jnp.zeros((rows_p, 128), F32).at[:rows, :n_rel].set(rpb.reshape(rows, n_rel))
    tb = pl.pallas_call(
        _na_bias_kernel,
        out_shape=jax.ShapeDtypeStruct((rows_p, GRID_W * GRID_W), F32),
        name="natten_bias_expand",
    )(rpb2, onehot, mask)
    tb = tb[:rows].reshape(NA_HEADS, n_dr, GRID_W, GRID_W)
    nxt = jnp.concatenate([tb[:, 1:], jnp.full_like(tb[:, :1], NEG_INF)], axis=1)
    tp = jnp.concatenate([tb, nxt], axis=-1)

    o_lat = pl.pallas_call(
        _na_lat_kernel,
        grid=(BATCH, N_ROWS),
        in_specs=[pl.BlockSpec((1, GRID_W, NA_W), lambda b, i: (b, i, 0)),
                  pl.BlockSpec((1, T_ALL, NA_W), lambda b, i: (b, 0, 1)),
                  pl.BlockSpec((1, T_ALL, NA_W), lambda b, i: (b, 0, 2)),
                  pl.BlockSpec((NA_HEADS, n_dr, GRID_W, 2 * GRID_W), lambda b, i: (0, 0, 0, 0))],
        out_specs=pl.BlockSpec((1, GRID_W, NA_W), lambda b, i: (b, i, 0)),
        out_shape=jax.ShapeDtypeStruct((BATCH, SEQ, NA_W), F32),
        compiler_params=_cparams(("parallel", "parallel")),
        name="natten_latent",
    )(p_na, p_na, p_na, tp)

    cblk = SEQ // CTX_LEN
    o_ctx = pl.pallas_call(
        _na_ctx_kernel,
        grid=(BATCH,),
        in_specs=[pl.BlockSpec((1, CTX_LEN, NA_W), lambda b: (b, cblk, 0)),
                  pl.BlockSpec((1, CTX_LEN, NA_W), lambda b: (b, cblk, 1)),
                  pl.BlockSpec((1, CTX_LEN, NA_W), lambda b: (b, cblk, 2))],
        out_specs=pl.BlockSpec((1, CTX_LEN, NA_W), lambda b: (b, 0, 0)),
        out_shape=jax.ShapeDtypeStruct((BATCH, CTX_LEN, NA_W), F32),
        compiler_params=_cparams(("parallel",)),
        name="natten_context",
    )(p_na, p_na, p_na)
    return jnp.concatenate([o_lat, o_ctx], axis=1)


def kernel(x, c, ctx, c_ctx, mod_w, mod_b, norm_g, ffn1_wgu, ffn1_wdn, ffn2_wgu, ffn2_wdn, w_in, w_out, hy_conv_w, hy_conv_b, hy_f_w1, hy_f_b1, hy_f_w2, hy_f_b2, hy_f_w3, hy_freq, hy_bias, rw_mu, rw_w0, rw_w2, rw_a0, rw_a2, rw_g2, rw_k_k, rw_k_a, rw_r_k, rw_ln_w, rw_ln_b, na_rpb):
    assert x.shape == (BATCH, SEQ, D_MODEL) and ctx.shape == (BATCH, CTX_LEN, D_MODEL)
    mods = _modulation(c, c_ctx, mod_w, mod_b)
    h = jnp.concatenate([x, ctx], axis=1)
    gpad = jnp.concatenate([norm_g, jnp.zeros((DEPTH, 2, D_MODEL), F32)], axis=1)
    for l in range(DEPTH):
        m_l, g_l = mods[l], gpad[l]
        h = _ffn(h, m_l, g_l, ffn1_wgu[l].astype(BF16), ffn1_wdn[l].astype(BF16), 0)
        p_hy, p_rw, p_na = _in_proj(h, m_l, g_l, w_in[l].astype(BF16))
        o_hy = _hyena(p_hy, hy_conv_w[l], hy_conv_b[l], hy_f_w1[l], hy_f_b1[l], hy_f_w2[l], hy_f_b2[l],
                      hy_f_w3[l], hy_freq[l], hy_bias[l])
        o_rw = _rwkv(p_rw, rw_mu[l], rw_w0[l], rw_w2[l], rw_a0[l], rw_a2[l], rw_g2[l], rw_k_k[l], rw_k_a[l],
                     rw_r_k[l], rw_ln_w[l], rw_ln_b[l])
        o_na = _natten(p_na, na_rpb[l])
        h = _out_proj(h, o_hy, o_rw, o_na, m_l, g_l, w_out[l].astype(BF16))
        h = _ffn(h, m_l, g_l, ffn2_wgu[l].astype(BF16), ffn2_wdn[l].astype(BF16), 1)
    return h[:, :SEQ]
```

```python
import functools
import math

import numpy as np
import jax
import jax.numpy as jnp
from jax import lax
from jax.experimental import pallas as pl
from jax.experimental.pallas import tpu as pltpu

D_MODEL = 1024
BATCH = 2
SEQ = 8192
DEPTH = 4
GRID_W = 64
CTX_LEN = 256
T_ALL = SEQ + CTX_LEN
N_MOD = 9
D_FF = 2816
MACARON_W = 0.5
NORM_EPS = 1e-6
NEG_INF = -1e30

HY_CH = 256
HY_ORDER = 2
HY_EMB = 33
HY_FILT = 64
HY_TARGET = 1e-2
HY_FAST = 0.3
HY_SLOW = 1.5
HY_IN = (HY_ORDER + 1) * HY_CH

RW_HEADS = 6
RW_HD = 64
RW_W = RW_HEADS * RW_HD
RW_LORA = 64
RW_GATE_LORA = 128
RW_GN_EPS = 64e-5
ROPE_BASE = 10000.0
RW_IN = 3 * RW_W + 4 * RW_LORA + RW_GATE_LORA

NA_HEADS = 6
NA_HD = 64
NA_W = NA_HEADS * NA_HD
WIN_ROWS = 8
WIN_COLS = 16
NA_IN = 3 * NA_W
N_ROWS = SEQ // GRID_W

F32 = jnp.float32
BF16 = jnp.bfloat16
HI = lax.Precision.HIGHEST

VMEM_LIMIT = 56 * 1024 * 1024

FFN_TM = 768
FFN_TF = 1408
PROJ_TM = 384
RW_TR = 256
RW_C = 64
FFT_N1 = 128
FFT_N2 = 128
FFT_CT = 16


def _dot(a, b):
    return jnp.dot(a, b, precision=HI, preferred_element_type=F32)


def _split(x):
    hi = x.astype(BF16)
    return hi, (x - hi.astype(F32)).astype(BF16)


def _dot3(x, w_ref, idx=()):
    xh, xm = _split(x)
    wh, wm = w_ref[idx + (0,)], w_ref[idx + (1,)]
    d = functools.partial(jnp.dot, preferred_element_type=F32)
    return d(xh, wh) + d(xm, wh) + d(xh, wm)


def _split3(x):
    hi = x.astype(BF16)
    r1 = x - hi.astype(F32)
    mid = r1.astype(BF16)
    return hi, mid, (r1 - mid.astype(F32)).astype(BF16)


def _dot01(x, m, const_left=False):
    d = functools.partial(jnp.dot, preferred_element_type=F32)
    parts = _split3(x)
    if const_left:
        return d(m, parts[0]) + d(m, parts[1]) + d(m, parts[2])
    return d(parts[0], m) + d(parts[1], m) + d(parts[2], m)


def _cparams(sem, vmem=VMEM_LIMIT):
    return pltpu.CompilerParams(dimension_semantics=sem, vmem_limit_bytes=vmem)


def _mod_kernel(cv_ref, w_ref, b_ref, o_ref):
    cv = cv_ref[...]
    s = cv * jax.nn.sigmoid(cv)
    o_ref[0] = _dot(s, w_ref[0]) + b_ref[0]


def _modulation(c, c_ctx, mod_w, mod_b):
    cv = jnp.concatenate([c, c_ctx[None, :], jnp.zeros((8 - BATCH - 1, D_MODEL), F32)], axis=0)
    tn = 1536
    return pl.pallas_call(
        _mod_kernel,
        grid=(DEPTH, N_MOD * D_MODEL // tn),
        in_specs=[pl.BlockSpec((8, D_MODEL), lambda l, j: (0, 0)),
                  pl.BlockSpec((1, D_MODEL, tn), lambda l, j: (l, 0, j)),
                  pl.BlockSpec((1, 1, tn), lambda l, j: (l, 0, j))],
        out_specs=pl.BlockSpec((1, 8, tn), lambda l, j: (l, 0, j)),
        out_shape=jax.ShapeDtypeStruct((DEPTH, 8, N_MOD * D_MODEL), F32),
        compiler_params=_cparams(("parallel", "parallel")),
        name="adaln_modulation",
    )(cv, mod_w, mod_b.reshape(DEPTH, 1, N_MOD * D_MODEL))


def _mod_rows(m_ref, b, i, tm, idx):
    t = i * tm + lax.broadcasted_iota(jnp.int32, (tm, 1), 0)
    ml = m_ref[pl.ds(b, 1), idx * D_MODEL:(idx + 1) * D_MODEL]
    mc = m_ref[BATCH:BATCH + 1, idx * D_MODEL:(idx + 1) * D_MODEL]
    return jnp.where(t >= SEQ, mc, ml)


def _rms(x, g):
    return x * lax.rsqrt(jnp.mean(x * x, axis=-1, keepdims=True) + NORM_EPS) * g


def _ffn_kernel(h_ref, m_ref, g_ref, wg_ref, wu_ref, wd_ref, o_ref, u_sc, acc_sc, *, mi, gi):
    b, i, k = pl.program_id(0), pl.program_id(1), pl.program_id(2)

    @pl.when(k == 0)
    def _():
        y = _rms(h_ref[0], g_ref[gi:gi + 1, :])
        u = y * (1.0 + _mod_rows(m_ref, b, i, FFN_TM, mi + 1)) + _mod_rows(m_ref, b, i, FFN_TM, mi)
        u_sc[...] = u.astype(BF16)
        acc_sc[...] = jnp.zeros_like(acc_sc)

    u = u_sc[...]
    gate = jnp.dot(u, wg_ref[...], preferred_element_type=F32)
    up = jnp.dot(u, wu_ref[...], preferred_element_type=F32)
    a = (gate * jax.nn.sigmoid(gate) * up).astype(BF16)
    acc_sc[...] += jnp.dot(a, wd_ref[...], preferred_element_type=F32)

    @pl.when(k == pl.num_programs(2) - 1)
    def _():
        y = _rms(acc_sc[...], g_ref[gi + 1:gi + 2, :])
        o_ref[0] = h_ref[0] + MACARON_W * _mod_rows(m_ref, b, i, FFN_TM, mi + 2) * y


def _ffn(h, mods_l, g_l, wgu, wdn, which):
    mi, gi = (0, 0) if which == 0 else (6, 4)
    nk = D_FF // FFN_TF
    return pl.pallas_call(
        functools.partial(_ffn_kernel, mi=mi, gi=gi),
        grid=(BATCH, T_ALL // FFN_TM, nk),
        in_specs=[pl.BlockSpec((1, FFN_TM, D_MODEL), lambda b, i, k: (b, i, 0)),
                  pl.BlockSpec((8, N_MOD * D_MODEL), lambda b, i, k: (0, 0)),
                  pl.BlockSpec((8, D_MODEL), lambda b, i, k: (0, 0)),
                  pl.BlockSpec((D_MODEL, FFN_TF), lambda b, i, k: (0, k)),
                  pl.BlockSpec((D_MODEL, FFN_TF), lambda b, i, k: (0, k + D_FF // FFN_TF)),
                  pl.BlockSpec((FFN_TF, D_MODEL), lambda b, i, k: (k, 0))],
        out_specs=pl.BlockSpec((1, FFN_TM, D_MODEL), lambda b, i, k: (b, i, 0)),
        out_shape=jax.ShapeDtypeStruct((BATCH, T_ALL, D_MODEL), F32),
        scratch_shapes=[pltpu.VMEM((FFN_TM, D_MODEL), BF16), pltpu.VMEM((FFN_TM, D_MODEL), F32)],
        compiler_params=_cparams(("parallel", "parallel", "arbitrary")),
        name="ffn_sublayer",
    )(h, mods_l, g_l, wgu, wgu, wdn)


def _inproj_kernel(h_ref, m_ref, g_ref, w_ref, hy_ref, rw_ref, na_ref):
    b, i = pl.program_id(0), pl.program_id(1)
    y = _rms(h_ref[0], g_ref[2:3, :])
    u = y * (1.0 + _mod_rows(m_ref, b, i, PROJ_TM, 4)) + _mod_rows(m_ref, b, i, PROJ_TM, 3)
    p = jnp.dot(u.astype(BF16), w_ref[...], preferred_element_type=F32)
    hy_ref[0] = p[:, :HY_IN]
    rw_ref[0] = p[:, HY_IN:HY_IN + RW_IN]
    na_ref[0] = p[:, HY_IN + RW_IN:].astype(BF16)


def _in_proj(h, mods_l, g_l, w_in):
    n_in = HY_IN + RW_IN + NA_IN
    return pl.pallas_call(
        _inproj_kernel,
        grid=(BATCH, T_ALL // PROJ_TM),
        in_specs=[pl.BlockSpec((1, PROJ_TM, D_MODEL), lambda b, i: (b, i, 0)),
                  pl.BlockSpec((8, N_MOD * D_MODEL), lambda b, i: (0, 0)),
                  pl.BlockSpec((8, D_MODEL), lambda b, i: (0, 0)),
                  pl.BlockSpec((D_MODEL, n_in), lambda b, i: (0, 0))],
        out_specs=[pl.BlockSpec((1, PROJ_TM, HY_IN), lambda b, i: (b, i, 0)),
                   pl.BlockSpec((1, PROJ_TM, RW_IN), lambda b, i: (b, i, 0)),
                   pl.BlockSpec((1, PROJ_TM, NA_IN), lambda b, i: (b, i, 0))],
        out_shape=[jax.ShapeDtypeStruct((BATCH, T_ALL, HY_IN), F32),
                   jax.ShapeDtypeStruct((BATCH, T_ALL, RW_IN), F32),
                   jax.ShapeDtypeStruct((BATCH, T_ALL, NA_IN), BF16)],
        compiler_params=_cparams(("parallel", "parallel")),
        name="mixer_in_proj",
    )(h, mods_l, g_l, w_in)


def _outproj_kernel(h_ref, hy_ref, rw_ref, na_ref, m_ref, g_ref, w_ref, o_ref):
    b, i = pl.program_id(0), pl.program_id(1)
    cat = jnp.concatenate([hy_ref[0], rw_ref[0], na_ref[0]], axis=-1).astype(BF16)
    o = jnp.dot(cat, w_ref[...], preferred_element_type=F32)
    o_ref[0] = h_ref[0] + _mod_rows(m_ref, b, i, PROJ_TM, 5) * _rms(o, g_ref[3:4, :])


def _out_proj(h, o_hy, o_rw, o_na, mods_l, g_l, w_out):
    return pl.pallas_call(
        _outproj_kernel,
        grid=(BATCH, T_ALL // PROJ_TM),
        in_specs=[pl.BlockSpec((1, PROJ_TM, D_MODEL), lambda b, i: (b, i, 0)),
                  pl.BlockSpec((1, PROJ_TM, HY_CH), lambda b, i: (b, i, 0)),
                  pl.BlockSpec((1, PROJ_TM, RW_W), lambda b, i: (b, i, 0)),
                  pl.BlockSpec((1, PROJ_TM, NA_W), lambda b, i: (b, i, 0)),
                  pl.BlockSpec((8, N_MOD * D_MODEL), lambda b, i: (0, 0)),
                  pl.BlockSpec((8, D_MODEL), lambda b, i: (0, 0)),
                  pl.BlockSpec((D_MODEL, D_MODEL), lambda b, i: (0, 0))],
        out_specs=pl.BlockSpec((1, PROJ_TM, D_MODEL), lambda b, i: (b, i, 0)),
        out_shape=jax.ShapeDtypeStruct((BATCH, T_ALL, D_MODEL), F32),
        compiler_params=_cparams(("parallel", "parallel")),
        name="mixer_out_proj",
    )(h, o_hy, o_rw, o_na, mods_l, g_l, w_out)


def _hy_short_kernel(x_ref, w_ref, b_ref, o_ref):
    x = x_ref[0]
    n = x.shape[0]
    row = lax.broadcasted_iota(jnp.int32, (n, 1), 0)
    prev = jnp.where(row == 0, 0.0, pltpu.roll(x, 1, 0))
    nxt = jnp.where(row == n - 1, 0.0, pltpu.roll(x, n - 1, 0))
    o_ref[0] = prev * w_ref[0:1, :] + x * w_ref[1:2, :] + nxt * w_ref[2:3, :] + b_ref[...]


def _hy_short(p_hy, conv_w, conv_b, seq_len, row_block):
    lanes = 128
    return pl.pallas_call(
        _hy_short_kernel,
        grid=(BATCH, HY_IN // lanes),
        in_specs=[pl.BlockSpec((1, seq_len, lanes), lambda b, j: (b, row_block, j)),
                  pl.BlockSpec((3, lanes), lambda b, j: (0, j)),
                  pl.BlockSpec((1, lanes), lambda b, j: (0, j))],
        out_specs=pl.BlockSpec((1, seq_len, lanes), lambda b, j: (b, 0, j)),
        out_shape=jax.ShapeDtypeStruct((BATCH, seq_len, HY_IN), F32),
        compiler_params=_cparams(("parallel", "parallel")),
        name="hyena_short_conv",
    )(p_hy, conv_w, conv_b.reshape(1, HY_IN))


@functools.lru_cache(maxsize=None)
def _filter_features(seq_len):
    n = 2 * seq_len
    m = np.arange(n)
    d = np.where(m < seq_len, m, n - m).astype(np.float64)
    t = d / (seq_len - 1)
    ang = 2.0 * math.pi * d / seq_len
    bands = (HY_EMB - 1) // 2
    fr = np.linspace(1e-4, bands - 1, bands)
    z = np.zeros((n, 128), np.float64)
    z[:, 0] = t
    z[:, 1:1 + bands] = np.cos(fr[None, :] * ang[:, None])
    z[:, 1 + bands:1 + 2 * bands] = -np.sin(fr[None, :] * ang[:, None])
    z[:, 33] = (m < seq_len)
    z[:, 34] = (m > seq_len)
    z[seq_len, :] = 0.0
    return z.astype(np.float32)


@functools.lru_cache(maxsize=None)
def _filter_deltas():
    d = np.abs(np.linspace(math.log(HY_TARGET) / HY_SLOW, math.log(HY_TARGET) / HY_FAST, HY_CH))
    return d.astype(np.float32).reshape(1, HY_CH)


def _hy_filter_kernel(z_ref, w1_ref, b1_ref, w2_ref, b2_ref, w3_ref, fq_ref, dl_ref, o_ref):
    z = z_ref[...]
    fq = fq_ref[...]
    h = jnp.sin(fq * (_dot(z, w1_ref[...]) + b1_ref[...]))
    h = jnp.sin(fq * (_dot(h, w2_ref[...]) + b2_ref[...]))
    o = _dot3(h, w3_ref)
    dec = jnp.exp(-z[:, 0:1] * dl_ref[...])
    mf = z[:, 33:34]
    mb = z[:, 34:35]
    outs = []
    for od in range(HY_ORDER):
        fwd = o[:, od * 2 * HY_CH:od * 2 * HY_CH + HY_CH]
        bwd = o[:, od * 2 * HY_CH + HY_CH:(od + 1) * 2 * HY_CH]
        outs.append((mf * fwd + mb * bwd) * dec)
    o_ref[...] = jnp.concatenate(outs, axis=1)


def _hy_filter(seq_len, w1, b1, w2, b2, w3, freq):
    n = 2 * seq_len
    tp = min(n, 1024)
    zt = jnp.asarray(_filter_features(seq_len))
    w1p = jnp.zeros((128, HY_FILT), F32).at[:HY_EMB].set(w1)
    full = lambda shape: pl.BlockSpec(shape, lambda i: (0,) * len(shape))
    return pl.pallas_call(
        _hy_filter_kernel,
        grid=(n // tp,),
        in_specs=[pl.BlockSpec((tp, 128), lambda i: (i, 0)),
                  full((128, HY_FILT)), full((1, HY_FILT)), full((HY_FILT, HY_FILT)), full((1, HY_FILT)),
                  full((2, HY_FILT, HY_ORDER * 2 * HY_CH)), full((1, HY_FILT)), full((1, HY_CH))],
        out_specs=pl.BlockSpec((tp, HY_ORDER * HY_CH), lambda i: (i, 0)),
        out_shape=jax.ShapeDtypeStruct((n, HY_ORDER * HY_CH), F32),
        compiler_params=_cparams(("parallel",)),
        name="hyena_filter_mlp",
    )(zt, w1p, b1.reshape(1, -1), w2, b2.reshape(1, -1), jnp.stack(_split(w3)), freq.reshape(1, -1),
      jnp.asarray(_filter_deltas()))


@functools.lru_cache(maxsize=None)
def _fft_tables():
    n1 = np.arange(FFT_N1)[:, None]
    k1 = np.arange(FFT_N1)[None, :]
    th = 2.0 * math.pi * n1 * k1 / FFT_N1
    c, s = np.cos(th), np.sin(th)
    half = FFT_N1 // 2
    big_n = FFT_N1 * FFT_N2
    wa_c = np.block([[c[:half], -s[:half]], [s[:half], c[:half]]])
    wa_r = np.concatenate([c, -s], axis=1)
    tw = 2.0 * math.pi * np.arange(FFT_N2)[:, None] * np.arange(FFT_N1)[None, :] / big_n
    twf = np.concatenate([np.cos(tw), -np.sin(tw)], axis=1)
    twi = np.concatenate([np.cos(tw.T), np.sin(tw.T)], axis=1)
    wb = np.block([[c, -s], [s, c]])
    wc = np.block([[c, s], [-s, c]])
    wd = np.block([[c[:, :half], s[:, :half]], [-s[:, :half], c[:, :half]]]) / big_n
    return tuple(np.asarray(a, np.float32) for a in (wa_c, wa_r, twf, twi, wb, wc, wd))


def _fft_fwd(z, wa_ref, tw, wb_ref):
    ct = z.shape[0]
    a = _dot3(z.reshape(ct * FFT_N2, z.shape[2]), wa_ref).reshape(ct, FFT_N2, 2 * FFT_N1)
    ar, ai = a[:, :, :FFT_N1], a[:, :, FFT_N1:]
    tr, ti = tw[:, :FFT_N1], tw[:, FFT_N1:]
    br = ar * tr - ai * ti
    bi = ar * ti + ai * tr
    bt = jnp.concatenate([jnp.swapaxes(br, 1, 2), jnp.swapaxes(bi, 1, 2)], axis=2)
    return _dot3(bt.reshape(ct * FFT_N1, 2 * FFT_N2), wb_ref).reshape(ct, FFT_N1, 2 * FFT_N2)


def _hy_kfft_kernel(k_ref, wa_ref, tw_ref, wb_ref, o_ref):
    o_ref[...] = _fft_fwd(k_ref[...], wa_ref, tw_ref[...], wb_ref)


def _hy_long_kernel(z_ref, g_ref, kh_ref, bias_ref, wa_ref, twf_ref, wb_ref, wc_ref, twi_ref, wd_ref, o_ref):
    z = z_ref[...]
    ct = z.shape[0]
    x = _fft_fwd(z, wa_ref, twf_ref[...], wb_ref)
    kh = kh_ref[...]
    xr, xi = x[:, :, :FFT_N2], x[:, :, FFT_N2:]
    kr, ki = kh[:, :, :FFT_N2], kh[:, :, FFT_N2:]
    y = jnp.concatenate([xr * kr - xi * ki, xr * ki + xi * kr], axis=2)
    c1 = _dot3(y.reshape(ct * FFT_N1, 2 * FFT_N2), wc_ref).reshape(ct, FFT_N1, 2 * FFT_N2)
    cr, ci = c1[:, :, :FFT_N2], c1[:, :, FFT_N2:]
    twi = twi_ref[...]
    tr, ti = twi[:, :FFT_N2], twi[:, FFT_N2:]
    dr = cr * tr - ci * ti
    di = cr * ti + ci * tr
    dt = jnp.concatenate([jnp.swapaxes(dr, 1, 2), jnp.swapaxes(di, 1, 2)], axis=2)
    conv = _dot3(dt.reshape(ct * FFT_N2, 2 * FFT_N1), wd_ref).reshape(ct, FFT_N2, FFT_N1)
    o_ref[...] = (conv + bias_ref[...] * z) * g_ref[...]


def _hy_ctx_kernel(x_ref, kc_ref, bias_ref, cf_ref, sf_ref, o_ref):
    x = x_ref[0]
    cf, sf = cf_ref[...], sf_ref[...]
    n = CTX_LEN
    z = x[:, :HY_CH]
    for od in range(HY_ORDER):
        k = kc_ref[:, od * HY_CH:(od + 1) * HY_CH]
        kr, ki = _dot(cf, k), -_dot(sf, k)
        xr, xi = _dot(cf[:, :n], z), -_dot(sf[:, :n], z)
        yr = xr * kr - xi * ki
        yi = xr * ki + xi * kr
        conv = (_dot(cf[:n, :], yr) - _dot(sf[:n, :], yi)) * (1.0 / (2 * n))
        z = x[:, (od + 1) * HY_CH:(od + 2) * HY_CH] * (conv + bias_ref[od:od + 1, :] * z)
    o_ref[0] = z


@functools.lru_cache(maxsize=None)
def _ctx_dft():
    m = np.arange(2 * CTX_LEN)
    th = 2.0 * math.pi * m[:, None] * m[None, :] / (2 * CTX_LEN)
    return np.cos(th).astype(np.float32), np.sin(th).astype(np.float32)


def _hyena(p_hy, conv_w, conv_b, w1, b1, w2, b2, w3, freq, bias):
    wa_c, wa_r, twf, twi, wb, wc, wd = (jnp.asarray(a) for a in _fft_tables())
    wa_c, wa_r, wb, wc, wd = (jnp.stack(_split(a)) for a in (wa_c, wa_r, wb, wc, wd))
    nch = HY_ORDER * HY_CH
    kf = _hy_filter(SEQ, w1, b1, w2, b2, w3, freq)
    kt = kf.reshape(FFT_N1, FFT_N2, nch).transpose(2, 1, 0)
    const = lambda shape: pl.BlockSpec(shape, lambda j: (0,) * len(shape))
    khat = pl.pallas_call(
        _hy_kfft_kernel,
        grid=(nch // FFT_CT,),
        in_specs=[pl.BlockSpec((FFT_CT, FFT_N2, FFT_N1), lambda j: (j, 0, 0)),
                  const((2, FFT_N1, 2 * FFT_N1)), const((FFT_N2, 2 * FFT_N1)), const((2, 2 * FFT_N2, 2 * FFT_N2))],
        out_specs=pl.BlockSpec((FFT_CT, FFT_N1, 2 * FFT_N2), lambda j: (j, 0, 0)),
        out_shape=jax.ShapeDtypeStruct((nch, FFT_N1, 2 * FFT_N2), F32),
        compiler_params=_cparams(("parallel",)),
        name="hyena_filter_fft",
    )(kt, wa_r, twf, wb)

    pc = _hy_short(p_hy, conv_w, conv_b, SEQ, 0)
    half = FFT_N1 // 2
    zall = pc.reshape(BATCH, half, FFT_N2, HY_IN).transpose(3, 2, 0, 1).reshape(HY_IN, FFT_N2, FFT_N1)
    nblk = HY_CH // FFT_CT
    z = zall
    for od in range(HY_ORDER):
        z_spec = pl.BlockSpec((FFT_CT, FFT_N2, FFT_N1), lambda j: (j, 0, 0))
        g_spec = pl.BlockSpec((FFT_CT, FFT_N2, FFT_N1), lambda j, od=od: ((od + 1) * nblk + j, 0, 0))
        z = pl.pallas_call(
            _hy_long_kernel,
            grid=(nblk,),
            in_specs=[z_spec, g_spec,
                      pl.BlockSpec((FFT_CT, FFT_N1, 2 * FFT_N2), lambda j, od=od: (od * nblk + j, 0, 0)),
                      pl.BlockSpec((FFT_CT, 1, 1), lambda j, od=od: (od * nblk + j, 0, 0)),
                      const((2, FFT_N1, 2 * FFT_N1)), const((FFT_N2, 2 * FFT_N1)), const((2, 2 * FFT_N2, 2 * FFT_N2)),
                      const((2, 2 * FFT_N2, 2 * FFT_N2)), const((FFT_N1, 2 * FFT_N2)), const((2, 2 * FFT_N1, FFT_N1))],
            out_specs=pl.BlockSpec((FFT_CT, FFT_N2, FFT_N1), lambda j: (j, 0, 0)),
            out_shape=jax.ShapeDtypeStruct((HY_CH, FFT_N2, FFT_N1), F32),
            compiler_params=_cparams(("parallel",)),
            name="hyena_long_conv",
        )(z, zall, khat, bias.reshape(nch, 1, 1), wa_c, twf, wb, wc, twi, wd)
    o_lat = z.reshape(HY_CH, FFT_N2, BATCH, half).transpose(2, 3, 1, 0).reshape(BATCH, SEQ, HY_CH)

    kc = _hy_filter(CTX_LEN, w1, b1, w2, b2, w3, freq)
    pcc = _hy_short(p_hy, conv_w, conv_b, CTX_LEN, SEQ // CTX_LEN)
    cf, sf = (jnp.asarray(a) for a in _ctx_dft())
    o_ctx = pl.pallas_call(
        _hy_ctx_kernel,
        grid=(BATCH,),
        in_specs=[pl.BlockSpec((1, CTX_LEN, HY_IN), lambda b: (b, 0, 0)),
                  pl.BlockSpec((2 * CTX_LEN, nch), lambda b: (0, 0)),
                  pl.BlockSpec((HY_ORDER, HY_CH), lambda b: (0, 0)),
                  pl.BlockSpec((2 * CTX_LEN, 2 * CTX_LEN), lambda b: (0, 0)),
                  pl.BlockSpec((2 * CTX_LEN, 2 * CTX_LEN), lambda b: (0, 0))],
        out_specs=pl.BlockSpec((1, CTX_LEN, HY_CH), lambda b: (b, 0, 0)),
        out_shape=jax.ShapeDtypeStruct((BATCH, CTX_LEN, HY_CH), F32),
        compiler_params=_cparams(("parallel",)),
        name="hyena_context",
    )(pcc, kc, bias, cf, sf)
    return jnp.concatenate([o_lat, o_ctx], axis=1)


@functools.lru_cache(maxsize=None)
def _rope_tables():
    nf = RW_HD // 4
    t = np.arange(SEQ)
    row = (t // GRID_W).astype(np.float64)
    col = (t % GRID_W).astype(np.float64)
    inv = ROPE_BASE ** (-np.arange(nf, dtype=np.float64) / nf)
    ang = np.concatenate([row[:, None] * inv] * 2 + [col[:, None] * inv] * 2, axis=-1)
    cos = np.concatenate([np.cos(ang), np.ones((CTX_LEN, RW_HD))], axis=0)
    sin = np.concatenate([np.sin(ang), np.zeros((CTX_LEN, RW_HD))], axis=0)
    return (np.tile(cos, (1, RW_HEADS)).astype(np.float32), np.tile(sin, (1, RW_HEADS)).astype(np.float32))


@functools.lru_cache(maxsize=None)
def _head_blockdiag():
    h = np.arange(RW_W) // RW_HD
    return (h[:, None] == h[None, :]).astype(np.float32)


@functools.lru_cache(maxsize=None)
def _chunk_tri():
    t = np.arange(RW_TR)[:, None]
    s = np.arange(RW_TR)[None, :]
    same = (t // RW_C) == (s // RW_C)
    return np.stack([same & (s <= t), same & (s >= t)]).astype(np.float32)


def _rope(x, cos, sin):
    lane = lax.broadcasted_iota(jnp.int32, (1, RW_W), 1)
    first = (lane % (RW_HD // 2)) < (RW_HD // 4)
    rot = jnp.where(first, -pltpu.roll(x, RW_W - RW_HD // 4, 1), pltpu.roll(x, RW_HD // 4, 1))
    return x * cos + rot * sin


def _rw_prep_kernel(p_ref, pv_ref, nx_ref, mu_ref, w0_ref, w2_ref, a0_ref, a2_ref, g2_ref, kk_ref, ka_ref, rk_ref,
                    cos_ref, sin_ref, bd_ref, tri_ref,
                    v_ref, bon_ref, g_ref, at_ref, rt_ref, bt_ref, kt_ref, wc_ref):
    i = pl.program_id(1)
    x = p_ref[0]
    row = lax.broadcasted_iota(jnp.int32, (RW_TR, 1), 0)
    t = i * RW_TR + row
    seq_first = (t == 0) | (t == SEQ)
    seq_last = (t == SEQ - 1) | (t == T_ALL - 1)
    prev = jnp.where(row == 0, pv_ref[0, 7:8, :], pltpu.roll(x, 1, 0))
    prev = jnp.where(seq_first, 0.0, prev)
    nxt = jnp.where(row == RW_TR - 1, nx_ref[0, 0:1, :], pltpu.roll(x, RW_TR - 1, 0))
    nxt = jnp.where(seq_last, 0.0, nxt)
    xs = x + mu_ref[0:1, :] * (prev - x) + mu_ref[1:2, :] * (nxt - x)
    r = xs[:, 0:RW_W]
    k = xs[:, RW_W:2 * RW_W]
    v = xs[:, 2 * RW_W:3 * RW_W]
    i1 = 3 * RW_W
    wd = jnp.tanh(xs[:, i1:i1 + 2 * RW_LORA])
    ad = xs[:, i1 + 2 * RW_LORA:i1 + 4 * RW_LORA]
    gd = xs[:, i1 + 4 * RW_LORA:]
    cos, sin, bd = cos_ref[...], sin_ref[...], bd_ref[...]
    g_ref[0] = _dot3(jax.nn.sigmoid(gd), g2_ref)
    kk = k * kk_ref[...]
    nrm = jnp.sqrt(_dot01(kk * kk, bd))
    kk = kk / jnp.maximum(nrm, 1e-12)
    ksum = jnp.zeros_like(k)
    r_s = _rope(r, cos, sin)
    na_s = -_rope(kk, cos, sin)
    n_chunks = RW_TR // RW_C
    for d in range(2):
        logit = w0_ref[d:d + 1, :] + _dot3(wd, w2_ref, (d,))
        logw = -jax.nn.softplus(-logit) - 0.5
        lw = -jnp.exp(logw)
        a = jax.nn.sigmoid(a0_ref[d:d + 1, :] + _dot3(ad, a2_ref, (d,)))
        kdir = k * (1.0 + (a - 1.0) * ka_ref[...])
        ksum = ksum + kdir
        cum = _dot01(lw, tri_ref[d], const_left=True)
        e_neg = jnp.exp(-cum)
        at_ref[d, 0] = na_s * jnp.exp(cum - lw)
        rt_ref[d, 0] = r_s * jnp.exp(cum)
        bt_ref[d, 0] = _rope(kk * a, cos, sin) * e_neg
        kt_ref[d, 0] = _rope(kdir, cos, sin) * e_neg
        tot = jnp.sum(lw.reshape(n_chunks, RW_C, RW_W), axis=1, keepdims=True)
        wc_ref[d, 0] = jnp.broadcast_to(jnp.exp(tot), (n_chunks, 8, RW_W))
    bon_ref[0] = _dot01(r * ksum * rk_ref[...], bd)
    v_ref[0] = v


def _bmm3(spec, a, b):
    (ah, am), (bh, bm) = a, b
    e = functools.partial(jnp.einsum, spec, preferred_element_type=F32)
    return e(ah, bh) + e(am, bh) + e(ah, bm)


def _rw_scan_kernel(vf_ref, vb_ref, atf_ref, atb_ref, rtf_ref, rtb_ref, btf_ref, btb_ref, ktf_ref, ktb_ref,
                    wcf_ref, wcb_ref, yf_ref, yb_ref, s_sc):
    step = pl.program_id(0)
    nb = BATCH * RW_HEADS
    c = RW_C

    @pl.when(step == 0)
    def _():
        s_sc[...] = jnp.zeros_like(s_sc)

    def heads(xf, xb):
        return jnp.stack([x[b][:, h * RW_HD:(h + 1) * RW_HD]
                          for x in (xf, xb) for b in range(BATCH) for h in range(RW_HEADS)])

    v = heads(vf_ref[...], vb_ref[...])
    ar = jnp.concatenate([heads(atf_ref[0], atb_ref[0]), heads(rtf_ref[0], rtb_ref[0])], axis=1)
    bt = heads(btf_ref[0], btb_ref[0])
    kt = heads(ktf_ref[0], ktb_ref[0])
    wc = heads(wcf_ref[0, :, 0], wcb_ref[0, :, 0])[:, 0:1, :]

    ti = lax.broadcasted_iota(jnp.int32, (c, c), 0)
    si = lax.broadcasted_iota(jnp.int32, (c, c), 1)
    sign = jnp.where(lax.broadcasted_iota(jnp.int32, (2 * nb, 1, 1), 0) < nb, 1, -1)
    ahead = (ti - si)[None] * sign
    strict = ahead > 0
    incl = ahead >= 0
    eye = (si == ti).astype(F32)[None]

    ar_s, bt_s, kt_s, v_s = _split(ar), _split(bt), _split(kt), _split(v)
    lb = _bmm3('htj,hsj->hts', ar_s, bt_s)
    lk = _bmm3('htj,hsj->hts', ar_s, kt_s)
    lab = jnp.where(strict, lb[:, :c], 0.0)
    lrb = jnp.where(incl, lb[:, c:], 0.0)
    lak = jnp.where(strict, lk[:, :c], 0.0)
    lrk = jnp.where(incl, lk[:, c:], 0.0)

    tt = eye + lab
    pw = lab
    for _ in range(int(math.log2(c)) - 1):
        pw_s = _split(pw)
        pw = _bmm3('hts,hsu->htu', pw_s, pw_s)
        tt = tt + _bmm3('hts,hsu->htu', _split(tt), _split(pw))

    s0 = s_sc[...]
    ars = _bmm3('htk,hvk->htv', ar_s, _split(s0))
    lv = _bmm3('hts,hsv->htv', _split(jnp.concatenate([lak, lrk], axis=1)), v_s)
    u = _bmm3('hts,hsv->htv', _split(tt), _split(lv[:, :c] + ars[:, :c]))
    y = ars[:, c:] + _bmm3('hts,hsv->htv', _split(lrb), _split(u)) + lv[:, c:]
    for d, y_ref in enumerate((yf_ref, yb_ref)):
        for b in range(BATCH):
            y_ref[b] = jnp.concatenate([y[(d * BATCH + b) * RW_HEADS + h] for h in range(RW_HEADS)], axis=-1)
    uv = jnp.concatenate([u, v], axis=1)
    bk = jnp.concatenate([bt, kt], axis=1)
    s_sc[...] = (s0 + _bmm3('htv,htj->hvj', _split(uv), _split(bk))) * wc


def _rw_out_kernel(yf_ref, yb_ref, bon_ref, v_ref, g_ref, lnw_ref, lnb_ref, bd_ref, o_ref):
    y = yf_ref[0] + yb_ref[0]
    bd = bd_ref[...]
    mean = _dot01(y, bd) * (1.0 / RW_HD)
    yc = y - mean
    var = _dot01(yc * yc, bd) * (1.0 / RW_HD)
    yn = yc * lax.rsqrt(var + RW_GN_EPS) * lnw_ref[...] + lnb_ref[...]
    o_ref[0] = (yn + bon_ref[0] * v_ref[0]) * g_ref[0]


def _rw_chunk_index(d, s):
    n_lat = SEQ // RW_C
    n_all = T_ALL // RW_C
    fwd = jnp.where(s < n_all - n_lat, n_lat + s, s - (n_all - n_lat))
    return jnp.where(d == 0, fwd, n_all - 1 - s)


def _rwkv(p_rw, mu, w0, w2, a0, a2, g2, k_k, k_a, r_k, ln_w, ln_b):
    cos, sin = (jnp.asarray(a) for a in _rope_tables())
    bd = jnp.asarray(_head_blockdiag()).astype(BF16)
    zpad = lambda w: jnp.stack([jnp.concatenate([w[0], jnp.zeros_like(w[0])], axis=0),
                                jnp.concatenate([jnp.zeros_like(w[1]), w[1]], axis=0)])
    split = lambda w: jnp.stack(_split(w), axis=1)
    nt = T_ALL // RW_TR
    n_ch = T_ALL // RW_C
    tok_shape = jax.ShapeDtypeStruct((BATCH, T_ALL, RW_W), F32)
    dir_shape = jax.ShapeDtypeStruct((2, BATCH, T_ALL, RW_W), F32)
    dir_spec = pl.BlockSpec((2, 1, RW_TR, RW_W), lambda b, i: (0, b, i, 0))
    wc_shape = jax.ShapeDtypeStruct((2, BATCH, n_ch, 8, RW_W), F32)
    wc_spec = pl.BlockSpec((2, 1, RW_TR // RW_C, 8, RW_W), lambda b, i: (0, b, i, 0, 0))
    tok = lambda w: pl.BlockSpec((1, RW_TR, w), lambda b, i: (b, i, 0))
    const = lambda shape: pl.BlockSpec(shape, lambda b, i: (0,) * len(shape))
    row = lambda a: a.reshape(1, RW_W)
    outs = pl.pallas_call(
        _rw_prep_kernel,
        grid=(BATCH, nt),
        in_specs=[tok(RW_IN),
                  pl.BlockSpec((1, 8, RW_IN), lambda b, i: (b, jnp.maximum(i * (RW_TR // 8) - 1, 0), 0)),
                  pl.BlockSpec((1, 8, RW_IN), lambda b, i: (b, jnp.minimum((i + 1) * (RW_TR // 8), T_ALL // 8 - 1), 0)),
                  const((2, RW_IN)), const((2, RW_W)), const((2, 2, 2 * RW_LORA, RW_W)), const((2, RW_W)),
                  const((2, 2, 2 * RW_LORA, RW_W)), const((2, RW_GATE_LORA, RW_W)),
                  const((1, RW_W)), const((1, RW_W)), const((1, RW_W)),
                  pl.BlockSpec((RW_TR, RW_W), lambda b, i: (i, 0)), pl.BlockSpec((RW_TR, RW_W), lambda b, i: (i, 0)),
                  const((RW_W, RW_W)), const((2, RW_TR, RW_TR))],
        out_specs=[tok(RW_W)] * 3 + [dir_spec] * 4 + [wc_spec],
        out_shape=[tok_shape] * 3 + [dir_shape] * 4 + [wc_shape],
        compiler_params=_cparams(("parallel", "parallel")),
        name="rwkv_prepare",
    )(p_rw, p_rw, p_rw, mu, w0, split(zpad(w2)), a0, split(zpad(a2)), jnp.stack(_split(g2)),
      row(k_k), row(k_a), row(r_k.reshape(-1)), cos, sin, bd, jnp.asarray(_chunk_tri()).astype(BF16))
    v, bonus, g, at, rt, bt, kt, wc = outs

    shared = lambda d: pl.BlockSpec((BATCH, RW_C, RW_W), lambda s: (0, _rw_chunk_index(d, s), 0))
    perdir = lambda d: pl.BlockSpec((1, BATCH, RW_C, RW_W), lambda s: (d, 0, _rw_chunk_index(d, s), 0))
    wcdir = lambda d: pl.BlockSpec((1, BATCH, 1, 8, RW_W), lambda s: (d, 0, _rw_chunk_index(d, s), 0, 0))
    yf, yb = pl.pallas_call(
        _rw_scan_kernel,
        grid=(n_ch,),
        in_specs=[shared(0), shared(1)] + [perdir(0), perdir(1)] * 4 + [wcdir(0), wcdir(1)],
        out_specs=[shared(0), shared(1)],
        out_shape=[tok_shape, tok_shape],
        scratch_shapes=[pltpu.VMEM((2 * BATCH * RW_HEADS, RW_HD, RW_HD), F32)],
        compiler_params=_cparams(("arbitrary",)),
        name="rwkv_chunk_scan",
    )(v, v, at, at, rt, rt, bt, bt, kt, kt, wc, wc)

    tok2 = lambda w: pl.BlockSpec((1, PROJ_TM, w), lambda b, i: (b, i, 0))
    return pl.pallas_call(
        _rw_out_kernel,
        grid=(BATCH, T_ALL // PROJ_TM),
        in_specs=[tok2(RW_W), tok2(RW_W), tok2(RW_W), tok2(RW_W), tok2(RW_W),
                  const((1, RW_W)), const((1, RW_W)), const((RW_W, RW_W))],
        out_specs=tok2(RW_W),
        out_shape=jax.ShapeDtypeStruct((BATCH, T_ALL, RW_W), F32),
        compiler_params=_cparams(("parallel", "parallel")),
        name="rwkv_output",
    )(yf, yb, bonus, v, g, row(ln_w), row(ln_b), bd)


@functools.lru_cache(maxsize=None)
def _na_bias_tables():
    qc = np.arange(GRID_W)[:, None]
    kc = np.arange(GRID_W)[None, :]
    win0 = np.clip(qc - WIN_COLS // 2, 0, GRID_W - WIN_COLS)
    valid = (kc >= win0) & (kc < win0 + WIN_COLS)
    off = np.clip(kc - qc + WIN_COLS - 1, 0, 2 * WIN_COLS - 2)
    onehot = np.zeros((128, GRID_W * GRID_W), np.float32)
    onehot[off.reshape(-1), np.arange(GRID_W * GRID_W)] = valid.reshape(-1)
    mask = np.where(valid, 0.0, NEG_INF).astype(np.float32).reshape(1, GRID_W * GRID_W)
    return onehot, mask


def _na_bias_kernel(rpb_ref, oh_ref, mask_ref, o_ref):
    o_ref[...] = _dot(rpb_ref[...], oh_ref[...]) + mask_ref[...]


def _na_lat_kernel(q_ref, k_ref, v_ref, tp_ref, o_ref):
    i = pl.program_id(1)
    start = jnp.clip(i - WIN_ROWS // 2, 0, N_ROWS - WIN_ROWS)
    dr0 = start - i + WIN_ROWS - 1
    scale = NA_HD ** -0.5
    lane = lax.broadcasted_iota(jnp.int32, (1, NA_W), 1)
    hmask = [(lane // NA_HD) == h for h in range(NA_HEADS)]
    q = q_ref[0]
    qm = jnp.concatenate([jnp.where(hmask[h], q, jnp.zeros_like(q)) for h in range(NA_HEADS)], axis=0)
    row0 = pl.multiple_of(start * GRID_W, GRID_W)
    kl = k_ref[0, pl.ds(row0, WIN_ROWS * GRID_W), :]
    vl = v_ref[0, pl.ds(row0, WIN_ROWS * GRID_W), :]
    kc = k_ref[0, SEQ:T_ALL, :]
    vc = v_ref[0, SEQ:T_ALL, :]
    nt = (((1,), (1,)), ((), ()))
    bias = jnp.concatenate(
        [jnp.concatenate([tp_ref[h, pl.ds(dr0 + 2 * j, 1)][0] for j in range(WIN_ROWS // 2)], axis=1)
         for h in range(NA_HEADS)], axis=0)
    s_loc = lax.dot_general(qm, kl, nt, preferred_element_type=F32) * scale + bias
    s_ctx = lax.dot_general(qm, kc, nt, preferred_element_type=F32) * scale
    m = jnp.maximum(jnp.max(s_loc, axis=-1, keepdims=True), jnp.max(s_ctx, axis=-1, keepdims=True))
    p_loc = jnp.exp(s_loc - m)
    p_ctx = jnp.exp(s_ctx - m)
    inv = 1.0 / (jnp.sum(p_loc, axis=-1, keepdims=True) + jnp.sum(p_ctx, axis=-1, keepdims=True))
    o_all = (jnp.dot((p_loc * inv).astype(BF16), vl, preferred_element_type=F32)
             + jnp.dot((p_ctx * inv).astype(BF16), vc, preferred_element_type=F32))
    o = jnp.zeros((GRID_W, NA_W), F32)
    for h in range(NA_HEADS):
        o = o + jnp.where(hmask[h], o_all[h * GRID_W:(h + 1) * GRID_W, :], 0.0)
    o_ref[0] = o


def _na_ctx_kernel(q_ref, k_ref, v_ref, o_ref):
    scale = NA_HD ** -0.5
    lane = lax.broadcasted_iota(jnp.int32, (1, NA_W), 1)
    hmask = [(lane // NA_HD) == h for h in range(NA_HEADS)]
    q, k, v = q_ref[0], k_ref[0], v_ref[0]
    qm = jnp.concatenate([jnp.where(hmask[h], q, jnp.zeros_like(q)) for h in range(NA_HEADS)], axis=0)
    s = lax.dot_general(qm, k, (((1,), (1,)), ((), ())), preferred_element_type=F32) * scale
    p = jnp.exp(s - jnp.max(s, axis=-1, keepdims=True))
    p = p / jnp.sum(p, axis=-1, keepdims=True)
    o_all = jnp.dot(p.astype(BF16), v, preferred_element_type=F32)
    o = jnp.zeros((CTX_LEN, NA_W), F32)
    for h in range(NA_HEADS):
        o = o + jnp.where(hmask[h], o_all[h * CTX_LEN:(h + 1) * CTX_LEN, :], 0.0)
    o_ref[0] = o


def _natten(p_na, rpb):
    onehot, mask = (jnp.asarray(a) for a in _na_bias_tables())
    n_dr = 2 * WIN_ROWS - 1
    n_rel = 2 * WIN_COLS - 1
    rows = NA_HEADS * n_dr
    rows_p = -(-rows // 8) * 8
    rpb2 = jnp.zeros((rows_p, 128), F32).at[:rows, :n_rel].set(rpb.reshape(rows, n_rel))
    tb = pl.pallas_call(
        _na_bias_kernel,
        out_shape=jax.ShapeDtypeStruct((rows_p, GRID_W * GRID_W), F32),
        name="natten_bias_expand",
    )(rpb2, onehot, mask)
    tb = tb[:rows].reshape(NA_HEADS, n_dr, GRID_W, GRID_W)
    nxt = jnp.concatenate([tb[:, 1:], jnp.full_like(tb[:, :1], NEG_INF)], axis=1)
    tp = jnp.concatenate([tb, nxt], axis=-1)

    o_lat = pl.pallas_call(
        _na_lat_kernel,
        grid=(BATCH, N_ROWS),
        in_specs=[pl.BlockSpec((1, GRID_W, NA_W), lambda b, i: (b, i, 0)),
                  pl.BlockSpec((1, T_ALL, NA_W), lambda b, i: (b, 0, 1)),
                  pl.BlockSpec((1, T_ALL, NA_W), lambda b, i: (b, 0, 2)),
                  pl.BlockSpec((NA_HEADS, n_dr, GRID_W, 2 * GRID_W), lambda b, i: (0, 0, 0, 0))],
        out_specs=pl.BlockSpec((1, GRID_W, NA_W), lambda b, i: (b, i, 0)),
        out_shape=jax.ShapeDtypeStruct((BATCH, SEQ, NA_W), F32),
        compiler_params=_cparams(("parallel", "parallel")),
        name="natten_latent",
    )(p_na, p_na, p_na, tp)

    cblk = SEQ // CTX_LEN
    o_ctx = pl.pallas_call(
        _na_ctx_kernel,
        grid=(BATCH,),
        in_specs=[pl.BlockSpec((1, CTX_LEN, NA_W), lambda b: (b, cblk, 0)),
                  pl.BlockSpec((1, CTX_LEN, NA_W), lambda b: (b, cblk, 1)),
                  pl.BlockSpec((1, CTX_LEN, NA_W), lambda b: (b, cblk, 2))],
        out_specs=pl.BlockSpec((1, CTX_LEN, NA_W), lambda b: (b, 0, 0)),
        out_shape=jax.ShapeDtypeStruct((BATCH, CTX_LEN, NA_W), F32),
        compiler_params=_cparams(("parallel",)),
        name="natten_context",
    )(p_na, p_na, p_na)
    return jnp.concatenate([o_lat, o_ctx], axis=1)


def kernel(x, c, ctx, c_ctx, mod_w, mod_b, norm_g, ffn1_wgu, ffn1_wdn, ffn2_wgu, ffn2_wdn, w_in, w_out, hy_conv_w, hy_conv_b, hy_f_w1, hy_f_b1, hy_f_w2, hy_f_b2, hy_f_w3, hy_freq, hy_bias, rw_mu, rw_w0, rw_w2, rw_a0, rw_a2, rw_g2, rw_k_k, rw_k_a, rw_r_k, rw_ln_w, rw_ln_b, na_rpb):
    assert x.shape == (BATCH, SEQ, D_MODEL) and ctx.shape == (BATCH, CTX_LEN, D_MODEL)
    mods = _modulation(c, c_ctx, mod_w, mod_b)
    h = jnp.concatenate([x, ctx], axis=1)
    gpad = jnp.concatenate([norm_g, jnp.zeros((DEPTH, 2, D_MODEL), F32)], axis=1)
    for l in range(DEPTH):
        m_l, g_l = mods[l], gpad[l]
        h = _ffn(h, m_l, g_l, ffn1_wgu[l].astype(BF16), ffn1_wdn[l].astype(BF16), 0)
        p_hy, p_rw, p_na = _in_proj(h, m_l, g_l, w_in[l].astype(BF16))
        o_hy = _hyena(p_hy, hy_conv_w[l], hy_conv_b[l], hy_f_w1[l], hy_f_b1[l], hy_f_w2[l], hy_f_b2[l],
                      hy_f_w3[l], hy_freq[l], hy_bias[l])
        o_rw = _rwkv(p_rw, rw_mu[l], rw_w0[l], rw_w2[l], rw_a0[l], rw_a2[l], rw_g2[l], rw_k_k[l], rw_k_a[l],
                     rw_r_k[l], rw_ln_w[l], rw_ln_b[l])
        o_na = _natten(p_na, na_rpb[l])
        h = _out_proj(h, o_hy, o_rw, o_na, m_l, g_l, w_out[l].astype(BF16))
        h = _ffn(h, m_l, g_l, ffn2_wgu[l].astype(BF16), ffn2_wdn[l].astype(BF16), 1)
    return h[:, :SEQ]
```

```python
import functools
import math

import numpy as np
import jax
import jax.numpy as jnp
from jax import lax
from jax.experimental import pallas as pl
from jax.experimental.pallas import tpu as pltpu

D_MODEL = 1024
BATCH = 2
SEQ = 8192
DEPTH = 4
GRID_W = 64
CTX_LEN = 256
T_ALL = SEQ + CTX_LEN
N_MOD = 9
D_FF = 2816
MACARON_W = 0.5
NORM_EPS = 1e-6
NEG_INF = -1e30

HY_CH = 256
HY_ORDER = 2
HY_EMB = 33
HY_FILT = 64
HY_TARGET = 1e-2
HY_FAST = 0.3
HY_SLOW = 1.5
HY_IN = (HY_ORDER + 1) * HY_CH

RW_HEADS = 6
RW_HD = 64
RW_W = RW_HEADS * RW_HD
RW_LORA = 64
RW_GATE_LORA = 128
RW_GN_EPS = 64e-5
ROPE_BASE = 10000.0
RW_IN = 3 * RW_W + 4 * RW_LORA + RW_GATE_LORA

NA_HEADS = 6
NA_HD = 64
NA_W = NA_HEADS * NA_HD
WIN_ROWS = 8
WIN_COLS = 16
NA_IN = 3 * NA_W
N_ROWS = SEQ // GRID_W

F32 = jnp.float32
BF16 = jnp.bfloat16
HI = lax.Precision.HIGHEST

VMEM_LIMIT = 56 * 1024 * 1024

FFN_TM = 768
FFN_TF = 1408
PROJ_TM = 384
RW_TR = 256
RW_C = 64
FFT_N1 = 128
FFT_N2 = 128
FFT_CT = 16
NA_G = 4


def _dot(a, b):
    return jnp.dot(a, b, precision=HI, preferred_element_type=F32)


def _split(x):
    hi = x.astype(BF16)
    return hi, (x - hi.astype(F32)).astype(BF16)


def _dot3(x, w_ref, idx=()):
    xh, xm = _split(x)
    wh, wm = w_ref[idx + (0,)], w_ref[idx + (1,)]
    d = functools.partial(jnp.dot, preferred_element_type=F32)
    return d(xh, wh) + d(xm, wh) + d(xh, wm)


def _split3(x):
    hi = x.astype(BF16)
    r1 = x - hi.astype(F32)
    mid = r1.astype(BF16)
    return hi, mid, (r1 - mid.astype(F32)).astype(BF16)


def _dot01(x, m, const_left=False):
    d = functools.partial(jnp.dot, preferred_element_type=F32)
    parts = _split3(x)
    if const_left:
        return d(m, parts[0]) + d(m, parts[1]) + d(m, parts[2])
    return d(parts[0], m) + d(parts[1], m) + d(parts[2], m)


def _cparams(sem, vmem=VMEM_LIMIT):
    return pltpu.CompilerParams(dimension_semantics=sem, vmem_limit_bytes=vmem)


def _mod_kernel(cv_ref, w_ref, b_ref, o_ref):
    cv = cv_ref[...]
    s = cv * jax.nn.sigmoid(cv)
    o_ref[0] = _dot(s, w_ref[0]) + b_ref[0]


def _modulation(c, c_ctx, mod_w, mod_b):
    cv = jnp.concatenate([c, c_ctx[None, :], jnp.zeros((8 - BATCH - 1, D_MODEL), F32)], axis=0)
    tn = 1536
    return pl.pallas_call(
        _mod_kernel,
        grid=(DEPTH, N_MOD * D_MODEL // tn),
        in_specs=[pl.BlockSpec((8, D_MODEL), lambda l, j: (0, 0)),
                  pl.BlockSpec((1, D_MODEL, tn), lambda l, j: (l, 0, j)),
                  pl.BlockSpec((1, 1, tn), lambda l, j: (l, 0, j))],
        out_specs=pl.BlockSpec((1, 8, tn), lambda l, j: (l, 0, j)),
        out_shape=jax.ShapeDtypeStruct((DEPTH, 8, N_MOD * D_MODEL), F32),
        compiler_params=_cparams(("parallel", "parallel")),
        name="adaln_modulation",
    )(cv, mod_w, mod_b.reshape(DEPTH, 1, N_MOD * D_MODEL))


def _mod_rows(m_ref, b, i, tm, idx):
    t = i * tm + lax.broadcasted_iota(jnp.int32, (tm, 1), 0)
    ml = m_ref[pl.ds(b, 1), idx * D_MODEL:(idx + 1) * D_MODEL]
    mc = m_ref[BATCH:BATCH + 1, idx * D_MODEL:(idx + 1) * D_MODEL]
    return jnp.where(t >= SEQ, mc, ml)


def _rms(x, g):
    return x * lax.rsqrt(jnp.mean(x * x, axis=-1, keepdims=True) + NORM_EPS) * g


def _ffn_kernel(h_ref, m_ref, g_ref, wg_ref, wu_ref, wd_ref, o_ref, u_sc, acc_sc, *, mi, gi):
    b, i, k = pl.program_id(0), pl.program_id(1), pl.program_id(2)

    @pl.when(k == 0)
    def _():
        y = _rms(h_ref[0], g_ref[gi:gi + 1, :])
        u = y * (1.0 + _mod_rows(m_ref, b, i, FFN_TM, mi + 1)) + _mod_rows(m_ref, b, i, FFN_TM, mi)
        u_sc[...] = u.astype(BF16)
        acc_sc[...] = jnp.zeros_like(acc_sc)

    u = u_sc[...]
    gate = jnp.dot(u, wg_ref[...], preferred_element_type=F32)
    up = jnp.dot(u, wu_ref[...], preferred_element_type=F32)
    a = (gate * jax.nn.sigmoid(gate) * up).astype(BF16)
    acc_sc[...] += jnp.dot(a, wd_ref[...], preferred_element_type=F32)

    @pl.when(k == pl.num_programs(2) - 1)
    def _():
        y = _rms(acc_sc[...], g_ref[gi + 1:gi + 2, :])
        o_ref[0] = h_ref[0] + MACARON_W * _mod_rows(m_ref, b, i, FFN_TM, mi + 2) * y


def _ffn(h, mods_l, g_l, wgu, wdn, which):
    mi, gi = (0, 0) if which == 0 else (6, 4)
    nk = D_FF // FFN_TF
    return pl.pallas_call(
        functools.partial(_ffn_kernel, mi=mi, gi=gi),
        grid=(BATCH, T_ALL // FFN_TM, nk),
        in_specs=[pl.BlockSpec((1, FFN_TM, D_MODEL), lambda b, i, k: (b, i, 0)),
                  pl.BlockSpec((8, N_MOD * D_MODEL), lambda b, i, k: (0, 0)),
                  pl.BlockSpec((8, D_MODEL), lambda b, i, k: (0, 0)),
                  pl.BlockSpec((D_MODEL, FFN_TF), lambda b, i, k: (0, k)),
                  pl.BlockSpec((D_MODEL, FFN_TF), lambda b, i, k: (0, k + D_FF // FFN_TF)),
                  pl.BlockSpec((FFN_TF, D_MODEL), lambda b, i, k: (k, 0))],
        out_specs=pl.BlockSpec((1, FFN_TM, D_MODEL), lambda b, i, k: (b, i, 0)),
        out_shape=jax.ShapeDtypeStruct((BATCH, T_ALL, D_MODEL), F32),
        scratch_shapes=[pltpu.VMEM((FFN_TM, D_MODEL), BF16), pltpu.VMEM((FFN_TM, D_MODEL), F32)],
        compiler_params=_cparams(("parallel", "parallel", "arbitrary")),
        name="ffn_sublayer",
    )(h, mods_l, g_l, wgu, wgu, wdn)


def _inproj_kernel(h_ref, m_ref, g_ref, w_ref, hy_ref, rw_ref, na_ref):
    b, i = pl.program_id(0), pl.program_id(1)
    y = _rms(h_ref[0], g_ref[2:3, :])
    u = y * (1.0 + _mod_rows(m_ref, b, i, PROJ_TM, 4)) + _mod_rows(m_ref, b, i, PROJ_TM, 3)
    p = jnp.dot(u.astype(BF16), w_ref[...], preferred_element_type=F32)
    hy_ref[0] = p[:, :HY_IN]
    rw_ref[0] = p[:, HY_IN:HY_IN + RW_IN]
    na_ref[0] = p[:, HY_IN + RW_IN:].astype(BF16)


def _in_proj(h, mods_l, g_l, w_in):
    n_in = HY_IN + RW_IN + NA_IN
    return pl.pallas_call(
        _inproj_kernel,
        grid=(BATCH, T_ALL // PROJ_TM),
        in_specs=[pl.BlockSpec((1, PROJ_TM, D_MODEL), lambda b, i: (b, i, 0)),
                  pl.BlockSpec((8, N_MOD * D_MODEL), lambda b, i: (0, 0)),
                  pl.BlockSpec((8, D_MODEL), lambda b, i: (0, 0)),
                  pl.BlockSpec((D_MODEL, n_in), lambda b, i: (0, 0))],
        out_specs=[pl.BlockSpec((1, PROJ_TM, HY_IN), lambda b, i: (b, i, 0)),
                   pl.BlockSpec((1, PROJ_TM, RW_IN), lambda b, i: (b, i, 0)),
                   pl.BlockSpec((1, PROJ_TM, NA_IN), lambda b, i: (b, i, 0))],
        out_shape=[jax.ShapeDtypeStruct((BATCH, T_ALL, HY_IN), F32),
                   jax.ShapeDtypeStruct((BATCH, T_ALL, RW_IN), F32),
                   jax.ShapeDtypeStruct((BATCH, T_ALL, NA_IN), BF16)],
        compiler_params=_cparams(("parallel", "parallel")),
        name="mixer_in_proj",
    )(h, mods_l, g_l, w_in)


def _outproj_kernel(h_ref, hy_ref, rw_ref, na_ref, m_ref, g_ref, w_ref, o_ref):
    b, i = pl.program_id(0), pl.program_id(1)
    cat = jnp.concatenate([hy_ref[0], rw_ref[0], na_ref[0]], axis=-1).astype(BF16)
    o = jnp.dot(cat, w_ref[...], preferred_element_type=F32)
    o_ref[0] = h_ref[0] + _mod_rows(m_ref, b, i, PROJ_TM, 5) * _rms(o, g_ref[3:4, :])


def _out_proj(h, o_hy, o_rw, o_na, mods_l, g_l, w_out):
    return pl.pallas_call(
        _outproj_kernel,
        grid=(BATCH, T_ALL // PROJ_TM),
        in_specs=[pl.BlockSpec((1, PROJ_TM, D_MODEL), lambda b, i: (b, i, 0)),
                  pl.BlockSpec((1, PROJ_TM, HY_CH), lambda b, i: (b, i, 0)),
                  pl.BlockSpec((1, PROJ_TM, RW_W), lambda b, i: (b, i, 0)),
                  pl.BlockSpec((1, PROJ_TM, NA_W), lambda b, i: (b, i, 0)),
                  pl.BlockSpec((8, N_MOD * D_MODEL), lambda b, i: (0, 0)),
                  pl.BlockSpec((8, D_MODEL), lambda b, i: (0, 0)),
                  pl.BlockSpec((D_MODEL, D_MODEL), lambda b, i: (0, 0))],
        out_specs=pl.BlockSpec((1, PROJ_TM, D_MODEL), lambda b, i: (b, i, 0)),
        out_shape=jax.ShapeDtypeStruct((BATCH, T_ALL, D_MODEL), F32),
        compiler_params=_cparams(("parallel", "parallel")),
        name="mixer_out_proj",
    )(h, o_hy, o_rw, o_na, mods_l, g_l, w_out)


def _hy_short_kernel(x_ref, w_ref, b_ref, o_ref):
    x = x_ref[0]
    n = x.shape[0]
    row = lax.broadcasted_iota(jnp.int32, (n, 1), 0)
    prev = jnp.where(row == 0, 0.0, pltpu.roll(x, 1, 0))
    nxt = jnp.where(row == n - 1, 0.0, pltpu.roll(x, n - 1, 0))
    o_ref[0] = prev * w_ref[0:1, :] + x * w_ref[1:2, :] + nxt * w_ref[2:3, :] + b_ref[...]


def _hy_short(p_hy, conv_w, conv_b, seq_len, row_block):
    lanes = 128
    return pl.pallas_call(
        _hy_short_kernel,
        grid=(BATCH, HY_IN // lanes),
        in_specs=[pl.BlockSpec((1, seq_len, lanes), lambda b, j: (b, row_block, j)),
                  pl.BlockSpec((3, lanes), lambda b, j: (0, j)),
                  pl.BlockSpec((1, lanes), lambda b, j: (0, j))],
        out_specs=pl.BlockSpec((1, seq_len, lanes), lambda b, j: (b, 0, j)),
        out_shape=jax.ShapeDtypeStruct((BATCH, seq_len, HY_IN), F32),
        compiler_params=_cparams(("parallel", "parallel")),
        name="hyena_short_conv",
    )(p_hy, conv_w, conv_b.reshape(1, HY_IN))


@functools.lru_cache(maxsize=None)
def _filter_features(seq_len):
    n = 2 * seq_len
    m = np.arange(n)
    d = np.where(m < seq_len, m, n - m).astype(np.float64)
    t = d / (seq_len - 1)
    ang = 2.0 * math.pi * d / seq_len
    bands = (HY_EMB - 1) // 2
    fr = np.linspace(1e-4, bands - 1, bands)
    z = np.zeros((n, 128), np.float64)
    z[:, 0] = t
    z[:, 1:1 + bands] = np.cos(fr[None, :] * ang[:, None])
    z[:, 1 + bands:1 + 2 * bands] = -np.sin(fr[None, :] * ang[:, None])
    z[:, 33] = (m < seq_len)
    z[:, 34] = (m > seq_len)
    z[seq_len, :] = 0.0
    return z.astype(np.float32)


@functools.lru_cache(maxsize=None)
def _filter_deltas():
    d = np.abs(np.linspace(math.log(HY_TARGET) / HY_SLOW, math.log(HY_TARGET) / HY_FAST, HY_CH))
    return d.astype(np.float32).reshape(1, HY_CH)


def _hy_filter_kernel(z_ref, w1_ref, b1_ref, w2_ref, b2_ref, w3_ref, fq_ref, dl_ref, o_ref):
    z = z_ref[...]
    fq = fq_ref[...]
    h = jnp.sin(fq * (_dot(z, w1_ref[...]) + b1_ref[...]))
    h = jnp.sin(fq * (_dot(h, w2_ref[...]) + b2_ref[...]))
    o = _dot3(h, w3_ref)
    dec = jnp.exp(-z[:, 0:1] * dl_ref[...])
    mf = z[:, 33:34]
    mb = z[:, 34:35]
    outs = []
    for od in range(HY_ORDER):
        fwd = o[:, od * 2 * HY_CH:od * 2 * HY_CH + HY_CH]
        bwd = o[:, od * 2 * HY_CH + HY_CH:(od + 1) * 2 * HY_CH]
        outs.append((mf * fwd + mb * bwd) * dec)
    o_ref[...] = jnp.concatenate(outs, axis=1)


def _hy_filter(seq_len, w1, b1, w2, b2, w3, freq):
    n = 2 * seq_len
    tp = min(n, 1024)
    zt = jnp.asarray(_filter_features(seq_len))
    w1p = jnp.zeros((128, HY_FILT), F32).at[:HY_EMB].set(w1)
    full = lambda shape: pl.BlockSpec(shape, lambda i: (0,) * len(shape))
    return pl.pallas_call(
        _hy_filter_kernel,
        grid=(n // tp,),
        in_specs=[pl.BlockSpec((tp, 128), lambda i: (i, 0)),
                  full((128, HY_FILT)), full((1, HY_FILT)), full((HY_FILT, HY_FILT)), full((1, HY_FILT)),
                  full((2, HY_FILT, HY_ORDER * 2 * HY_CH)), full((1, HY_FILT)), full((1, HY_CH))],
        out_specs=pl.BlockSpec((tp, HY_ORDER * HY_CH), lambda i: (i, 0)),
        out_shape=jax.ShapeDtypeStruct((n, HY_ORDER * HY_CH), F32),
        compiler_params=_cparams(("parallel",)),
        name="hyena_filter_mlp",
    )(zt, w1p, b1.reshape(1, -1), w2, b2.reshape(1, -1), jnp.stack(_split(w3)), freq.reshape(1, -1),
      jnp.asarray(_filter_deltas()))


@functools.lru_cache(maxsize=None)
def _fft_tables():
    n1 = np.arange(FFT_N1)[:, None]
    k1 = np.arange(FFT_N1)[None, :]
    th = 2.0 * math.pi * n1 * k1 / FFT_N1
    c, s = np.cos(th), np.sin(th)
    half = FFT_N1 // 2
    big_n = FFT_N1 * FFT_N2
    wa_c = np.block([[c[:half], -s[:half]], [s[:half], c[:half]]])
    wa_r = np.concatenate([c, -s], axis=1)
    tw = 2.0 * math.pi * np.arange(FFT_N2)[:, None] * np.arange(FFT_N1)[None, :] / big_n
    twf = np.concatenate([np.cos(tw), -np.sin(tw)], axis=1)
    twi = np.concatenate([np.cos(tw.T), np.sin(tw.T)], axis=1)
    wb = np.block([[c, -s], [s, c]])
    wc = np.block([[c, s], [-s, c]])
    wd = np.block([[c[:, :half], s[:, :half]], [-s[:, :half], c[:, :half]]]) / big_n
    return tuple(np.asarray(a, np.float32) for a in (wa_c, wa_r, twf, twi, wb, wc, wd))


def _fft_fwd(z, wa_ref, tw, wb_ref):
    ct = z.shape[0]
    a = _dot3(z.reshape(ct * FFT_N2, z.shape[2]), wa_ref).reshape(ct, FFT_N2, 2 * FFT_N1)
    ar, ai = a[:, :, :FFT_N1], a[:, :, FFT_N1:]
    tr, ti = tw[:, :FFT_N1], tw[:, FFT_N1:]
    br = ar * tr - ai * ti
    bi = ar * ti + ai * tr
    bt = jnp.concatenate([jnp.swapaxes(br, 1, 2), jnp.swapaxes(bi, 1, 2)], axis=2)
    return _dot3(bt.reshape(ct * FFT_N1, 2 * FFT_N2), wb_ref).reshape(ct, FFT_N1, 2 * FFT_N2)


def _hy_kfft_kernel(k_ref, wa_ref, tw_ref, wb_ref, o_ref):
    o_ref[...] = _fft_fwd(k_ref[...], wa_ref, tw_ref[...], wb_ref)


def _hy_long_kernel(z_ref, g_ref, kh_ref, bias_ref, wa_ref, twf_ref, wb_ref, wc_ref, twi_ref, wd_ref, o_ref):
    z = z_ref[...]
    ct = z.shape[0]
    x = _fft_fwd(z, wa_ref, twf_ref[...], wb_ref)
    kh = kh_ref[...]
    xr, xi = x[:, :, :FFT_N2], x[:, :, FFT_N2:]
    kr, ki = kh[:, :, :FFT_N2], kh[:, :, FFT_N2:]
    y = jnp.concatenate([xr * kr - xi * ki, xr * ki + xi * kr], axis=2)
    c1 = _dot3(y.reshape(ct * FFT_N1, 2 * FFT_N2), wc_ref).reshape(ct, FFT_N1, 2 * FFT_N2)
    cr, ci = c1[:, :, :FFT_N2], c1[:, :, FFT_N2:]
    twi = twi_ref[...]
    tr, ti = twi[:, :FFT_N2], twi[:, FFT_N2:]
    dr = cr * tr - ci * ti
    di = cr * ti + ci * tr
    dt = jnp.concatenate([jnp.swapaxes(dr, 1, 2), jnp.swapaxes(di, 1, 2)], axis=2)
    conv = _dot3(dt.reshape(ct * FFT_N2, 2 * FFT_N1), wd_ref).reshape(ct, FFT_N2, FFT_N1)
    o_ref[...] = (conv + bias_ref[...] * z) * g_ref[...]


def _hy_ctx_kernel(x_ref, kc_ref, bias_ref, cf_ref, sf_ref, o_ref):
    x = x_ref[0]
    cf, sf = cf_ref[...], sf_ref[...]
    n = CTX_LEN
    z = x[:, :HY_CH]
    for od in range(HY_ORDER):
        k = kc_ref[:, od * HY_CH:(od + 1) * HY_CH]
        kr, ki = _dot(cf, k), -_dot(sf, k)
        xr, xi = _dot(cf[:, :n], z), -_dot(sf[:, :n], z)
        yr = xr * kr - xi * ki
        yi = xr * ki + xi * kr
        conv = (_dot(cf[:n, :], yr) - _dot(sf[:n, :], yi)) * (1.0 / (2 * n))
        z = x[:, (od + 1) * HY_CH:(od + 2) * HY_CH] * (conv + bias_ref[od:od + 1, :] * z)
    o_ref[0] = z


@functools.lru_cache(maxsize=None)
def _ctx_dft():
    m = np.arange(2 * CTX_LEN)
    th = 2.0 * math.pi * m[:, None] * m[None, :] / (2 * CTX_LEN)
    return np.cos(th).astype(np.float32), np.sin(th).astype(np.float32)


def _hyena(p_hy, conv_w, conv_b, w1, b1, w2, b2, w3, freq, bias):
    wa_c, wa_r, twf, twi, wb, wc, wd = (jnp.asarray(a) for a in _fft_tables())
    wa_c, wa_r, wb, wc, wd = (jnp.stack(_split(a)) for a in (wa_c, wa_r, wb, wc, wd))
    nch = HY_ORDER * HY_CH
    kf = _hy_filter(SEQ, w1, b1, w2, b2, w3, freq)
    kt = kf.reshape(FFT_N1, FFT_N2, nch).transpose(2, 1, 0)
    const = lambda shape: pl.BlockSpec(shape, lambda j: (0,) * len(shape))
    khat = pl.pallas_call(
        _hy_kfft_kernel,
        grid=(nch // FFT_CT,),
        in_specs=[pl.BlockSpec((FFT_CT, FFT_N2, FFT_N1), lambda j: (j, 0, 0)),
                  const((2, FFT_N1, 2 * FFT_N1)), const((FFT_N2, 2 * FFT_N1)), const((2, 2 * FFT_N2, 2 * FFT_N2))],
        out_specs=pl.BlockSpec((FFT_CT, FFT_N1, 2 * FFT_N2), lambda j: (j, 0, 0)),
        out_shape=jax.ShapeDtypeStruct((nch, FFT_N1, 2 * FFT_N2), F32),
        compiler_params=_cparams(("parallel",)),
        name="hyena_filter_fft",
    )(kt, wa_r, twf, wb)

    pc = _hy_short(p_hy, conv_w, conv_b, SEQ, 0)
    half = FFT_N1 // 2
    zall = pc.reshape(BATCH, half, FFT_N2, HY_IN).transpose(3, 2, 0, 1).reshape(HY_IN, FFT_N2, FFT_N1)
    nblk = HY_CH // FFT_CT
    z = zall
    for od in range(HY_ORDER):
        z_spec = pl.BlockSpec((FFT_CT, FFT_N2, FFT_N1), lambda j: (j, 0, 0))
        g_spec = pl.BlockSpec((FFT_CT, FFT_N2, FFT_N1), lambda j, od=od: ((od + 1) * nblk + j, 0, 0))
        z = pl.pallas_call(
            _hy_long_kernel,
            grid=(nblk,),
            in_specs=[z_spec, g_spec,
                      pl.BlockSpec((FFT_CT, FFT_N1, 2 * FFT_N2), lambda j, od=od: (od * nblk + j, 0, 0)),
                      pl.BlockSpec((FFT_CT, 1, 1), lambda j, od=od: (od * nblk + j, 0, 0)),
                      const((2, FFT_N1, 2 * FFT_N1)), const((FFT_N2, 2 * FFT_N1)), const((2, 2 * FFT_N2, 2 * FFT_N2)),
                      const((2, 2 * FFT_N2, 2 * FFT_N2)), const((FFT_N1, 2 * FFT_N2)), const((2, 2 * FFT_N1, FFT_N1))],
            out_specs=pl.BlockSpec((FFT_CT, FFT_N2, FFT_N1), lambda j: (j, 0, 0)),
            out_shape=jax.ShapeDtypeStruct((HY_CH, FFT_N2, FFT_N1), F32),
            compiler_params=_cparams(("parallel",)),
            name="hyena_long_conv",
        )(z, zall, khat, bias.reshape(nch, 1, 1), wa_c, twf, wb, wc, twi, wd)
    o_lat = z.reshape(HY_CH, FFT_N2, BATCH, half).transpose(2, 3, 1, 0).reshape(BATCH, SEQ, HY_CH)

    kc = _hy_filter(CTX_LEN, w1, b1, w2, b2, w3, freq)
    pcc = _hy_short(p_hy, conv_w, conv_b, CTX_LEN, SEQ // CTX_LEN)
    cf, sf = (jnp.asarray(a) for a in _ctx_dft())
    o_ctx = pl.pallas_call(
        _hy_ctx_kernel,
        grid=(BATCH,),
        in_specs=[pl.BlockSpec((1, CTX_LEN, HY_IN), lambda b: (b, 0, 0)),
                  pl.BlockSpec((2 * CTX_LEN, nch), lambda b: (0, 0)),
                  pl.BlockSpec((HY_ORDER, HY_CH), lambda b: (0, 0)),
                  pl.BlockSpec((2 * CTX_LEN, 2 * CTX_LEN), lambda b: (0, 0)),
                  pl.BlockSpec((2 * CTX_LEN, 2 * CTX_LEN), lambda b: (0, 0))],
        out_specs=pl.BlockSpec((1, CTX_LEN, HY_CH), lambda b: (b, 0, 0)),
        out_shape=jax.ShapeDtypeStruct((BATCH, CTX_LEN, HY_CH), F32),
        compiler_params=_cparams(("parallel",)),
        name="hyena_context",
    )(pcc, kc, bias, cf, sf)
    return jnp.concatenate([o_lat, o_ctx], axis=1)


@functools.lru_cache(maxsize=None)
def _rope_tables():
    nf = RW_HD // 4
    t = np.arange(SEQ)
    row = (t // GRID_W).astype(np.float64)
    col = (t % GRID_W).astype(np.float64)
    inv = ROPE_BASE ** (-np.arange(nf, dtype=np.float64) / nf)
    ang = np.concatenate([row[:, None] * inv] * 2 + [col[:, None] * inv] * 2, axis=-1)
    cos = np.concatenate([np.cos(ang), np.ones((CTX_LEN, RW_HD))], axis=0)
    sin = np.concatenate([np.sin(ang), np.zeros((CTX_LEN, RW_HD))], axis=0)
    return (np.tile(cos, (1, RW_HEADS)).astype(np.float32), np.tile(sin, (1, RW_HEADS)).astype(np.float32))


@functools.lru_cache(maxsize=None)
def _head_blockdiag():
    h = np.arange(RW_W) // RW_HD
    return (h[:, None] == h[None, :]).astype(np.float32)


@functools.lru_cache(maxsize=None)
def _chunk_tri():
    t = np.arange(RW_TR)[:, None]
    s = np.arange(RW_TR)[None, :]
    same = (t // RW_C) == (s // RW_C)
    return np.stack([same & (s <= t), same & (s >= t)]).astype(np.float32)


def _rope(x, cos, sin):
    lane = lax.broadcasted_iota(jnp.int32, (1, RW_W), 1)
    first = (lane % (RW_HD // 2)) < (RW_HD // 4)
    rot = jnp.where(first, -pltpu.roll(x, RW_W - RW_HD // 4, 1), pltpu.roll(x, RW_HD // 4, 1))
    return x * cos + rot * sin


def _rw_prep_kernel(p_ref, pv_ref, nx_ref, mu_ref, w0_ref, w2_ref, a0_ref, a2_ref, g2_ref, kk_ref, ka_ref, rk_ref,
                    cos_ref, sin_ref, bd_ref, tri_ref,
                    v_ref, bon_ref, g_ref, at_ref, rt_ref, bt_ref, kt_ref, wc_ref):
    i = pl.program_id(1)
    x = p_ref[0]
    row = lax.broadcasted_iota(jnp.int32, (RW_TR, 1), 0)
    t = i * RW_TR + row
    seq_first = (t == 0) | (t == SEQ)
    seq_last = (t == SEQ - 1) | (t == T_ALL - 1)
    prev = jnp.where(row == 0, pv_ref[0, 7:8, :], pltpu.roll(x, 1, 0))
    prev = jnp.where(seq_first, 0.0, prev)
    nxt = jnp.where(row == RW_TR - 1, nx_ref[0, 0:1, :], pltpu.roll(x, RW_TR - 1, 0))
    nxt = jnp.where(seq_last, 0.0, nxt)
    xs = x + mu_ref[0:1, :] * (prev - x) + mu_ref[1:2, :] * (nxt - x)
    r = xs[:, 0:RW_W]
    k = xs[:, RW_W:2 * RW_W]
    v = xs[:, 2 * RW_W:3 * RW_W]
    i1 = 3 * RW_W
    wd = jnp.tanh(xs[:, i1:i1 + 2 * RW_LORA])
    ad = xs[:, i1 + 2 * RW_LORA:i1 + 4 * RW_LORA]
    gd = xs[:, i1 + 4 * RW_LORA:]
    cos, sin, bd = cos_ref[...], sin_ref[...], bd_ref[...]
    g_ref[0] = _dot3(jax.nn.sigmoid(gd), g2_ref)
    kk = k * kk_ref[...]
    nrm = jnp.sqrt(_dot01(kk * kk, bd))
    kk = kk / jnp.maximum(nrm, 1e-12)
    ksum = jnp.zeros_like(k)
    r_s = _rope(r, cos, sin)
    na_s = -_rope(kk, cos, sin)
    n_chunks = RW_TR // RW_C
    for d in range(2):
        logit = w0_ref[d:d + 1, :] + _dot3(wd, w2_ref, (d,))
        logw = -jax.nn.softplus(-logit) - 0.5
        lw = -jnp.exp(logw)
        a = jax.nn.sigmoid(a0_ref[d:d + 1, :] + _dot3(ad, a2_ref, (d,)))
        kdir = k * (1.0 + (a - 1.0) * ka_ref[...])
        ksum = ksum + kdir
        cum = _dot01(lw, tri_ref[d], const_left=True)
        e_neg = jnp.exp(-cum)
        at_ref[d, 0] = na_s * jnp.exp(cum - lw)
        rt_ref[d, 0] = r_s * jnp.exp(cum)
        bt_ref[d, 0] = _rope(kk * a, cos, sin) * e_neg
        kt_ref[d, 0] = _rope(kdir, cos, sin) * e_neg
        tot = jnp.sum(lw.reshape(n_chunks, RW_C, RW_W), axis=1, keepdims=True)
        wc_ref[d, 0] = jnp.broadcast_to(jnp.exp(tot), (n_chunks, 8, RW_W))
    bon_ref[0] = _dot01(r * ksum * rk_ref[...], bd)
    v_ref[0] = v


def _bmm3(spec, a, b):
    (ah, am), (bh, bm) = a, b
    e = functools.partial(jnp.einsum, spec, preferred_element_type=F32)
    return e(ah, bh) + e(am, bh) + e(ah, bm)


def _rw_scan_kernel(vf_ref, vb_ref, atf_ref, atb_ref, rtf_ref, rtb_ref, btf_ref, btb_ref, ktf_ref, ktb_ref,
                    wcf_ref, wcb_ref, yf_ref, yb_ref, s_sc):
    step = pl.program_id(0)
    nb = BATCH * RW_HEADS
    c = RW_C

    @pl.when(step == 0)
    def _():
        s_sc[...] = jnp.zeros_like(s_sc)

    def heads(xf, xb):
        return jnp.stack([x[b][:, h * RW_HD:(h + 1) * RW_HD]
                          for x in (xf, xb) for b in range(BATCH) for h in range(RW_HEADS)])

    v = heads(vf_ref[...], vb_ref[...])
    ar = jnp.concatenate([heads(atf_ref[0], atb_ref[0]), heads(rtf_ref[0], rtb_ref[0])], axis=1)
    bt = heads(btf_ref[0], btb_ref[0])
    kt = heads(ktf_ref[0], ktb_ref[0])
    wc = heads(wcf_ref[0, :, 0], wcb_ref[0, :, 0])[:, 0:1, :]

    ti = lax.broadcasted_iota(jnp.int32, (c, c), 0)
    si = lax.broadcasted_iota(jnp.int32, (c, c), 1)
    sign = jnp.where(lax.broadcasted_iota(jnp.int32, (2 * nb, 1, 1), 0) < nb, 1, -1)
    ahead = (ti - si)[None] * sign
    strict = ahead > 0
    incl = ahead >= 0
    eye = (si == ti).astype(F32)[None]

    ar_s, bt_s, kt_s, v_s = _split(ar), _split(bt), _split(kt), _split(v)
    lb = _bmm3('htj,hsj->hts', ar_s, bt_s)
    lk = _bmm3('htj,hsj->hts', ar_s, kt_s)
    lab = jnp.where(strict, lb[:, :c], 0.0)
    lrb = jnp.where(incl, lb[:, c:], 0.0)
    lak = jnp.where(strict, lk[:, :c], 0.0)
    lrk = jnp.where(incl, lk[:, c:], 0.0)

    tt = eye + lab
    pw = lab
    for _ in range(int(math.log2(c)) - 1):
        pw_s = _split(pw)
        pw = _bmm3('hts,hsu->htu', pw_s, pw_s)
        tt = tt + _bmm3('hts,hsu->htu', _split(tt), _split(pw))

    s0 = s_sc[...]
    ars = _bmm3('htk,hvk->htv', ar_s, _split(s0))
    lv = _bmm3('hts,hsv->htv', _split(jnp.concatenate([lak, lrk], axis=1)), v_s)
    u = _bmm3('hts,hsv->htv', _split(tt), _split(lv[:, :c] + ars[:, :c]))
    y = ars[:, c:] + _bmm3('hts,hsv->htv', _split(lrb), _split(u)) + lv[:, c:]
    for d, y_ref in enumerate((yf_ref, yb_ref)):
        for b in range(BATCH):
            y_ref[b] = jnp.concatenate([y[(d * BATCH + b) * RW_HEADS + h] for h in range(RW_HEADS)], axis=-1)
    uv = jnp.concatenate([u, v], axis=1)
    bk = jnp.concatenate([bt, kt], axis=1)
    s_sc[...] = (s0 + _bmm3('htv,htj->hvj', _split(uv), _split(bk))) * wc


def _rw_out_kernel(yf_ref, yb_ref, bon_ref, v_ref, g_ref, lnw_ref, lnb_ref, bd_ref, o_ref):
    y = yf_ref[0] + yb_ref[0]
    bd = bd_ref[...]
    mean = _dot01(y, bd) * (1.0 / RW_HD)
    yc = y - mean
    var = _dot01(yc * yc, bd) * (1.0 / RW_HD)
    yn = yc * lax.rsqrt(var + RW_GN_EPS) * lnw_ref[...] + lnb_ref[...]
    o_ref[0] = (yn + bon_ref[0] * v_ref[0]) * g_ref[0]


def _rw_chunk_index(d, s):
    n_lat = SEQ // RW_C
    n_all = T_ALL // RW_C
    fwd = jnp.where(s < n_all - n_lat, n_lat + s, s - (n_all - n_lat))
    return jnp.where(d == 0, fwd, n_all - 1 - s)


def _rwkv(p_rw, mu, w0, w2, a0, a2, g2, k_k, k_a, r_k, ln_w, ln_b):
    cos, sin = (jnp.asarray(a) for a in _rope_tables())
    bd = jnp.asarray(_head_blockdiag()).astype(BF16)
    zpad = lambda w: jnp.stack([jnp.concatenate([w[0], jnp.zeros_like(w[0])], axis=0),
                                jnp.concatenate([jnp.zeros_like(w[1]), w[1]], axis=0)])
    split = lambda w: jnp.stack(_split(w), axis=1)
    nt = T_ALL // RW_TR
    n_ch = T_ALL // RW_C
    tok_shape = jax.ShapeDtypeStruct((BATCH, T_ALL, RW_W), F32)
    dir_shape = jax.ShapeDtypeStruct((2, BATCH, T_ALL, RW_W), F32)
    dir_spec = pl.BlockSpec((2, 1, RW_TR, RW_W), lambda b, i: (0, b, i, 0))
    wc_shape = jax.ShapeDtypeStruct((2, BATCH, n_ch, 8, RW_W), F32)
    wc_spec = pl.BlockSpec((2, 1, RW_TR // RW_C, 8, RW_W), lambda b, i: (0, b, i, 0, 0))
    tok = lambda w: pl.BlockSpec((1, RW_TR, w), lambda b, i: (b, i, 0))
    const = lambda shape: pl.BlockSpec(shape, lambda b, i: (0,) * len(shape))
    row = lambda a: a.reshape(1, RW_W)
    outs = pl.pallas_call(
        _rw_prep_kernel,
        grid=(BATCH, nt),
        in_specs=[tok(RW_IN),
                  pl.BlockSpec((1, 8, RW_IN), lambda b, i: (b, jnp.maximum(i * (RW_TR // 8) - 1, 0), 0)),
                  pl.BlockSpec((1, 8, RW_IN), lambda b, i: (b, jnp.minimum((i + 1) * (RW_TR // 8), T_ALL // 8 - 1), 0)),
                  const((2, RW_IN)), const((2, RW_W)), const((2, 2, 2 * RW_LORA, RW_W)), const((2, RW_W)),
                  const((2, 2, 2 * RW_LORA, RW_W)), const((2, RW_GATE_LORA, RW_W)),
                  const((1, RW_W)), const((1, RW_W)), const((1, RW_W)),
                  pl.BlockSpec((RW_TR, RW_W), lambda b, i: (i, 0)), pl.BlockSpec((RW_TR, RW_W), lambda b, i: (i, 0)),
                  const((RW_W, RW_W)), const((2, RW_TR, RW_TR))],
        out_specs=[tok(RW_W)] * 3 + [dir_spec] * 4 + [wc_spec],
        out_shape=[tok_shape] * 3 + [dir_shape] * 4 + [wc_shape],
        compiler_params=_cparams(("parallel", "parallel")),
        name="rwkv_prepare",
    )(p_rw, p_rw, p_rw, mu, w0, split(zpad(w2)), a0, split(zpad(a2)), jnp.stack(_split(g2)),
      row(k_k), row(k_a), row(r_k.reshape(-1)), cos, sin, bd, jnp.asarray(_chunk_tri()).astype(BF16))
    v, bonus, g, at, rt, bt, kt, wc = outs

    shared = lambda d: pl.BlockSpec((BATCH, RW_C, RW_W), lambda s: (0, _rw_chunk_index(d, s), 0))
    perdir = lambda d: pl.BlockSpec((1, BATCH, RW_C, RW_W), lambda s: (d, 0, _rw_chunk_index(d, s), 0))
    wcdir = lambda d: pl.BlockSpec((1, BATCH, 1, 8, RW_W), lambda s: (d, 0, _rw_chunk_index(d, s), 0, 0))
    yf, yb = pl.pallas_call(
        _rw_scan_kernel,
        grid=(n_ch,),
        in_specs=[shared(0), shared(1)] + [perdir(0), perdir(1)] * 4 + [wcdir(0), wcdir(1)],
        out_specs=[shared(0), shared(1)],
        out_shape=[tok_shape, tok_shape],
        scratch_shapes=[pltpu.VMEM((2 * BATCH * RW_HEADS, RW_HD, RW_HD), F32)],
        compiler_params=_cparams(("arbitrary",)),
        name="rwkv_chunk_scan",
    )(v, v, at, at, rt, rt, bt, bt, kt, kt, wc, wc)

    tok2 = lambda w: pl.BlockSpec((1, PROJ_TM, w), lambda b, i: (b, i, 0))
    return pl.pallas_call(
        _rw_out_kernel,
        grid=(BATCH, T_ALL // PROJ_TM),
        in_specs=[tok2(RW_W), tok2(RW_W), tok2(RW_W), tok2(RW_W), tok2(RW_W),
                  const((1, RW_W)), const((1, RW_W)), const((RW_W, RW_W))],
        out_specs=tok2(RW_W),
        out_shape=jax.ShapeDtypeStruct((BATCH, T_ALL, RW_W), F32),
        compiler_params=_cparams(("parallel", "parallel")),
        name="rwkv_output",
    )(yf, yb, bonus, v, g, row(ln_w), row(ln_b), bd)


@functools.lru_cache(maxsize=None)
def _na_bias_tables():
    qc = np.arange(GRID_W)[:, None]
    kc = np.arange(GRID_W)[None, :]
    win0 = np.clip(qc - WIN_COLS // 2, 0, GRID_W - WIN_COLS)
    valid = (kc >= win0) & (kc < win0 + WIN_COLS)
    off = np.clip(kc - qc + WIN_COLS - 1, 0, 2 * WIN_COLS - 2)
    onehot = np.zeros((128, GRID_W * GRID_W), np.float32)
    onehot[off.reshape(-1), np.arange(GRID_W * GRID_W)] = valid.reshape(-1)
    mask = np.where(valid, 0.0, NEG_INF).astype(np.float32).reshape(1, GRID_W * GRID_W)
    return onehot, mask


def _na_bias_kernel(rpb_ref, oh_ref, mask_ref, o_ref):
    o_ref[...] = _dot(rpb_ref[...], oh_ref[...]) + mask_ref[...]


def _na_lat_kernel(q_ref, k_ref, v_ref, tp_ref, o_ref):
    for g in range(NA_G):
        _na_lat_row(pl.program_id(1) * NA_G + g, g, q_ref, k_ref, v_ref, tp_ref, o_ref)


def _na_lat_row(i, g, q_ref, k_ref, v_ref, tp_ref, o_ref):
    start = jnp.clip(i - WIN_ROWS // 2, 0, N_ROWS - WIN_ROWS)
    dr0 = start - i + WIN_ROWS - 1
    scale = NA_HD ** -0.5
    lane = lax.broadcasted_iota(jnp.int32, (1, NA_W), 1)
    hmask = [(lane // NA_HD) == h for h in range(NA_HEADS)]
    q = q_ref[0, g * GRID_W:(g + 1) * GRID_W, :]
    qm = jnp.concatenate([jnp.where(hmask[h], q, jnp.zeros_like(q)) for h in range(NA_HEADS)], axis=0)
    row0 = pl.multiple_of(start * GRID_W, GRID_W)
    kl = k_ref[0, pl.ds(row0, WIN_ROWS * GRID_W), :]
    vl = v_ref[0, pl.ds(row0, WIN_ROWS * GRID_W), :]
    kc = k_ref[0, SEQ:T_ALL, :]
    vc = v_ref[0, SEQ:T_ALL, :]
    nt = (((1,), (1,)), ((), ()))
    bias = jnp.concatenate(
        [jnp.concatenate([tp_ref[h, pl.ds(dr0 + 2 * j, 1)][0] for j in range(WIN_ROWS // 2)], axis=1)
         for h in range(NA_HEADS)], axis=0)
    s_loc = lax.dot_general(qm, kl, nt, preferred_element_type=F32) * scale + bias
    s_ctx = lax.dot_general(qm, kc, nt, preferred_element_type=F32) * scale
    m = jnp.maximum(jnp.max(s_loc, axis=-1, keepdims=True), jnp.max(s_ctx, axis=-1, keepdims=True))
    p_loc = jnp.exp(s_loc - m)
    p_ctx = jnp.exp(s_ctx - m)
    inv = 1.0 / (jnp.sum(p_loc, axis=-1, keepdims=True) + jnp.sum(p_ctx, axis=-1, keepdims=True))
    o_all = (jnp.dot((p_loc * inv).astype(BF16), vl, preferred_element_type=F32)
             + jnp.dot((p_ctx * inv).astype(BF16), vc, preferred_element_type=F32))
    o = jnp.zeros((GRID_W, NA_W), F32)
    for h in range(NA_HEADS):
        o = o + jnp.where(hmask[h], o_all[h * GRID_W:(h + 1) * GRID_W, :], 0.0)
    o_ref[0, g * GRID_W:(g + 1) * GRID_W, :] = o


def _na_ctx_kernel(q_ref, k_ref, v_ref, o_ref):
    scale = NA_HD ** -0.5
    lane = lax.broadcasted_iota(jnp.int32, (1, NA_W), 1)
    hmask = [(lane // NA_HD) == h for h in range(NA_HEADS)]
    q, k, v = q_ref[0], k_ref[0], v_ref[0]
    qm = jnp.concatenate([jnp.where(hmask[h], q, jnp.zeros_like(q)) for h in range(NA_HEADS)], axis=0)
    s = lax.dot_general(qm, k, (((1,), (1,)), ((), ())), preferred_element_type=F32) * scale
    p = jnp.exp(s - jnp.max(s, axis=-1, keepdims=True))
    p = p / jnp.sum(p, axis=-1, keepdims=True)
    o_all = jnp.dot(p.astype(BF16), v, preferred_element_type=F32)
    o = jnp.zeros((CTX_LEN, NA_W), F32)
    for h in range(NA_HEADS):
        o = o + jnp.where(hmask[h], o_all[h * CTX_LEN:(h + 1) * CTX_LEN, :], 0.0)
    o_ref[0] = o


def _natten(p_na, rpb):
    onehot, mask = (jnp.asarray(a) for a in _na_bias_tables())
    n_dr = 2 * WIN_ROWS - 1
    n_rel = 2 * WIN_COLS - 1
    rows = NA_HEADS * n_dr
    rows_p = -(-rows // 8) * 8
    rpb2 = jnp.zeros((rows_p, 128), F32).at[:rows, :n_rel].set(rpb.reshape(rows, n_rel))
    tb = pl.pallas_call(
        _na_bias_kernel,
        out_shape=jax.ShapeDtypeStruct((rows_p, GRID_W * GRID_W), F32),
        name="natten_bias_expand",
    )(rpb2, onehot, mask)
    tb = tb[:rows].reshape(NA_HEADS, n_dr, GRID_W, GRID_W)
    nxt = jnp.concatenate([tb[:, 1:], jnp.full_like(tb[:, :1], NEG_INF)], axis=1)
    tp = jnp.concatenate([tb, nxt], axis=-1)

    o_lat = pl.pallas_call(
        _na_lat_kernel,
        grid=(BATCH, N_ROWS // NA_G),
        in_specs=[pl.BlockSpec((1, NA_G * GRID_W, NA_W), lambda b, i: (b, i, 0)),
                  pl.BlockSpec((1, T_ALL, NA_W), lambda b, i: (b, 0, 1)),
                  pl.BlockSpec((1, T_ALL, NA_W), lambda b, i: (b, 0, 2)),
                  pl.BlockSpec((NA_HEADS, n_dr, GRID_W, 2 * GRID_W), lambda b, i: (0, 0, 0, 0))],
        out_specs=pl.BlockSpec((1, NA_G * GRID_W, NA_W), lambda b, i: (b, i, 0)),
        out_shape=jax.ShapeDtypeStruct((BATCH, SEQ, NA_W), F32),
        compiler_params=_cparams(("parallel", "parallel")),
        name="natten_latent",
    )(p_na, p_na, p_na, tp)

    cblk = SEQ // CTX_LEN
    o_ctx = pl.pallas_call(
        _na_ctx_kernel,
        grid=(BATCH,),
        in_specs=[pl.BlockSpec((1, CTX_LEN, NA_W), lambda b: (b, cblk, 0)),
                  pl.BlockSpec((1, CTX_LEN, NA_W), lambda b: (b, cblk, 1)),
                  pl.BlockSpec((1, CTX_LEN, NA_W), lambda b: (b, cblk, 2))],
        out_specs=pl.BlockSpec((1, CTX_LEN, NA_W), lambda b: (b, 0, 0)),
        out_shape=jax.ShapeDtypeStruct((BATCH, CTX_LEN, NA_W), F32),
        compiler_params=_cparams(("parallel",)),
        name="natten_context",
    )(p_na, p_na, p_na)
    return jnp.concatenate([o_lat, o_ctx], axis=1)


def kernel(x, c, ctx, c_ctx, mod_w, mod_b, norm_g, ffn1_wgu, ffn1_wdn, ffn2_wgu, ffn2_wdn, w_in, w_out, hy_conv_w, hy_conv_b, hy_f_w1, hy_f_b1, hy_f_w2, hy_f_b2, hy_f_w3, hy_freq, hy_bias, rw_mu, rw_w0, rw_w2, rw_a0, rw_a2, rw_g2, rw_k_k, rw_k_a, rw_r_k, rw_ln_w, rw_ln_b, na_rpb):
    assert x.shape == (BATCH, SEQ, D_MODEL) and ctx.shape == (BATCH, CTX_LEN, D_MODEL)
    mods = _modulation(c, c_ctx, mod_w, mod_b)
    h = jnp.concatenate([x, ctx], axis=1)
    gpad = jnp.concatenate([norm_g, jnp.zeros((DEPTH, 2, D_MODEL), F32)], axis=1)
    for l in range(DEPTH):
        m_l, g_l = mods[l], gpad[l]
        h = _ffn(h, m_l, g_l, ffn1_wgu[l].astype(BF16), ffn1_wdn[l].astype(BF16), 0)
        p_hy, p_rw, p_na = _in_proj(h, m_l, g_l, w_in[l].astype(BF16))
        o_hy = _hyena(p_hy, hy_conv_w[l], hy_conv_b[l], hy_f_w1[l], hy_f_b1[l], hy_f_w2[l], hy_f_b2[l],
                      hy_f_w3[l], hy_freq[l], hy_bias[l])
        o_rw = _rwkv(p_rw, rw_mu[l], rw_w0[l], rw_w2[l], rw_a0[l], rw_a2[l], rw_g2[l], rw_k_k[l], rw_k_a[l],
                     rw_r_k[l], rw_ln_w[l], rw_ln_b[l])
        o_na = _natten(p_na, na_rpb[l])
        h = _out_proj(h, o_hy, o_rw, o_na, m_l, g_l, w_out[l].astype(BF16))
        h = _ffn(h, m_l, g_l, ffn2_wgu[l].astype(BF16), ffn2_wdn[l].astype(BF16), 1)
    return h[:, :SEQ]
```

```python
import functools
import math

import numpy as np
import jax
import jax.numpy as jnp
from jax import lax
from jax.experimental import pallas as pl
from jax.experimental.pallas import tpu as pltpu

D_MODEL = 1024
BATCH = 2
SEQ = 8192
DEPTH = 4
GRID_W = 64
CTX_LEN = 256
T_ALL = SEQ + CTX_LEN
N_MOD = 9
D_FF = 2816
MACARON_W = 0.5
NORM_EPS = 1e-6
NEG_INF = -1e30

HY_CH = 256
HY_ORDER = 2
HY_EMB = 33
HY_FILT = 64
HY_TARGET = 1e-2
HY_FAST = 0.3
HY_SLOW = 1.5
HY_IN = (HY_ORDER + 1) * HY_CH

RW_HEADS = 6
RW_HD = 64
RW_W = RW_HEADS * RW_HD
RW_LORA = 64
RW_GATE_LORA = 128
RW_GN_EPS = 64e-5
ROPE_BASE = 10000.0
RW_IN = 3 * RW_W + 4 * RW_LORA + RW_GATE_LORA

NA_HEADS = 6
NA_HD = 64
NA_W = NA_HEADS * NA_HD
WIN_ROWS = 8
WIN_COLS = 16
NA_IN = 3 * NA_W
N_ROWS = SEQ // GRID_W

F32 = jnp.float32
BF16 = jnp.bfloat16
HI = lax.Precision.HIGHEST

VMEM_LIMIT = 56 * 1024 * 1024

FFN_TM = 768
FFN_TF = 1408
FFN_SUB = 2
PROJ_TM = 384
RW_TR = 256
RW_C = 64
FFT_N1 = 128
FFT_N2 = 128
FFT_CT = 16
NA_G = 4


def _dot(a, b):
    return jnp.dot(a, b, precision=HI, preferred_element_type=F32)


def _split(x):
    hi = x.astype(BF16)
    return hi, (x - hi.astype(F32)).astype(BF16)


def _dot3(x, w_ref, idx=()):
    xh, xm = _split(x)
    wh, wm = w_ref[idx + (0,)], w_ref[idx + (1,)]
    d = functools.partial(jnp.dot, preferred_element_type=F32)
    return d(xh, wh) + d(xm, wh) + d(xh, wm)


def _split3(x):
    hi = x.astype(BF16)
    r1 = x - hi.astype(F32)
    mid = r1.astype(BF16)
    return hi, mid, (r1 - mid.astype(F32)).astype(BF16)


def _dot01(x, m, const_left=False):
    d = functools.partial(jnp.dot, preferred_element_type=F32)
    parts = _split3(x)
    if const_left:
        return d(m, parts[0]) + d(m, parts[1]) + d(m, parts[2])
    return d(parts[0], m) + d(parts[1], m) + d(parts[2], m)


def _cparams(sem, vmem=VMEM_LIMIT):
    return pltpu.CompilerParams(dimension_semantics=sem, vmem_limit_bytes=vmem)


def _mod_kernel(cv_ref, w_ref, b_ref, o_ref):
    cv = cv_ref[...]
    s = cv * jax.nn.sigmoid(cv)
    o_ref[0] = _dot(s, w_ref[0]) + b_ref[0]


def _modulation(c, c_ctx, mod_w, mod_b):
    cv = jnp.concatenate([c, c_ctx[None, :], jnp.zeros((8 - BATCH - 1, D_MODEL), F32)], axis=0)
    tn = 1536
    return pl.pallas_call(
        _mod_kernel,
        grid=(DEPTH, N_MOD * D_MODEL // tn),
        in_specs=[pl.BlockSpec((8, D_MODEL), lambda l, j: (0, 0)),
                  pl.BlockSpec((1, D_MODEL, tn), lambda l, j: (l, 0, j)),
                  pl.BlockSpec((1, 1, tn), lambda l, j: (l, 0, j))],
        out_specs=pl.BlockSpec((1, 8, tn), lambda l, j: (l, 0, j)),
        out_shape=jax.ShapeDtypeStruct((DEPTH, 8, N_MOD * D_MODEL), F32),
        compiler_params=_cparams(("parallel", "parallel")),
        name="adaln_modulation",
    )(cv, mod_w, mod_b.reshape(DEPTH, 1, N_MOD * D_MODEL))


def _mod_rows(m_ref, b, i, tm, idx, sub=0, n_sub=1):
    n = tm // n_sub
    t = i * tm + sub * n + lax.broadcasted_iota(jnp.int32, (n, 1), 0)
    ml = m_ref[pl.ds(b, 1), idx * D_MODEL:(idx + 1) * D_MODEL]
    mc = m_ref[BATCH:BATCH + 1, idx * D_MODEL:(idx + 1) * D_MODEL]
    return jnp.where(t >= SEQ, mc, ml)


def _rms(x, g):
    return x * lax.rsqrt(jnp.mean(x * x, axis=-1, keepdims=True) + NORM_EPS) * g


def _ffn_kernel(h_ref, m_ref, g_ref, wgu_ref, wdn_ref, o_ref, *, mi, gi):
    b, i = pl.program_id(0), pl.program_id(1)
    n = FFN_TM // FFN_SUB
    for sub in range(FFN_SUB):
        mod = functools.partial(_mod_rows, m_ref, b, i, FFN_TM, sub=sub, n_sub=FFN_SUB)
        h = h_ref[0, sub * n:(sub + 1) * n, :]
        u = (_rms(h, g_ref[gi:gi + 1, :]) * (1.0 + mod(mi + 1)) + mod(mi)).astype(BF16)
        acc = jnp.zeros((n, D_MODEL), F32)
        for k in range(D_FF // FFN_TF):
            gate = jnp.dot(u, wgu_ref[:, k * FFN_TF:(k + 1) * FFN_TF], preferred_element_type=F32)
            up = jnp.dot(u, wgu_ref[:, D_FF + k * FFN_TF:D_FF + (k + 1) * FFN_TF], preferred_element_type=F32)
            a = (gate * jax.nn.sigmoid(gate) * up).astype(BF16)
            acc = acc + jnp.dot(a, wdn_ref[k * FFN_TF:(k + 1) * FFN_TF, :], preferred_element_type=F32)
        y = _rms(acc, g_ref[gi + 1:gi + 2, :])
        o_ref[0, sub * n:(sub + 1) * n, :] = h + MACARON_W * mod(mi + 2) * y


def _ffn(h, mods_l, g_l, wgu, wdn, which):
    mi, gi = (0, 0) if which == 0 else (6, 4)
    resident = pl.Buffered(1)
    return pl.pallas_call(
        functools.partial(_ffn_kernel, mi=mi, gi=gi),
        grid=(BATCH, T_ALL // FFN_TM),
        in_specs=[pl.BlockSpec((1, FFN_TM, D_MODEL), lambda b, i: (b, i, 0)),
                  pl.BlockSpec((8, N_MOD * D_MODEL), lambda b, i: (0, 0)),
                  pl.BlockSpec((8, D_MODEL), lambda b, i: (0, 0)),
                  pl.BlockSpec((D_MODEL, 2 * D_FF), lambda b, i: (0, 0), pipeline_mode=resident),
                  pl.BlockSpec((D_FF, D_MODEL), lambda b, i: (0, 0), pipeline_mode=resident)],
        out_specs=pl.BlockSpec((1, FFN_TM, D_MODEL), lambda b, i: (b, i, 0)),
        out_shape=jax.ShapeDtypeStruct((BATCH, T_ALL, D_MODEL), F32),
        compiler_params=_cparams(("parallel", "parallel")),
        name="ffn_sublayer",
    )(h, mods_l, g_l, wgu, wdn)


def _inproj_kernel(h_ref, m_ref, g_ref, w_ref, hy_ref, rw_ref, na_ref):
    b, i = pl.program_id(0), pl.program_id(1)
    y = _rms(h_ref[0], g_ref[2:3, :])
    u = y * (1.0 + _mod_rows(m_ref, b, i, PROJ_TM, 4)) + _mod_rows(m_ref, b, i, PROJ_TM, 3)
    p = jnp.dot(u.astype(BF16), w_ref[...], preferred_element_type=F32)
    hy_ref[0] = p[:, :HY_IN]
    rw_ref[0] = p[:, HY_IN:HY_IN + RW_IN]
    na_ref[0] = p[:, HY_IN + RW_IN:].astype(BF16)


def _in_proj(h, mods_l, g_l, w_in):
    n_in = HY_IN + RW_IN + NA_IN
    return pl.pallas_call(
        _inproj_kernel,
        grid=(BATCH, T_ALL // PROJ_TM),
        in_specs=[pl.BlockSpec((1, PROJ_TM, D_MODEL), lambda b, i: (b, i, 0)),
                  pl.BlockSpec((8, N_MOD * D_MODEL), lambda b, i: (0, 0)),
                  pl.BlockSpec((8, D_MODEL), lambda b, i: (0, 0)),
                  pl.BlockSpec((D_MODEL, n_in), lambda b, i: (0, 0))],
        out_specs=[pl.BlockSpec((1, PROJ_TM, HY_IN), lambda b, i: (b, i, 0)),
                   pl.BlockSpec((1, PROJ_TM, RW_IN), lambda b, i: (b, i, 0)),
                   pl.BlockSpec((1, PROJ_TM, NA_IN), lambda b, i: (b, i, 0))],
        out_shape=[jax.ShapeDtypeStruct((BATCH, T_ALL, HY_IN), F32),
                   jax.ShapeDtypeStruct((BATCH, T_ALL, RW_IN), F32),
                   jax.ShapeDtypeStruct((BATCH, T_ALL, NA_IN), BF16)],
        compiler_params=_cparams(("parallel", "parallel")),
        name="mixer_in_proj",
    )(h, mods_l, g_l, w_in)


def _outproj_kernel(h_ref, hy_ref, rw_ref, na_ref, m_ref, g_ref, w_ref, o_ref):
    b, i = pl.program_id(0), pl.program_id(1)
    cat = jnp.concatenate([hy_ref[0], rw_ref[0], na_ref[0]], axis=-1).astype(BF16)
    o = jnp.dot(cat, w_ref[...], preferred_element_type=F32)
    o_ref[0] = h_ref[0] + _mod_rows(m_ref, b, i, PROJ_TM, 5) * _rms(o, g_ref[3:4, :])


def _out_proj(h, o_hy, o_rw, o_na, mods_l, g_l, w_out):
    return pl.pallas_call(
        _outproj_kernel,
        grid=(BATCH, T_ALL // PROJ_TM),
        in_specs=[pl.BlockSpec((1, PROJ_TM, D_MODEL), lambda b, i: (b, i, 0)),
                  pl.BlockSpec((1, PROJ_TM, HY_CH), lambda b, i: (b, i, 0)),
                  pl.BlockSpec((1, PROJ_TM, RW_W), lambda b, i: (b, i, 0)),
                  pl.BlockSpec((1, PROJ_TM, NA_W), lambda b, i: (b, i, 0)),
                  pl.BlockSpec((8, N_MOD * D_MODEL), lambda b, i: (0, 0)),
                  pl.BlockSpec((8, D_MODEL), lambda b, i: (0, 0)),
                  pl.BlockSpec((D_MODEL, D_MODEL), lambda b, i: (0, 0))],
        out_specs=pl.BlockSpec((1, PROJ_TM, D_MODEL), lambda b, i: (b, i, 0)),
        out_shape=jax.ShapeDtypeStruct((BATCH, T_ALL, D_MODEL), F32),
        compiler_params=_cparams(("parallel", "parallel")),
        name="mixer_out_proj",
    )(h, o_hy, o_rw, o_na, mods_l, g_l, w_out)


def _hy_short_kernel(x_ref, w_ref, b_ref, o_ref):
    x = x_ref[0]
    n = x.shape[0]
    row = lax.broadcasted_iota(jnp.int32, (n, 1), 0)
    prev = jnp.where(row == 0, 0.0, pltpu.roll(x, 1, 0))
    nxt = jnp.where(row == n - 1, 0.0, pltpu.roll(x, n - 1, 0))
    o_ref[0] = prev * w_ref[0:1, :] + x * w_ref[1:2, :] + nxt * w_ref[2:3, :] + b_ref[...]


def _hy_short(p_hy, conv_w, conv_b, seq_len, row_block):
    lanes = 128
    return pl.pallas_call(
        _hy_short_kernel,
        grid=(BATCH, HY_IN // lanes),
        in_specs=[pl.BlockSpec((1, seq_len, lanes), lambda b, j: (b, row_block, j)),
                  pl.BlockSpec((3, lanes), lambda b, j: (0, j)),
                  pl.BlockSpec((1, lanes), lambda b, j: (0, j))],
        out_specs=pl.BlockSpec((1, seq_len, lanes), lambda b, j: (b, 0, j)),
        out_shape=jax.ShapeDtypeStruct((BATCH, seq_len, HY_IN), F32),
        compiler_params=_cparams(("parallel", "parallel")),
        name="hyena_short_conv",
    )(p_hy, conv_w, conv_b.reshape(1, HY_IN))


@functools.lru_cache(maxsize=None)
def _filter_features(seq_len):
    n = 2 * seq_len
    m = np.arange(n)
    d = np.where(m < seq_len, m, n - m).astype(np.float64)
    t = d / (seq_len - 1)
    ang = 2.0 * math.pi * d / seq_len
    bands = (HY_EMB - 1) // 2
    fr = np.linspace(1e-4, bands - 1, bands)
    z = np.zeros((n, 128), np.float64)
    z[:, 0] = t
    z[:, 1:1 + bands] = np.cos(fr[None, :] * ang[:, None])
    z[:, 1 + bands:1 + 2 * bands] = -np.sin(fr[None, :] * ang[:, None])
    z[:, 33] = (m < seq_len)
    z[:, 34] = (m > seq_len)
    z[seq_len, :] = 0.0
    return z.astype(np.float32)


@functools.lru_cache(maxsize=None)
def _filter_deltas():
    d = np.abs(np.linspace(math.log(HY_TARGET) / HY_SLOW, math.log(HY_TARGET) / HY_FAST, HY_CH))
    return d.astype(np.float32).reshape(1, HY_CH)


def _hy_filter_kernel(z_ref, w1_ref, b1_ref, w2_ref, b2_ref, w3_ref, fq_ref, dl_ref, o_ref):
    z = z_ref[...]
    fq = fq_ref[...]
    h = jnp.sin(fq * (_dot(z, w1_ref[...]) + b1_ref[...]))
    h = jnp.sin(fq * (_dot(h, w2_ref[...]) + b2_ref[...]))
    o = _dot3(h, w3_ref)
    dec = jnp.exp(-z[:, 0:1] * dl_ref[...])
    mf = z[:, 33:34]
    mb = z[:, 34:35]
    outs = []
    for od in range(HY_ORDER):
        fwd = o[:, od * 2 * HY_CH:od * 2 * HY_CH + HY_CH]
        bwd = o[:, od * 2 * HY_CH + HY_CH:(od + 1) * 2 * HY_CH]
        outs.append((mf * fwd + mb * bwd) * dec)
    o_ref[...] = jnp.concatenate(outs, axis=1)


def _hy_filter(seq_len, w1, b1, w2, b2, w3, freq):
    n = 2 * seq_len
    tp = min(n, 1024)
    zt = jnp.asarray(_filter_features(seq_len))
    w1p = jnp.zeros((128, HY_FILT), F32).at[:HY_EMB].set(w1)
    full = lambda shape: pl.BlockSpec(shape, lambda i: (0,) * len(shape))
    return pl.pallas_call(
        _hy_filter_kernel,
        grid=(n // tp,),
        in_specs=[pl.BlockSpec((tp, 128), lambda i: (i, 0)),
                  full((128, HY_FILT)), full((1, HY_FILT)), full((HY_FILT, HY_FILT)), full((1, HY_FILT)),
                  full((2, HY_FILT, HY_ORDER * 2 * HY_CH)), full((1, HY_FILT)), full((1, HY_CH))],
        out_specs=pl.BlockSpec((tp, HY_ORDER * HY_CH), lambda i: (i, 0)),
        out_shape=jax.ShapeDtypeStruct((n, HY_ORDER * HY_CH), F32),
        compiler_params=_cparams(("parallel",)),
        name="hyena_filter_mlp",
    )(zt, w1p, b1.reshape(1, -1), w2, b2.reshape(1, -1), jnp.stack(_split(w3)), freq.reshape(1, -1),
      jnp.asarray(_filter_deltas()))


@functools.lru_cache(maxsize=None)
def _fft_tables():
    n1 = np.arange(FFT_N1)[:, None]
    k1 = np.arange(FFT_N1)[None, :]
    th = 2.0 * math.pi * n1 * k1 / FFT_N1
    c, s = np.cos(th), np.sin(th)
    half = FFT_N1 // 2
    big_n = FFT_N1 * FFT_N2
    wa_c = np.block([[c[:half], -s[:half]], [s[:half], c[:half]]])
    wa_r = np.concatenate([c, -s], axis=1)
    tw = 2.0 * math.pi * np.arange(FFT_N2)[:, None] * np.arange(FFT_N1)[None, :] / big_n
    twf = np.concatenate([np.cos(tw), -np.sin(tw)], axis=1)
    twi = np.concatenate([np.cos(tw.T), np.sin(tw.T)], axis=1)
    wb = np.block([[c, -s], [s, c]])
    wc = np.block([[c, s], [-s, c]])
    wd = np.block([[c[:, :half], s[:, :half]], [-s[:, :half], c[:, :half]]]) / big_n
    return tuple(np.asarray(a, np.float32) for a in (wa_c, wa_r, twf, twi, wb, wc, wd))


def _fft_fwd(z, wa_ref, tw, wb_ref):
    ct = z.shape[0]
    a = _dot3(z.reshape(ct * FFT_N2, z.shape[2]), wa_ref).reshape(ct, FFT_N2, 2 * FFT_N1)
    ar, ai = a[:, :, :FFT_N1], a[:, :, FFT_N1:]
    tr, ti = tw[:, :FFT_N1], tw[:, FFT_N1:]
    br = ar * tr - ai * ti
    bi = ar * ti + ai * tr
    bt = jnp.concatenate([jnp.swapaxes(br, 1, 2), jnp.swapaxes(bi, 1, 2)], axis=2)
    return _dot3(bt.reshape(ct * FFT_N1, 2 * FFT_N2), wb_ref).reshape(ct, FFT_N1, 2 * FFT_N2)


def _hy_kfft_kernel(k_ref, wa_ref, tw_ref, wb_ref, o_ref):
    o_ref[...] = _fft_fwd(k_ref[...], wa_ref, tw_ref[...], wb_ref)


def _hy_long_kernel(z_ref, g_ref, kh_ref, bias_ref, wa_ref, twf_ref, wb_ref, wc_ref, twi_ref, wd_ref, o_ref):
    z = z_ref[...]
    ct = z.shape[0]
    x = _fft_fwd(z, wa_ref, twf_ref[...], wb_ref)
    kh = kh_ref[...]
    xr, xi = x[:, :, :FFT_N2], x[:, :, FFT_N2:]
    kr, ki = kh[:, :, :FFT_N2], kh[:, :, FFT_N2:]
    y = jnp.concatenate([xr * kr - xi * ki, xr * ki + xi * kr], axis=2)
    c1 = _dot3(y.reshape(ct * FFT_N1, 2 * FFT_N2), wc_ref).reshape(ct, FFT_N1, 2 * FFT_N2)
    cr, ci = c1[:, :, :FFT_N2], c1[:, :, FFT_N2:]
    twi = twi_ref[...]
    tr, ti = twi[:, :FFT_N2], twi[:, FFT_N2:]
    dr = cr * tr - ci * ti
    di = cr * ti + ci * tr
    dt = jnp.concatenate([jnp.swapaxes(dr, 1, 2), jnp.swapaxes(di, 1, 2)], axis=2)
    conv = _dot3(dt.reshape(ct * FFT_N2, 2 * FFT_N1), wd_ref).reshape(ct, FFT_N2, FFT_N1)
    o_ref[...] = (conv + bias_ref[...] * z) * g_ref[...]


def _hy_ctx_kernel(x_ref, kc_ref, bias_ref, cf_ref, sf_ref, o_ref):
    x = x_ref[0]
    cf, sf = cf_ref[...], sf_ref[...]
    n = CTX_LEN
    z = x[:, :HY_CH]
    for od in range(HY_ORDER):
        k = kc_ref[:, od * HY_CH:(od + 1) * HY_CH]
        kr, ki = _dot(cf, k), -_dot(sf, k)
        xr, xi = _dot(cf[:, :n], z), -_dot(sf[:, :n], z)
        yr = xr * kr - xi * ki
        yi = xr * ki + xi * kr
        conv = (_dot(cf[:n, :], yr) - _dot(sf[:n, :], yi)) * (1.0 / (2 * n))
        z = x[:, (od + 1) * HY_CH:(od + 2) * HY_CH] * (conv + bias_ref[od:od + 1, :] * z)
    o_ref[0] = z


@functools.lru_cache(maxsize=None)
def _ctx_dft():
    m = np.arange(2 * CTX_LEN)
    th = 2.0 * math.pi * m[:, None] * m[None, :] / (2 * CTX_LEN)
    return np.cos(th).astype(np.float32), np.sin(th).astype(np.float32)


def _hyena(p_hy, conv_w, conv_b, w1, b1, w2, b2, w3, freq, bias):
    wa_c, wa_r, twf, twi, wb, wc, wd = (jnp.asarray(a) for a in _fft_tables())
    wa_c, wa_r, wb, wc, wd = (jnp.stack(_split(a)) for a in (wa_c, wa_r, wb, wc, wd))
    nch = HY_ORDER * HY_CH
    kf = _hy_filter(SEQ, w1, b1, w2, b2, w3, freq)
    kt = kf.reshape(FFT_N1, FFT_N2, nch).transpose(2, 1, 0)
    const = lambda shape: pl.BlockSpec(shape, lambda j: (0,) * len(shape))
    khat = pl.pallas_call(
        _hy_kfft_kernel,
        grid=(nch // FFT_CT,),
        in_specs=[pl.BlockSpec((FFT_CT, FFT_N2, FFT_N1), lambda j: (j, 0, 0)),
                  const((2, FFT_N1, 2 * FFT_N1)), const((FFT_N2, 2 * FFT_N1)), const((2, 2 * FFT_N2, 2 * FFT_N2))],
        out_specs=pl.BlockSpec((FFT_CT, FFT_N1, 2 * FFT_N2), lambda j: (j, 0, 0)),
        out_shape=jax.ShapeDtypeStruct((nch, FFT_N1, 2 * FFT_N2), F32),
        compiler_params=_cparams(("parallel",)),
        name="hyena_filter_fft",
    )(kt, wa_r, twf, wb)

    pc = _hy_short(p_hy, conv_w, conv_b, SEQ, 0)
    half = FFT_N1 // 2
    zall = pc.reshape(BATCH, half, FFT_N2, HY_IN).transpose(3, 2, 0, 1).reshape(HY_IN, FFT_N2, FFT_N1)
    nblk = HY_CH // FFT_CT
    z = zall
    for od in range(HY_ORDER):
        z_spec = pl.BlockSpec((FFT_CT, FFT_N2, FFT_N1), lambda j: (j, 0, 0))
        g_spec = pl.BlockSpec((FFT_CT, FFT_N2, FFT_N1), lambda j, od=od: ((od + 1) * nblk + j, 0, 0))
        z = pl.pallas_call(
            _hy_long_kernel,
            grid=(nblk,),
            in_specs=[z_spec, g_spec,
                      pl.BlockSpec((FFT_CT, FFT_N1, 2 * FFT_N2), lambda j, od=od: (od * nblk + j, 0, 0)),
                      pl.BlockSpec((FFT_CT, 1, 1), lambda j, od=od: (od * nblk + j, 0, 0)),
                      const((2, FFT_N1, 2 * FFT_N1)), const((FFT_N2, 2 * FFT_N1)), const((2, 2 * FFT_N2, 2 * FFT_N2)),
                      const((2, 2 * FFT_N2, 2 * FFT_N2)), const((FFT_N1, 2 * FFT_N2)), const((2, 2 * FFT_N1, FFT_N1))],
            out_specs=pl.BlockSpec((FFT_CT, FFT_N2, FFT_N1), lambda j: (j, 0, 0)),
            out_shape=jax.ShapeDtypeStruct((HY_CH, FFT_N2, FFT_N1), F32),
            compiler_params=_cparams(("parallel",)),
            name="hyena_long_conv",
        )(z, zall, khat, bias.reshape(nch, 1, 1), wa_c, twf, wb, wc, twi, wd)
    o_lat = z.reshape(HY_CH, FFT_N2, BATCH, half).transpose(2, 3, 1, 0).reshape(BATCH, SEQ, HY_CH)

    kc = _hy_filter(CTX_LEN, w1, b1, w2, b2, w3, freq)
    pcc = _hy_short(p_hy, conv_w, conv_b, CTX_LEN, SEQ // CTX_LEN)
    cf, sf = (jnp.asarray(a) for a in _ctx_dft())
    o_ctx = pl.pallas_call(
        _hy_ctx_kernel,
        grid=(BATCH,),
        in_specs=[pl.BlockSpec((1, CTX_LEN, HY_IN), lambda b: (b, 0, 0)),
                  pl.BlockSpec((2 * CTX_LEN, nch), lambda b: (0, 0)),
                  pl.BlockSpec((HY_ORDER, HY_CH), lambda b: (0, 0)),
                  pl.BlockSpec((2 * CTX_LEN, 2 * CTX_LEN), lambda b: (0, 0)),
                  pl.BlockSpec((2 * CTX_LEN, 2 * CTX_LEN), lambda b: (0, 0))],
        out_specs=pl.BlockSpec((1, CTX_LEN, HY_CH), lambda b: (b, 0, 0)),
        out_shape=jax.ShapeDtypeStruct((BATCH, CTX_LEN, HY_CH), F32),
        compiler_params=_cparams(("parallel",)),
        name="hyena_context",
    )(pcc, kc, bias, cf, sf)
    return jnp.concatenate([o_lat, o_ctx], axis=1)


@functools.lru_cache(maxsize=None)
def _rope_tables():
    nf = RW_HD // 4
    t = np.arange(SEQ)
    row = (t // GRID_W).astype(np.float64)
    col = (t % GRID_W).astype(np.float64)
    inv = ROPE_BASE ** (-np.arange(nf, dtype=np.float64) / nf)
    ang = np.concatenate([row[:, None] * inv] * 2 + [col[:, None] * inv] * 2, axis=-1)
    cos = np.concatenate([np.cos(ang), np.ones((CTX_LEN, RW_HD))], axis=0)
    sin = np.concatenate([np.sin(ang), np.zeros((CTX_LEN, RW_HD))], axis=0)
    return (np.tile(cos, (1, RW_HEADS)).astype(np.float32), np.tile(sin, (1, RW_HEADS)).astype(np.float32))


@functools.lru_cache(maxsize=None)
def _head_blockdiag():
    h = np.arange(RW_W) // RW_HD
    return (h[:, None] == h[None, :]).astype(np.float32)


@functools.lru_cache(maxsize=None)
def _chunk_tri():
    t = np.arange(RW_TR)[:, None]
    s = np.arange(RW_TR)[None, :]
    same = (t // RW_C) == (s // RW_C)
    return np.stack([same & (s <= t), same & (s >= t)]).astype(np.float32)


def _rope(x, cos, sin):
    lane = lax.broadcasted_iota(jnp.int32, (1, RW_W), 1)
    first = (lane % (RW_HD // 2)) < (RW_HD // 4)
    rot = jnp.where(first, -pltpu.roll(x, RW_W - RW_HD // 4, 1), pltpu.roll(x, RW_HD // 4, 1))
    return x * cos + rot * sin


def _rw_prep_kernel(p_ref, pv_ref, nx_ref, mu_ref, w0_ref, w2_ref, a0_ref, a2_ref, g2_ref, kk_ref, ka_ref, rk_ref,
                    cos_ref, sin_ref, bd_ref, tri_ref,
                    v_ref, bon_ref, g_ref, at_ref, rt_ref, bt_ref, kt_ref, wc_ref):
    i = pl.program_id(1)
    x = p_ref[0]
    row = lax.broadcasted_iota(jnp.int32, (RW_TR, 1), 0)
    t = i * RW_TR + row
    seq_first = (t == 0) | (t == SEQ)
    seq_last = (t == SEQ - 1) | (t == T_ALL - 1)
    prev = jnp.where(row == 0, pv_ref[0, 7:8, :], pltpu.roll(x, 1, 0))
    prev = jnp.where(seq_first, 0.0, prev)
    nxt = jnp.where(row == RW_TR - 1, nx_ref[0, 0:1, :], pltpu.roll(x, RW_TR - 1, 0))
    nxt = jnp.where(seq_last, 0.0, nxt)
    xs = x + mu_ref[0:1, :] * (prev - x) + mu_ref[1:2, :] * (nxt - x)
    r = xs[:, 0:RW_W]
    k = xs[:, RW_W:2 * RW_W]
    v = xs[:, 2 * RW_W:3 * RW_W]
    i1 = 3 * RW_W
    wd = jnp.tanh(xs[:, i1:i1 + 2 * RW_LORA])
    ad = xs[:, i1 + 2 * RW_LORA:i1 + 4 * RW_LORA]
    gd = xs[:, i1 + 4 * RW_LORA:]
    cos, sin, bd = cos_ref[...], sin_ref[...], bd_ref[...]
    g_ref[0] = _dot3(jax.nn.sigmoid(gd), g2_ref)
    kk = k * kk_ref[...]
    nrm = jnp.sqrt(_dot01(kk * kk, bd))
    kk = kk / jnp.maximum(nrm, 1e-12)
    ksum = jnp.zeros_like(k)
    r_s = _rope(r, cos, sin)
    na_s = -_rope(kk, cos, sin)
    n_chunks = RW_TR // RW_C
    for d in range(2):
        logit = w0_ref[d:d + 1, :] + _dot3(wd, w2_ref, (d,))
        logw = -jax.nn.softplus(-logit) - 0.5
        lw = -jnp.exp(logw)
        a = jax.nn.sigmoid(a0_ref[d:d + 1, :] + _dot3(ad, a2_ref, (d,)))
        kdir = k * (1.0 + (a - 1.0) * ka_ref[...])
        ksum = ksum + kdir
        cum = _dot01(lw, tri_ref[d], const_left=True)
        e_neg = jnp.exp(-cum)
        at_ref[d, 0] = na_s * jnp.exp(cum - lw)
        rt_ref[d, 0] = r_s * jnp.exp(cum)
        bt_ref[d, 0] = _rope(kk * a, cos, sin) * e_neg
        kt_ref[d, 0] = _rope(kdir, cos, sin) * e_neg
        tot = jnp.sum(lw.reshape(n_chunks, RW_C, RW_W), axis=1, keepdims=True)
        wc_ref[d, 0] = jnp.broadcast_to(jnp.exp(tot), (n_chunks, 8, RW_W))
    bon_ref[0] = _dot01(r * ksum * rk_ref[...], bd)
    v_ref[0] = v


def _bmm3(spec, a, b):
    (ah, am), (bh, bm) = a, b
    e = functools.partial(jnp.einsum, spec, preferred_element_type=F32)
    return e(ah, bh) + e(am, bh) + e(ah, bm)


def _rw_scan_kernel(vf_ref, vb_ref, atf_ref, atb_ref, rtf_ref, rtb_ref, btf_ref, btb_ref, ktf_ref, ktb_ref,
                    wcf_ref, wcb_ref, yf_ref, yb_ref, s_sc):
    step = pl.program_id(0)
    nb = BATCH * RW_HEADS
    c = RW_C

    @pl.when(step == 0)
    def _():
        s_sc[...] = jnp.zeros_like(s_sc)

    def heads(xf, xb):
        return jnp.stack([x[b][:, h * RW_HD:(h + 1) * RW_HD]
                          for x in (xf, xb) for b in range(BATCH) for h in range(RW_HEADS)])

    v = heads(vf_ref[...], vb_ref[...])
    ar = jnp.concatenate([heads(atf_ref[0], atb_ref[0]), heads(rtf_ref[0], rtb_ref[0])], axis=1)
    bt = heads(btf_ref[0], btb_ref[0])
    kt = heads(ktf_ref[0], ktb_ref[0])
    wc = heads(wcf_ref[0, :, 0], wcb_ref[0, :, 0])[:, 0:1, :]

    ti = lax.broadcasted_iota(jnp.int32, (c, c), 0)
    si = lax.broadcasted_iota(jnp.int32, (c, c), 1)
    sign = jnp.where(lax.broadcasted_iota(jnp.int32, (2 * nb, 1, 1), 0) < nb, 1, -1)
    ahead = (ti - si)[None] * sign
    strict = ahead > 0
    incl = ahead >= 0
    eye = (si == ti).astype(F32)[None]

    ar_s, bt_s, kt_s, v_s = _split(ar), _split(bt), _split(kt), _split(v)
    lb = _bmm3('htj,hsj->hts', ar_s, bt_s)
    lk = _bmm3('htj,hsj->hts', ar_s, kt_s)
    lab = jnp.where(strict, lb[:, :c], 0.0)
    lrb = jnp.where(incl, lb[:, c:], 0.0)
    lak = jnp.where(strict, lk[:, :c], 0.0)
    lrk = jnp.where(incl, lk[:, c:], 0.0)

    tt = eye + lab
    pw = lab
    for _ in range(int(math.log2(c)) - 1):
        pw_s = _split(pw)
        pw = _bmm3('hts,hsu->htu', pw_s, pw_s)
        tt = tt + _bmm3('hts,hsu->htu', _split(tt), _split(pw))

    s0 = s_sc[...]
    ars = _bmm3('htk,hvk->htv', ar_s, _split(s0))
    lv = _bmm3('hts,hsv->htv', _split(jnp.concatenate([lak, lrk], axis=1)), v_s)
    u = _bmm3('hts,hsv->htv', _split(tt), _split(lv[:, :c] + ars[:, :c]))
    y = ars[:, c:] + _bmm3('hts,hsv->htv', _split(lrb), _split(u)) + lv[:, c:]
    for d, y_ref in enumerate((yf_ref, yb_ref)):
        for b in range(BATCH):
            y_ref[b] = jnp.concatenate([y[(d * BATCH + b) * RW_HEADS + h] for h in range(RW_HEADS)], axis=-1)
    uv = jnp.concatenate([u, v], axis=1)
    bk = jnp.concatenate([bt, kt], axis=1)
    s_sc[...] = (s0 + _bmm3('htv,htj->hvj', _split(uv), _split(bk))) * wc


def _rw_out_kernel(yf_ref, yb_ref, bon_ref, v_ref, g_ref, lnw_ref, lnb_ref, bd_ref, o_ref):
    y = yf_ref[0] + yb_ref[0]
    bd = bd_ref[...]
    mean = _dot01(y, bd) * (1.0 / RW_HD)
    yc = y - mean
    var = _dot01(yc * yc, bd) * (1.0 / RW_HD)
    yn = yc * lax.rsqrt(var + RW_GN_EPS) * lnw_ref[...] + lnb_ref[...]
    o_ref[0] = (yn + bon_ref[0] * v_ref[0]) * g_ref[0]


def _rw_chunk_index(d, s):
    n_lat = SEQ // RW_C
    n_all = T_ALL // RW_C
    fwd = jnp.where(s < n_all - n_lat, n_lat + s, s - (n_all - n_lat))
    return jnp.where(d == 0, fwd, n_all - 1 - s)


def _rwkv(p_rw, mu, w0, w2, a0, a2, g2, k_k, k_a, r_k, ln_w, ln_b):
    cos, sin = (jnp.asarray(a) for a in _rope_tables())
    bd = jnp.asarray(_head_blockdiag()).astype(BF16)
    zpad = lambda w: jnp.stack([jnp.concatenate([w[0], jnp.zeros_like(w[0])], axis=0),
                                jnp.concatenate([jnp.zeros_like(w[1]), w[1]], axis=0)])
    split = lambda w: jnp.stack(_split(w), axis=1)
    nt = T_ALL // RW_TR
    n_ch = T_ALL // RW_C
    tok_shape = jax.ShapeDtypeStruct((BATCH, T_ALL, RW_W), F32)
    dir_shape = jax.ShapeDtypeStruct((2, BATCH, T_ALL, RW_W), F32)
    dir_spec = pl.BlockSpec((2, 1, RW_TR, RW_W), lambda b, i: (0, b, i, 0))
    wc_shape = jax.ShapeDtypeStruct((2, BATCH, n_ch, 8, RW_W), F32)
    wc_spec = pl.BlockSpec((2, 1, RW_TR // RW_C, 8, RW_W), lambda b, i: (0, b, i, 0, 0))
    tok = lambda w: pl.BlockSpec((1, RW_TR, w), lambda b, i: (b, i, 0))
    const = lambda shape: pl.BlockSpec(shape, lambda b, i: (0,) * len(shape))
    row = lambda a: a.reshape(1, RW_W)
    outs = pl.pallas_call(
        _rw_prep_kernel,
        grid=(BATCH, nt),
        in_specs=[tok(RW_IN),
                  pl.BlockSpec((1, 8, RW_IN), lambda b, i: (b, jnp.maximum(i * (RW_TR // 8) - 1, 0), 0)),
                  pl.BlockSpec((1, 8, RW_IN), lambda b, i: (b, jnp.minimum((i + 1) * (RW_TR // 8), T_ALL // 8 - 1), 0)),
                  const((2, RW_IN)), const((2, RW_W)), const((2, 2, 2 * RW_LORA, RW_W)), const((2, RW_W)),
                  const((2, 2, 2 * RW_LORA, RW_W)), const((2, RW_GATE_LORA, RW_W)),
                  const((1, RW_W)), const((1, RW_W)), const((1, RW_W)),
                  pl.BlockSpec((RW_TR, RW_W), lambda b, i: (i, 0)), pl.BlockSpec((RW_TR, RW_W), lambda b, i: (i, 0)),
                  const((RW_W, RW_W)), const((2, RW_TR, RW_TR))],
        out_specs=[tok(RW_W)] * 3 + [dir_spec] * 4 + [wc_spec],
        out_shape=[tok_shape] * 3 + [dir_shape] * 4 + [wc_shape],
        compiler_params=_cparams(("parallel", "parallel")),
        name="rwkv_prepare",
    )(p_rw, p_rw, p_rw, mu, w0, split(zpad(w2)), a0, split(zpad(a2)), jnp.stack(_split(g2)),
      row(k_k), row(k_a), row(r_k.reshape(-1)), cos, sin, bd, jnp.asarray(_chunk_tri()).astype(BF16))
    v, bonus, g, at, rt, bt, kt, wc = outs

    shared = lambda d: pl.BlockSpec((BATCH, RW_C, RW_W), lambda s: (0, _rw_chunk_index(d, s), 0))
    perdir = lambda d: pl.BlockSpec((1, BATCH, RW_C, RW_W), lambda s: (d, 0, _rw_chunk_index(d, s), 0))
    wcdir = lambda d: pl.BlockSpec((1, BATCH, 1, 8, RW_W), lambda s: (d, 0, _rw_chunk_index(d, s), 0, 0))
    yf, yb = pl.pallas_call(
        _rw_scan_kernel,
        grid=(n_ch,),
        in_specs=[shared(0), shared(1)] + [perdir(0), perdir(1)] * 4 + [wcdir(0), wcdir(1)],
        out_specs=[shared(0), shared(1)],
        out_shape=[tok_shape, tok_shape],
        scratch_shapes=[pltpu.VMEM((2 * BATCH * RW_HEADS, RW_HD, RW_HD), F32)],
        compiler_params=_cparams(("arbitrary",)),
        name="rwkv_chunk_scan",
    )(v, v, at, at, rt, rt, bt, bt, kt, kt, wc, wc)

    tok2 = lambda w: pl.BlockSpec((1, PROJ_TM, w), lambda b, i: (b, i, 0))
    return pl.pallas_call(
        _rw_out_kernel,
        grid=(BATCH, T_ALL // PROJ_TM),
        in_specs=[tok2(RW_W), tok2(RW_W), tok2(RW_W), tok2(RW_W), tok2(RW_W),
                  const((1, RW_W)), const((1, RW_W)), const((RW_W, RW_W))],
        out_specs=tok2(RW_W),
        out_shape=jax.ShapeDtypeStruct((BATCH, T_ALL, RW_W), F32),
        compiler_params=_cparams(("parallel", "parallel")),
        name="rwkv_output",
    )(yf, yb, bonus, v, g, row(ln_w), row(ln_b), bd)


@functools.lru_cache(maxsize=None)
def _na_bias_tables():
    qc = np.arange(GRID_W)[:, None]
    kc = np.arange(GRID_W)[None, :]
    win0 = np.clip(qc - WIN_COLS // 2, 0, GRID_W - WIN_COLS)
    valid = (kc >= win0) & (kc < win0 + WIN_COLS)
    off = np.clip(kc - qc + WIN_COLS - 1, 0, 2 * WIN_COLS - 2)
    onehot = np.zeros((128, GRID_W * GRID_W), np.float32)
    onehot[off.reshape(-1), np.arange(GRID_W * GRID_W)] = valid.reshape(-1)
    mask = np.where(valid, 0.0, NEG_INF).astype(np.float32).reshape(1, GRID_W * GRID_W)
    return onehot, mask


def _na_bias_kernel(rpb_ref, oh_ref, mask_ref, o_ref):
    o_ref[...] = _dot(rpb_ref[...], oh_ref[...]) + mask_ref[...]


def _na_lat_kernel(q_ref, k_ref, v_ref, tp_ref, o_ref):
    for g in range(NA_G):
        _na_lat_row(pl.program_id(1) * NA_G + g, g, q_ref, k_ref, v_ref, tp_ref, o_ref)


def _na_lat_row(i, g, q_ref, k_ref, v_ref, tp_ref, o_ref):
    start = jnp.clip(i - WIN_ROWS // 2, 0, N_ROWS - WIN_ROWS)
    dr0 = start - i + WIN_ROWS - 1
    scale = NA_HD ** -0.5
    lane = lax.broadcasted_iota(jnp.int32, (1, NA_W), 1)
    hmask = [(lane // NA_HD) == h for h in range(NA_HEADS)]
    q = q_ref[0, g * GRID_W:(g + 1) * GRID_W, :]
    qm = jnp.concatenate([jnp.where(hmask[h], q, jnp.zeros_like(q)) for h in range(NA_HEADS)], axis=0)
    row0 = pl.multiple_of(start * GRID_W, GRID_W)
    kl = k_ref[0, pl.ds(row0, WIN_ROWS * GRID_W), :]
    vl = v_ref[0, pl.ds(row0, WIN_ROWS * GRID_W), :]
    kc = k_ref[0, SEQ:T_ALL, :]
    vc = v_ref[0, SEQ:T_ALL, :]
    nt = (((1,), (1,)), ((), ()))
    bias = jnp.concatenate(
        [jnp.concatenate([tp_ref[h, pl.ds(dr0 + 2 * j, 1)][0] for j in range(WIN_ROWS // 2)], axis=1)
         for h in range(NA_HEADS)], axis=0)
    s_loc = lax.dot_general(qm, kl, nt, preferred_element_type=F32) * scale + bias
    s_ctx = lax.dot_general(qm, kc, nt, preferred_element_type=F32) * scale
    m = jnp.maximum(jnp.max(s_loc, axis=-1, keepdims=True), jnp.max(s_ctx, axis=-1, keepdims=True))
    p_loc = jnp.exp(s_loc - m)
    p_ctx = jnp.exp(s_ctx - m)
    inv = 1.0 / (jnp.sum(p_loc, axis=-1, keepdims=True) + jnp.sum(p_ctx, axis=-1, keepdims=True))
    o_all = (jnp.dot((p_loc * inv).astype(BF16), vl, preferred_element_type=F32)
             + jnp.dot((p_ctx * inv).astype(BF16), vc, preferred_element_type=F32))
    o = jnp.zeros((GRID_W, NA_W), F32)
    for h in range(NA_HEADS):
        o = o + jnp.where(hmask[h], o_all[h * GRID_W:(h + 1) * GRID_W, :], 0.0)
    o_ref[0, g * GRID_W:(g + 1) * GRID_W, :] = o


def _na_ctx_kernel(q_ref, k_ref, v_ref, o_ref):
    scale = NA_HD ** -0.5
    lane = lax.broadcasted_iota(jnp.int32, (1, NA_W), 1)
    hmask = [(lane // NA_HD) == h for h in range(NA_HEADS)]
    q, k, v = q_ref[0], k_ref[0], v_ref[0]
    qm = jnp.concatenate([jnp.where(hmask[h], q, jnp.zeros_like(q)) for h in range(NA_HEADS)], axis=0)
    s = lax.dot_general(qm, k, (((1,), (1,)), ((), ())), preferred_element_type=F32) * scale
    p = jnp.exp(s - jnp.max(s, axis=-1, keepdims=True))
    p = p / jnp.sum(p, axis=-1, keepdims=True)
    o_all = jnp.dot(p.astype(BF16), v, preferred_element_type=F32)
    o = jnp.zeros((CTX_LEN, NA_W), F32)
    for h in range(NA_HEADS):
        o = o + jnp.where(hmask[h], o_all[h * CTX_LEN:(h + 1) * CTX_LEN, :], 0.0)
    o_ref[0] = o


def _natten(p_na, rpb):
    onehot, mask = (jnp.asarray(a) for a in _na_bias_tables())
    n_dr = 2 * WIN_ROWS - 1
    n_rel = 2 * WIN_COLS - 1
    rows = NA_HEADS * n_dr
    rows_p = -(-rows // 8) * 8
    rpb2 = jnp.zeros((rows_p, 128), F32).at[:rows, :n_rel].set(rpb.reshape(rows, n_rel))
    tb = pl.pallas_call(
        _na_bias_kernel,
        out_shape=jax.ShapeDtypeStruct((rows_p, GRID_W * GRID_W), F32),
        name="natten_bias_expand",
    )(rpb2, onehot, mask)
    tb = tb[:rows].reshape(NA_HEADS, n_dr, GRID_W, GRID_W)
    nxt = jnp.concatenate([tb[:, 1:], jnp.full_like(tb[:, :1], NEG_INF)], axis=1)
    tp = jnp.concatenate([tb, nxt], axis=-1)

    o_lat = pl.pallas_call(
        _na_lat_kernel,
        grid=(BATCH, N_ROWS // NA_G),
        in_specs=[pl.BlockSpec((1, NA_G * GRID_W, NA_W), lambda b, i: (b, i, 0)),
                  pl.BlockSpec((1, T_ALL, NA_W), lambda b, i: (b, 0, 1)),
                  pl.BlockSpec((1, T_ALL, NA_W), lambda b, i: (b, 0, 2)),
                  pl.BlockSpec((NA_HEADS, n_dr, GRID_W, 2 * GRID_W), lambda b, i: (0, 0, 0, 0))],
        out_specs=pl.BlockSpec((1, NA_G * GRID_W, NA_W), lambda b, i: (b, i, 0)),
        out_shape=jax.ShapeDtypeStruct((BATCH, SEQ, NA_W), F32),
        compiler_params=_cparams(("parallel", "parallel")),
        name="natten_latent",
    )(p_na, p_na, p_na, tp)

    cblk = SEQ // CTX_LEN
    o_ctx = pl.pallas_call(
        _na_ctx_kernel,
        grid=(BATCH,),
        in_specs=[pl.BlockSpec((1, CTX_LEN, NA_W), lambda b: (b, cblk, 0)),
                  pl.BlockSpec((1, CTX_LEN, NA_W), lambda b: (b, cblk, 1)),
                  pl.BlockSpec((1, CTX_LEN, NA_W), lambda b: (b, cblk, 2))],
        out_specs=pl.BlockSpec((1, CTX_LEN, NA_W), lambda b: (b, 0, 0)),
        out_shape=jax.ShapeDtypeStruct((BATCH, CTX_LEN, NA_W), F32),
        compiler_params=_cparams(("parallel",)),
        name="natten_context",
    )(p_na, p_na, p_na)
    return jnp.concatenate([o_lat, o_ctx], axis=1)


def kernel(x, c, ctx, c_ctx, mod_w, mod_b, norm_g, ffn1_wgu, ffn1_wdn, ffn2_wgu, ffn2_wdn, w_in, w_out, hy_conv_w, hy_conv_b, hy_f_w1, hy_f_b1, hy_f_w2, hy_f_b2, hy_f_w3, hy_freq, hy_bias, rw_mu, rw_w0, rw_w2, rw_a0, rw_a2, rw_g2, rw_k_k, rw_k_a, rw_r_k, rw_ln_w, rw_ln_b, na_rpb):
    assert x.shape == (BATCH, SEQ, D_MODEL) and ctx.shape == (BATCH, CTX_LEN, D_MODEL)
    mods = _modulation(c, c_ctx, mod_w, mod_b)
    h = jnp.concatenate([x, ctx], axis=1)
    gpad = jnp.concatenate([norm_g, jnp.zeros((DEPTH, 2, D_MODEL), F32)], axis=1)
    for l in range(DEPTH):
        m_l, g_l = mods[l], gpad[l]
        h = _ffn(h, m_l, g_l, ffn1_wgu[l].astype(BF16), ffn1_wdn[l].astype(BF16), 0)
        p_hy, p_rw, p_na = _in_proj(h, m_l, g_l, w_in[l].astype(BF16))
        o_hy = _hyena(p_hy, hy_conv_w[l], hy_conv_b[l], hy_f_w1[l], hy_f_b1[l], hy_f_w2[l], hy_f_b2[l],
                      hy_f_w3[l], hy_freq[l], hy_bias[l])
        o_rw = _rwkv(p_rw, rw_mu[l], rw_w0[l], rw_w2[l], rw_a0[l], rw_a2[l], rw_g2[l], rw_k_k[l], rw_k_a[l],
                     rw_r_k[l], rw_ln_w[l], rw_ln_b[l])
        o_na = _natten(p_na, na_rpb[l])
        h = _out_proj(h, o_hy, o_rw, o_na, m_l, g_l, w_out[l].astype(BF16))
        h = _ffn(h, m_l, g_l, ffn2_wgu[l].astype(BF16), ffn2_wdn[l].astype(BF16), 1)
    return h[:, :SEQ]
```

```python
import functools
import math

import numpy as np
import jax
import jax.numpy as jnp
from jax import lax
from jax.experimental import pallas as pl
from jax.experimental.pallas import tpu as pltpu

D_MODEL = 1024
BATCH = 2
SEQ = 8192
DEPTH = 4
GRID_W = 64
CTX_LEN = 256
T_ALL = SEQ + CTX_LEN
N_MOD = 9
D_FF = 2816
MACARON_W = 0.5
NORM_EPS = 1e-6
NEG_INF = -1e30

HY_CH = 256
HY_ORDER = 2
HY_EMB = 33
HY_FILT = 64
HY_TARGET = 1e-2
HY_FAST = 0.3
HY_SLOW = 1.5
HY_IN = (HY_ORDER + 1) * HY_CH

RW_HEADS = 6
RW_HD = 64
RW_W = RW_HEADS * RW_HD
RW_LORA = 64
RW_GATE_LORA = 128
RW_GN_EPS = 64e-5
ROPE_BASE = 10000.0
RW_IN = 3 * RW_W + 4 * RW_LORA + RW_GATE_LORA

NA_HEADS = 6
NA_HD = 64
NA_W = NA_HEADS * NA_HD
WIN_ROWS = 8
WIN_COLS = 16
NA_IN = 3 * NA_W
N_ROWS = SEQ // GRID_W

F32 = jnp.float32
BF16 = jnp.bfloat16
HI = lax.Precision.HIGHEST

VMEM_LIMIT = 56 * 1024 * 1024

FFN_TM = 768
FFN_TF = 1408
FFN_SUB = 2
PROJ_TM = 768
PROJ_SUB = 2
RW_TR = 256
RW_C = 64
FFT_N1 = 128
FFT_N2 = 128
FFT_CT = 16
NA_G = 4


def _dot(a, b):
    return jnp.dot(a, b, precision=HI, preferred_element_type=F32)


def _split(x):
    hi = x.astype(BF16)
    return hi, (x - hi.astype(F32)).astype(BF16)


def _dot3(x, w_ref, idx=()):
    xh, xm = _split(x)
    wh, wm = w_ref[idx + (0,)], w_ref[idx + (1,)]
    d = functools.partial(jnp.dot, preferred_element_type=F32)
    return d(xh, wh) + d(xm, wh) + d(xh, wm)


def _split3(x):
    hi = x.astype(BF16)
    r1 = x - hi.astype(F32)
    mid = r1.astype(BF16)
    return hi, mid, (r1 - mid.astype(F32)).astype(BF16)


def _dot01(x, m, const_left=False):
    d = functools.partial(jnp.dot, preferred_element_type=F32)
    parts = _split3(x)
    if const_left:
        return d(m, parts[0]) + d(m, parts[1]) + d(m, parts[2])
    return d(parts[0], m) + d(parts[1], m) + d(parts[2], m)


def _cparams(sem, vmem=VMEM_LIMIT):
    return pltpu.CompilerParams(dimension_semantics=sem, vmem_limit_bytes=vmem)


def _mod_kernel(cv_ref, w_ref, b_ref, o_ref):
    cv = cv_ref[...]
    s = cv * jax.nn.sigmoid(cv)
    o_ref[0] = _dot(s, w_ref[0]) + b_ref[0]


def _modulation(c, c_ctx, mod_w, mod_b):
    cv = jnp.concatenate([c, c_ctx[None, :], jnp.zeros((8 - BATCH - 1, D_MODEL), F32)], axis=0)
    tn = 1536
    return pl.pallas_call(
        _mod_kernel,
        grid=(DEPTH, N_MOD * D_MODEL // tn),
        in_specs=[pl.BlockSpec((8, D_MODEL), lambda l, j: (0, 0)),
                  pl.BlockSpec((1, D_MODEL, tn), lambda l, j: (l, 0, j)),
                  pl.BlockSpec((1, 1, tn), lambda l, j: (l, 0, j))],
        out_specs=pl.BlockSpec((1, 8, tn), lambda l, j: (l, 0, j)),
        out_shape=jax.ShapeDtypeStruct((DEPTH, 8, N_MOD * D_MODEL), F32),
        compiler_params=_cparams(("parallel", "parallel")),
        name="adaln_modulation",
    )(cv, mod_w, mod_b.reshape(DEPTH, 1, N_MOD * D_MODEL))


def _mod_rows(m_ref, b, i, tm, idx, sub=0, n_sub=1):
    n = tm // n_sub
    t = i * tm + sub * n + lax.broadcasted_iota(jnp.int32, (n, 1), 0)
    ml = m_ref[pl.ds(b, 1), idx * D_MODEL:(idx + 1) * D_MODEL]
    mc = m_ref[BATCH:BATCH + 1, idx * D_MODEL:(idx + 1) * D_MODEL]
    return jnp.where(t >= SEQ, mc, ml)


def _rms(x, g):
    return x * lax.rsqrt(jnp.mean(x * x, axis=-1, keepdims=True) + NORM_EPS) * g


def _ffn_kernel(h_ref, m_ref, g_ref, wgu_ref, wdn_ref, o_ref, *, mi, gi):
    b, i = pl.program_id(0), pl.program_id(1)
    n = FFN_TM // FFN_SUB
    for sub in range(FFN_SUB):
        mod = functools.partial(_mod_rows, m_ref, b, i, FFN_TM, sub=sub, n_sub=FFN_SUB)
        h = h_ref[0, sub * n:(sub + 1) * n, :]
        u = (_rms(h, g_ref[gi:gi + 1, :]) * (1.0 + mod(mi + 1)) + mod(mi)).astype(BF16)
        acc = jnp.zeros((n, D_MODEL), F32)
        for k in range(D_FF // FFN_TF):
            gate = jnp.dot(u, wgu_ref[:, k * FFN_TF:(k + 1) * FFN_TF], preferred_element_type=F32)
            up = jnp.dot(u, wgu_ref[:, D_FF + k * FFN_TF:D_FF + (k + 1) * FFN_TF], preferred_element_type=F32)
            a = (gate * jax.nn.sigmoid(gate) * up).astype(BF16)
            acc = acc + jnp.dot(a, wdn_ref[k * FFN_TF:(k + 1) * FFN_TF, :], preferred_element_type=F32)
        y = _rms(acc, g_ref[gi + 1:gi + 2, :])
        o_ref[0, sub * n:(sub + 1) * n, :] = h + MACARON_W * mod(mi + 2) * y


def _ffn(h, mods_l, g_l, wgu, wdn, which):
    mi, gi = (0, 0) if which == 0 else (6, 4)
    resident = pl.Buffered(1)
    return pl.pallas_call(
        functools.partial(_ffn_kernel, mi=mi, gi=gi),
        grid=(BATCH, T_ALL // FFN_TM),
        in_specs=[pl.BlockSpec((1, FFN_TM, D_MODEL), lambda b, i: (b, i, 0)),
                  pl.BlockSpec((8, N_MOD * D_MODEL), lambda b, i: (0, 0)),
                  pl.BlockSpec((8, D_MODEL), lambda b, i: (0, 0)),
                  pl.BlockSpec((D_MODEL, 2 * D_FF), lambda b, i: (0, 0), pipeline_mode=resident),
                  pl.BlockSpec((D_FF, D_MODEL), lambda b, i: (0, 0), pipeline_mode=resident)],
        out_specs=pl.BlockSpec((1, FFN_TM, D_MODEL), lambda b, i: (b, i, 0)),
        out_shape=jax.ShapeDtypeStruct((BATCH, T_ALL, D_MODEL), F32),
        compiler_params=_cparams(("parallel", "parallel")),
        name="ffn_sublayer",
    )(h, mods_l, g_l, wgu, wdn)


def _inproj_kernel(h_ref, m_ref, g_ref, w_ref, hy_ref, rw_ref, na_ref):
    b, i = pl.program_id(0), pl.program_id(1)
    n = PROJ_TM // PROJ_SUB
    for sub in range(PROJ_SUB):
        rows = slice(sub * n, (sub + 1) * n)
        mod = functools.partial(_mod_rows, m_ref, b, i, PROJ_TM, sub=sub, n_sub=PROJ_SUB)
        u = _rms(h_ref[0, rows, :], g_ref[2:3, :]) * (1.0 + mod(4)) + mod(3)
        p = jnp.dot(u.astype(BF16), w_ref[...], preferred_element_type=F32)
        hy_ref[0, rows, :] = p[:, :HY_IN]
        rw_ref[0, rows, :] = p[:, HY_IN:HY_IN + RW_IN]
        na_ref[0, rows, :] = p[:, HY_IN + RW_IN:].astype(BF16)


def _in_proj(h, mods_l, g_l, w_in):
    n_in = HY_IN + RW_IN + NA_IN
    return pl.pallas_call(
        _inproj_kernel,
        grid=(BATCH, T_ALL // PROJ_TM),
        in_specs=[pl.BlockSpec((1, PROJ_TM, D_MODEL), lambda b, i: (b, i, 0)),
                  pl.BlockSpec((8, N_MOD * D_MODEL), lambda b, i: (0, 0)),
                  pl.BlockSpec((8, D_MODEL), lambda b, i: (0, 0)),
                  pl.BlockSpec((D_MODEL, n_in), lambda b, i: (0, 0), pipeline_mode=pl.Buffered(1))],
        out_specs=[pl.BlockSpec((1, PROJ_TM, HY_IN), lambda b, i: (b, i, 0)),
                   pl.BlockSpec((1, PROJ_TM, RW_IN), lambda b, i: (b, i, 0)),
                   pl.BlockSpec((1, PROJ_TM, NA_IN), lambda b, i: (b, i, 0))],
        out_shape=[jax.ShapeDtypeStruct((BATCH, T_ALL, HY_IN), F32),
                   jax.ShapeDtypeStruct((BATCH, T_ALL, RW_IN), F32),
                   jax.ShapeDtypeStruct((BATCH, T_ALL, NA_IN), BF16)],
        compiler_params=_cparams(("parallel", "parallel")),
        name="mixer_in_proj",
    )(h, mods_l, g_l, w_in)


def _outproj_kernel(h_ref, hy_ref, rw_ref, na_ref, m_ref, g_ref, w_ref, o_ref):
    b, i = pl.program_id(0), pl.program_id(1)
    n = PROJ_TM // PROJ_SUB
    for sub in range(PROJ_SUB):
        rows = slice(sub * n, (sub + 1) * n)
        cat = jnp.concatenate([hy_ref[0, rows, :], rw_ref[0, rows, :], na_ref[0, rows, :]], axis=-1).astype(BF16)
        o = jnp.dot(cat, w_ref[...], preferred_element_type=F32)
        gate = _mod_rows(m_ref, b, i, PROJ_TM, 5, sub=sub, n_sub=PROJ_SUB)
        o_ref[0, rows, :] = h_ref[0, rows, :] + gate * _rms(o, g_ref[3:4, :])


def _out_proj(h, o_hy, o_rw, o_na, mods_l, g_l, w_out):
    return pl.pallas_call(
        _outproj_kernel,
        grid=(BATCH, T_ALL // PROJ_TM),
        in_specs=[pl.BlockSpec((1, PROJ_TM, D_MODEL), lambda b, i: (b, i, 0)),
                  pl.BlockSpec((1, PROJ_TM, HY_CH), lambda b, i: (b, i, 0)),
                  pl.BlockSpec((1, PROJ_TM, RW_W), lambda b, i: (b, i, 0)),
                  pl.BlockSpec((1, PROJ_TM, NA_W), lambda b, i: (b, i, 0)),
                  pl.BlockSpec((8, N_MOD * D_MODEL), lambda b, i: (0, 0)),
                  pl.BlockSpec((8, D_MODEL), lambda b, i: (0, 0)),
                  pl.BlockSpec((D_MODEL, D_MODEL), lambda b, i: (0, 0), pipeline_mode=pl.Buffered(1))],
        out_specs=pl.BlockSpec((1, PROJ_TM, D_MODEL), lambda b, i: (b, i, 0)),
        out_shape=jax.ShapeDtypeStruct((BATCH, T_ALL, D_MODEL), F32),
        compiler_params=_cparams(("parallel", "parallel")),
        name="mixer_out_proj",
    )(h, o_hy, o_rw, o_na, mods_l, g_l, w_out)


def _hy_short_kernel(x_ref, w_ref, b_ref, o_ref):
    x = x_ref[0]
    n = x.shape[0]
    row = lax.broadcasted_iota(jnp.int32, (n, 1), 0)
    prev = jnp.where(row == 0, 0.0, pltpu.roll(x, 1, 0))
    nxt = jnp.where(row == n - 1, 0.0, pltpu.roll(x, n - 1, 0))
    o_ref[0] = prev * w_ref[0:1, :] + x * w_ref[1:2, :] + nxt * w_ref[2:3, :] + b_ref[...]


def _hy_short(p_hy, conv_w, conv_b, seq_len, row_block):
    lanes = 128
    return pl.pallas_call(
        _hy_short_kernel,
        grid=(BATCH, HY_IN // lanes),
        in_specs=[pl.BlockSpec((1, seq_len, lanes), lambda b, j: (b, row_block, j)),
                  pl.BlockSpec((3, lanes), lambda b, j: (0, j)),
                  pl.BlockSpec((1, lanes), lambda b, j: (0, j))],
        out_specs=pl.BlockSpec((1, seq_len, lanes), lambda b, j: (b, 0, j)),
        out_shape=jax.ShapeDtypeStruct((BATCH, seq_len, HY_IN), F32),
        compiler_params=_cparams(("parallel", "parallel")),
        name="hyena_short_conv",
    )(p_hy, conv_w, conv_b.reshape(1, HY_IN))


@functools.lru_cache(maxsize=None)
def _filter_features(seq_len):
    n = 2 * seq_len
    m = np.arange(n)
    d = np.where(m < seq_len, m, n - m).astype(np.float64)
    t = d / (seq_len - 1)
    ang = 2.0 * math.pi * d / seq_len
    bands = (HY_EMB - 1) // 2
    fr = np.linspace(1e-4, bands - 1, bands)
    z = np.zeros((n, 128), np.float64)
    z[:, 0] = t
    z[:, 1:1 + bands] = np.cos(fr[None, :] * ang[:, None])
    z[:, 1 + bands:1 + 2 * bands] = -np.sin(fr[None, :] * ang[:, None])
    z[:, 33] = (m < seq_len)
    z[:, 34] = (m > seq_len)
    z[seq_len, :] = 0.0
    return z.astype(np.float32)


@functools.lru_cache(maxsize=None)
def _filter_deltas():
    d = np.abs(np.linspace(math.log(HY_TARGET) / HY_SLOW, math.log(HY_TARGET) / HY_FAST, HY_CH))
    return d.astype(np.float32).reshape(1, HY_CH)


def _hy_filter_kernel(z_ref, w1_ref, b1_ref, w2_ref, b2_ref, w3_ref, fq_ref, dl_ref, o_ref):
    z = z_ref[...]
    fq = fq_ref[...]
    h = jnp.sin(fq * (_dot(z, w1_ref[...]) + b1_ref[...]))
    h = jnp.sin(fq * (_dot(h, w2_ref[...]) + b2_ref[...]))
    o = _dot3(h, w3_ref)
    dec = jnp.exp(-z[:, 0:1] * dl_ref[...])
    mf = z[:, 33:34]
    mb = z[:, 34:35]
    outs = []
    for od in range(HY_ORDER):
        fwd = o[:, od * 2 * HY_CH:od * 2 * HY_CH + HY_CH]
        bwd = o[:, od * 2 * HY_CH + HY_CH:(od + 1) * 2 * HY_CH]
        outs.append((mf * fwd + mb * bwd) * dec)
    o_ref[...] = jnp.concatenate(outs, axis=1)


def _hy_filter(seq_len, w1, b1, w2, b2, w3, freq):
    n = 2 * seq_len
    tp = min(n, 1024)
    zt = jnp.asarray(_filter_features(seq_len))
    w1p = jnp.zeros((128, HY_FILT), F32).at[:HY_EMB].set(w1)
    full = lambda shape: pl.BlockSpec(shape, lambda i: (0,) * len(shape))
    return pl.pallas_call(
        _hy_filter_kernel,
        grid=(n // tp,),
        in_specs=[pl.BlockSpec((tp, 128), lambda i: (i, 0)),
                  full((128, HY_FILT)), full((1, HY_FILT)), full((HY_FILT, HY_FILT)), full((1, HY_FILT)),
                  full((2, HY_FILT, HY_ORDER * 2 * HY_CH)), full((1, HY_FILT)), full((1, HY_CH))],
        out_specs=pl.BlockSpec((tp, HY_ORDER * HY_CH), lambda i: (i, 0)),
        out_shape=jax.ShapeDtypeStruct((n, HY_ORDER * HY_CH), F32),
        compiler_params=_cparams(("parallel",)),
        name="hyena_filter_mlp",
    )(zt, w1p, b1.reshape(1, -1), w2, b2.reshape(1, -1), jnp.stack(_split(w3)), freq.reshape(1, -1),
      jnp.asarray(_filter_deltas()))


@functools.lru_cache(maxsize=None)
def _fft_tables():
    n1 = np.arange(FFT_N1)[:, None]
    k1 = np.arange(FFT_N1)[None, :]
    th = 2.0 * math.pi * n1 * k1 / FFT_N1
    c, s = np.cos(th), np.sin(th)
    half = FFT_N1 // 2
    big_n = FFT_N1 * FFT_N2
    wa_c = np.block([[c[:half], -s[:half]], [s[:half], c[:half]]])
    wa_r = np.concatenate([c, -s], axis=1)
    tw = 2.0 * math.pi * np.arange(FFT_N2)[:, None] * np.arange(FFT_N1)[None, :] / big_n
    twf = np.concatenate([np.cos(tw), -np.sin(tw)], axis=1)
    twi = np.concatenate([np.cos(tw.T), np.sin(tw.T)], axis=1)
    wb = np.block([[c, -s], [s, c]])
    wc = np.block([[c, s], [-s, c]])
    wd = np.block([[c[:, :half], s[:, :half]], [-s[:, :half], c[:, :half]]]) / big_n
    return tuple(np.asarray(a, np.float32) for a in (wa_c, wa_r, twf, twi, wb, wc, wd))


def _fft_fwd(z, wa_ref, tw, wb_ref):
    ct = z.shape[0]
    a = _dot3(z.reshape(ct * FFT_N2, z.shape[2]), wa_ref).reshape(ct, FFT_N2, 2 * FFT_N1)
    ar, ai = a[:, :, :FFT_N1], a[:, :, FFT_N1:]
    tr, ti = tw[:, :FFT_N1], tw[:, FFT_N1:]
    br = ar * tr - ai * ti
    bi = ar * ti + ai * tr
    bt = jnp.concatenate([jnp.swapaxes(br, 1, 2), jnp.swapaxes(bi, 1, 2)], axis=2)
    return _dot3(bt.reshape(ct * FFT_N1, 2 * FFT_N2), wb_ref).reshape(ct, FFT_N1, 2 * FFT_N2)


def _hy_kfft_kernel(k_ref, wa_ref, tw_ref, wb_ref, o_ref):
    o_ref[...] = _fft_fwd(k_ref[...], wa_ref, tw_ref[...], wb_ref)


def _hy_long_kernel(z_ref, g_ref, kh_ref, bias_ref, wa_ref, twf_ref, wb_ref, wc_ref, twi_ref, wd_ref, o_ref):
    z = z_ref[...]
    ct = z.shape[0]
    x = _fft_fwd(z, wa_ref, twf_ref[...], wb_ref)
    kh = kh_ref[...]
    xr, xi = x[:, :, :FFT_N2], x[:, :, FFT_N2:]
    kr, ki = kh[:, :, :FFT_N2], kh[:, :, FFT_N2:]
    y = jnp.concatenate([xr * kr - xi * ki, xr * ki + xi * kr], axis=2)
    c1 = _dot3(y.reshape(ct * FFT_N1, 2 * FFT_N2), wc_ref).reshape(ct, FFT_N1, 2 * FFT_N2)
    cr, ci = c1[:, :, :FFT_N2], c1[:, :, FFT_N2:]
    twi = twi_ref[...]
    tr, ti = twi[:, :FFT_N2], twi[:, FFT_N2:]
    dr = cr * tr - ci * ti
    di = cr * ti + ci * tr
    dt = jnp.concatenate([jnp.swapaxes(dr, 1, 2), jnp.swapaxes(di, 1, 2)], axis=2)
    conv = _dot3(dt.reshape(ct * FFT_N2, 2 * FFT_N1), wd_ref).reshape(ct, FFT_N2, FFT_N1)
    o_ref[...] = (conv + bias_ref[...] * z) * g_ref[...]


def _hy_ctx_kernel(x_ref, kc_ref, bias_ref, cf_ref, sf_ref, o_ref):
    x = x_ref[0]
    cf, sf = cf_ref[...], sf_ref[...]
    n = CTX_LEN
    z = x[:, :HY_CH]
    for od in range(HY_ORDER):
        k = kc_ref[:, od * HY_CH:(od + 1) * HY_CH]
        kr, ki = _dot(cf, k), -_dot(sf, k)
        xr, xi = _dot(cf[:, :n], z), -_dot(sf[:, :n], z)
        yr = xr * kr - xi * ki
        yi = xr * ki + xi * kr
        conv = (_dot(cf[:n, :], yr) - _dot(sf[:n, :], yi)) * (1.0 / (2 * n))
        z = x[:, (od + 1) * HY_CH:(od + 2) * HY_CH] * (conv + bias_ref[od:od + 1, :] * z)
    o_ref[0] = z


@functools.lru_cache(maxsize=None)
def _ctx_dft():
    m = np.arange(2 * CTX_LEN)
    th = 2.0 * math.pi * m[:, None] * m[None, :] / (2 * CTX_LEN)
    return np.cos(th).astype(np.float32), np.sin(th).astype(np.float32)


def _hyena(p_hy, conv_w, conv_b, w1, b1, w2, b2, w3, freq, bias):
    wa_c, wa_r, twf, twi, wb, wc, wd = (jnp.asarray(a) for a in _fft_tables())
    wa_c, wa_r, wb, wc, wd = (jnp.stack(_split(a)) for a in (wa_c, wa_r, wb, wc, wd))
    nch = HY_ORDER * HY_CH
    kf = _hy_filter(SEQ, w1, b1, w2, b2, w3, freq)
    kt = kf.reshape(FFT_N1, FFT_N2, nch).transpose(2, 1, 0)
    const = lambda shape: pl.BlockSpec(shape, lambda j: (0,) * len(shape))
    khat = pl.pallas_call(
        _hy_kfft_kernel,
        grid=(nch // FFT_CT,),
        in_specs=[pl.BlockSpec((FFT_CT, FFT_N2, FFT_N1), lambda j: (j, 0, 0)),
                  const((2, FFT_N1, 2 * FFT_N1)), const((FFT_N2, 2 * FFT_N1)), const((2, 2 * FFT_N2, 2 * FFT_N2))],
        out_specs=pl.BlockSpec((FFT_CT, FFT_N1, 2 * FFT_N2), lambda j: (j, 0, 0)),
        out_shape=jax.ShapeDtypeStruct((nch, FFT_N1, 2 * FFT_N2), F32),
        compiler_params=_cparams(("parallel",)),
        name="hyena_filter_fft",
    )(kt, wa_r, twf, wb)

    pc = _hy_short(p_hy, conv_w, conv_b, SEQ, 0)
    half = FFT_N1 // 2
    zall = pc.reshape(BATCH, half, FFT_N2, HY_IN).transpose(3, 2, 0, 1).reshape(HY_IN, FFT_N2, FFT_N1)
    nblk = HY_CH // FFT_CT
    z = zall
    for od in range(HY_ORDER):
        z_spec = pl.BlockSpec((FFT_CT, FFT_N2, FFT_N1), lambda j: (j, 0, 0))
        g_spec = pl.BlockSpec((FFT_CT, FFT_N2, FFT_N1), lambda j, od=od: ((od + 1) * nblk + j, 0, 0))
        z = pl.pallas_call(
            _hy_long_kernel,
            grid=(nblk,),
            in_specs=[z_spec, g_spec,
                      pl.BlockSpec((FFT_CT, FFT_N1, 2 * FFT_N2), lambda j, od=od: (od * nblk + j, 0, 0)),
                      pl.BlockSpec((FFT_CT, 1, 1), lambda j, od=od: (od * nblk + j, 0, 0)),
                      const((2, FFT_N1, 2 * FFT_N1)), const((FFT_N2, 2 * FFT_N1)), const((2, 2 * FFT_N2, 2 * FFT_N2)),
                      const((2, 2 * FFT_N2, 2 * FFT_N2)), const((FFT_N1, 2 * FFT_N2)), const((2, 2 * FFT_N1, FFT_N1))],
            out_specs=pl.BlockSpec((FFT_CT, FFT_N2, FFT_N1), lambda j: (j, 0, 0)),
            out_shape=jax.ShapeDtypeStruct((HY_CH, FFT_N2, FFT_N1), F32),
            compiler_params=_cparams(("parallel",)),
            name="hyena_long_conv",
        )(z, zall, khat, bias.reshape(nch, 1, 1), wa_c, twf, wb, wc, twi, wd)
    o_lat = z.reshape(HY_CH, FFT_N2, BATCH, half).transpose(2, 3, 1, 0).reshape(BATCH, SEQ, HY_CH)

    kc = _hy_filter(CTX_LEN, w1, b1, w2, b2, w3, freq)
    pcc = _hy_short(p_hy, conv_w, conv_b, CTX_LEN, SEQ // CTX_LEN)
    cf, sf = (jnp.asarray(a) for a in _ctx_dft())
    o_ctx = pl.pallas_call(
        _hy_ctx_kernel,
        grid=(BATCH,),
        in_specs=[pl.BlockSpec((1, CTX_LEN, HY_IN), lambda b: (b, 0, 0)),
                  pl.BlockSpec((2 * CTX_LEN, nch), lambda b: (0, 0)),
                  pl.BlockSpec((HY_ORDER, HY_CH), lambda b: (0, 0)),
                  pl.BlockSpec((2 * CTX_LEN, 2 * CTX_LEN), lambda b: (0, 0)),
                  pl.BlockSpec((2 * CTX_LEN, 2 * CTX_LEN), lambda b: (0, 0))],
        out_specs=pl.BlockSpec((1, CTX_LEN, HY_CH), lambda b: (b, 0, 0)),
        out_shape=jax.ShapeDtypeStruct((BATCH, CTX_LEN, HY_CH), F32),
        compiler_params=_cparams(("parallel",)),
        name="hyena_context",
    )(pcc, kc, bias, cf, sf)
    return jnp.concatenate([o_lat, o_ctx], axis=1)


@functools.lru_cache(maxsize=None)
def _rope_tables():
    nf = RW_HD // 4
    t = np.arange(SEQ)
    row = (t // GRID_W).astype(np.float64)
    col = (t % GRID_W).astype(np.float64)
    inv = ROPE_BASE ** (-np.arange(nf, dtype=np.float64) / nf)
    ang = np.concatenate([row[:, None] * inv] * 2 + [col[:, None] * inv] * 2, axis=-1)
    cos = np.concatenate([np.cos(ang), np.ones((CTX_LEN, RW_HD))], axis=0)
    sin = np.concatenate([np.sin(ang), np.zeros((CTX_LEN, RW_HD))], axis=0)
    return (np.tile(cos, (1, RW_HEADS)).astype(np.float32), np.tile(sin, (1, RW_HEADS)).astype(np.float32))


@functools.lru_cache(maxsize=None)
def _head_blockdiag():
    h = np.arange(RW_W) // RW_HD
    return (h[:, None] == h[None, :]).astype(np.float32)


@functools.lru_cache(maxsize=None)
def _chunk_tri():
    t = np.arange(RW_TR)[:, None]
    s = np.arange(RW_TR)[None, :]
    same = (t // RW_C) == (s // RW_C)
    return np.stack([same & (s <= t), same & (s >= t)]).astype(np.float32)


def _rope(x, cos, sin):
    lane = lax.broadcasted_iota(jnp.int32, (1, RW_W), 1)
    first = (lane % (RW_HD // 2)) < (RW_HD // 4)
    rot = jnp.where(first, -pltpu.roll(x, RW_W - RW_HD // 4, 1), pltpu.roll(x, RW_HD // 4, 1))
    return x * cos + rot * sin


def _rw_prep_kernel(p_ref, pv_ref, nx_ref, mu_ref, w0_ref, w2_ref, a0_ref, a2_ref, g2_ref, kk_ref, ka_ref, rk_ref,
                    cos_ref, sin_ref, bd_ref, tri_ref,
                    v_ref, bon_ref, g_ref, at_ref, rt_ref, bt_ref, kt_ref, wc_ref):
    i = pl.program_id(1)
    x = p_ref[0]
    row = lax.broadcasted_iota(jnp.int32, (RW_TR, 1), 0)
    t = i * RW_TR + row
    seq_first = (t == 0) | (t == SEQ)
    seq_last = (t == SEQ - 1) | (t == T_ALL - 1)
    prev = jnp.where(row == 0, pv_ref[0, 7:8, :], pltpu.roll(x, 1, 0))
    prev = jnp.where(seq_first, 0.0, prev)
    nxt = jnp.where(row == RW_TR - 1, nx_ref[0, 0:1, :], pltpu.roll(x, RW_TR - 1, 0))
    nxt = jnp.where(seq_last, 0.0, nxt)
    xs = x + mu_ref[0:1, :] * (prev - x) + mu_ref[1:2, :] * (nxt - x)
    r = xs[:, 0:RW_W]
    k = xs[:, RW_W:2 * RW_W]
    v = xs[:, 2 * RW_W:3 * RW_W]
    i1 = 3 * RW_W
    wd = jnp.tanh(xs[:, i1:i1 + 2 * RW_LORA])
    ad = xs[:, i1 + 2 * RW_LORA:i1 + 4 * RW_LORA]
    gd = xs[:, i1 + 4 * RW_LORA:]
    cos, sin, bd = cos_ref[...], sin_ref[...], bd_ref[...]
    g_ref[0] = _dot3(jax.nn.sigmoid(gd), g2_ref)
    kk = k * kk_ref[...]
    nrm = jnp.sqrt(_dot01(kk * kk, bd))
    kk = kk / jnp.maximum(nrm, 1e-12)
    ksum = jnp.zeros_like(k)
    r_s = _rope(r, cos, sin)
    na_s = -_rope(kk, cos, sin)
    n_chunks = RW_TR // RW_C
    for d in range(2):
        logit = w0_ref[d:d + 1, :] + _dot3(wd, w2_ref, (d,))
        logw = -jax.nn.softplus(-logit) - 0.5
        lw = -jnp.exp(logw)
        a = jax.nn.sigmoid(a0_ref[d:d + 1, :] + _dot3(ad, a2_ref, (d,)))
        kdir = k * (1.0 + (a - 1.0) * ka_ref[...])
        ksum = ksum + kdir
        cum = _dot01(lw, tri_ref[d], const_left=True)
        e_neg = jnp.exp(-cum)
        at_ref[d, 0] = na_s * jnp.exp(cum - lw)
        rt_ref[d, 0] = r_s * jnp.exp(cum)
        bt_ref[d, 0] = _rope(kk * a, cos, sin) * e_neg
        kt_ref[d, 0] = _rope(kdir, cos, sin) * e_neg
        tot = jnp.sum(lw.reshape(n_chunks, RW_C, RW_W), axis=1, keepdims=True)
        wc_ref[d, 0] = jnp.broadcast_to(jnp.exp(tot), (n_chunks, 8, RW_W))
    bon_ref[0] = _dot01(r * ksum * rk_ref[...], bd)
    v_ref[0] = v


def _bmm3(spec, a, b):
    (ah, am), (bh, bm) = a, b
    e = functools.partial(jnp.einsum, spec, preferred_element_type=F32)
    return e(ah, bh) + e(am, bh) + e(ah, bm)


def _rw_scan_kernel(vf_ref, vb_ref, atf_ref, atb_ref, rtf_ref, rtb_ref, btf_ref, btb_ref, ktf_ref, ktb_ref,
                    wcf_ref, wcb_ref, yf_ref, yb_ref, s_sc):
    step = pl.program_id(0)
    nb = BATCH * RW_HEADS
    c = RW_C

    @pl.when(step == 0)
    def _():
        s_sc[...] = jnp.zeros_like(s_sc)

    def heads(xf, xb):
        return jnp.stack([x[b][:, h * RW_HD:(h + 1) * RW_HD]
                          for x in (xf, xb) for b in range(BATCH) for h in range(RW_HEADS)])

    v = heads(vf_ref[...], vb_ref[...])
    ar = jnp.concatenate([heads(atf_ref[0], atb_ref[0]), heads(rtf_ref[0], rtb_ref[0])], axis=1)
    bt = heads(btf_ref[0], btb_ref[0])
    kt = heads(ktf_ref[0], ktb_ref[0])
    wc = heads(wcf_ref[0, :, 0], wcb_ref[0, :, 0])[:, 0:1, :]

    ti = lax.broadcasted_iota(jnp.int32, (c, c), 0)
    si = lax.broadcasted_iota(jnp.int32, (c, c), 1)
    sign = jnp.where(lax.broadcasted_iota(jnp.int32, (2 * nb, 1, 1), 0) < nb, 1, -1)
    ahead = (ti - si)[None] * sign
    strict = ahead > 0
    incl = ahead >= 0
    eye = (si == ti).astype(F32)[None]

    ar_s, bt_s, kt_s, v_s = _split(ar), _split(bt), _split(kt), _split(v)
    lb = _bmm3('htj,hsj->hts', ar_s, bt_s)
    lk = _bmm3('htj,hsj->hts', ar_s, kt_s)
    lab = jnp.where(strict, lb[:, :c], 0.0)
    lrb = jnp.where(incl, lb[:, c:], 0.0)
    lak = jnp.where(strict, lk[:, :c], 0.0)
    lrk = jnp.where(incl, lk[:, c:], 0.0)

    tt = eye + lab
    pw = lab
    for _ in range(int(math.log2(c)) - 1):
        pw_s = _split(pw)
        pw = _bmm3('hts,hsu->htu', pw_s, pw_s)
        tt = tt + _bmm3('hts,hsu->htu', _split(tt), _split(pw))

    s0 = s_sc[...]
    ars = _bmm3('htk,hvk->htv', ar_s, _split(s0))
    lv = _bmm3('hts,hsv->htv', _split(jnp.concatenate([lak, lrk], axis=1)), v_s)
    u = _bmm3('hts,hsv->htv', _split(tt), _split(lv[:, :c] + ars[:, :c]))
    y = ars[:, c:] + _bmm3('hts,hsv->htv', _split(lrb), _split(u)) + lv[:, c:]
    for d, y_ref in enumerate((yf_ref, yb_ref)):
        for b in range(BATCH):
            y_ref[b] = jnp.concatenate([y[(d * BATCH + b) * RW_HEADS + h] for h in range(RW_HEADS)], axis=-1)
    uv = jnp.concatenate([u, v], axis=1)
    bk = jnp.concatenate([bt, kt], axis=1)
    s_sc[...] = (s0 + _bmm3('htv,htj->hvj', _split(uv), _split(bk))) * wc


def _rw_out_kernel(yf_ref, yb_ref, bon_ref, v_ref, g_ref, lnw_ref, lnb_ref, bd_ref, o_ref):
    y = yf_ref[0] + yb_ref[0]
    bd = bd_ref[...]
    mean = _dot01(y, bd) * (1.0 / RW_HD)
    yc = y - mean
    var = _dot01(yc * yc, bd) * (1.0 / RW_HD)
    yn = yc * lax.rsqrt(var + RW_GN_EPS) * lnw_ref[...] + lnb_ref[...]
    o_ref[0] = (yn + bon_ref[0] * v_ref[0]) * g_ref[0]


def _rw_chunk_index(d, s):
    n_lat = SEQ // RW_C
    n_all = T_ALL // RW_C
    fwd = jnp.where(s < n_all - n_lat, n_lat + s, s - (n_all - n_lat))
    return jnp.where(d == 0, fwd, n_all - 1 - s)


def _rwkv(p_rw, mu, w0, w2, a0, a2, g2, k_k, k_a, r_k, ln_w, ln_b):
    cos, sin = (jnp.asarray(a) for a in _rope_tables())
    bd = jnp.asarray(_head_blockdiag()).astype(BF16)
    zpad = lambda w: jnp.stack([jnp.concatenate([w[0], jnp.zeros_like(w[0])], axis=0),
                                jnp.concatenate([jnp.zeros_like(w[1]), w[1]], axis=0)])
    split = lambda w: jnp.stack(_split(w), axis=1)
    nt = T_ALL // RW_TR
    n_ch = T_ALL // RW_C
    tok_shape = jax.ShapeDtypeStruct((BATCH, T_ALL, RW_W), F32)
    dir_shape = jax.ShapeDtypeStruct((2, BATCH, T_ALL, RW_W), F32)
    dir_spec = pl.BlockSpec((2, 1, RW_TR, RW_W), lambda b, i: (0, b, i, 0))
    wc_shape = jax.ShapeDtypeStruct((2, BATCH, n_ch, 8, RW_W), F32)
    wc_spec = pl.BlockSpec((2, 1, RW_TR // RW_C, 8, RW_W), lambda b, i: (0, b, i, 0, 0))
    tok = lambda w: pl.BlockSpec((1, RW_TR, w), lambda b, i: (b, i, 0))
    const = lambda shape: pl.BlockSpec(shape, lambda b, i: (0,) * len(shape))
    row = lambda a: a.reshape(1, RW_W)
    outs = pl.pallas_call(
        _rw_prep_kernel,
        grid=(BATCH, nt),
        in_specs=[tok(RW_IN),
                  pl.BlockSpec((1, 8, RW_IN), lambda b, i: (b, jnp.maximum(i * (RW_TR // 8) - 1, 0), 0)),
                  pl.BlockSpec((1, 8, RW_IN), lambda b, i: (b, jnp.minimum((i + 1) * (RW_TR // 8), T_ALL // 8 - 1), 0)),
                  const((2, RW_IN)), const((2, RW_W)), const((2, 2, 2 * RW_LORA, RW_W)), const((2, RW_W)),
                  const((2, 2, 2 * RW_LORA, RW_W)), const((2, RW_GATE_LORA, RW_W)),
                  const((1, RW_W)), const((1, RW_W)), const((1, RW_W)),
                  pl.BlockSpec((RW_TR, RW_W), lambda b, i: (i, 0)), pl.BlockSpec((RW_TR, RW_W), lambda b, i: (i, 0)),
                  const((RW_W, RW_W)), const((2, RW_TR, RW_TR))],
        out_specs=[tok(RW_W)] * 3 + [dir_spec] * 4 + [wc_spec],
        out_shape=[tok_shape] * 3 + [dir_shape] * 4 + [wc_shape],
        compiler_params=_cparams(("parallel", "parallel")),
        name="rwkv_prepare",
    )(p_rw, p_rw, p_rw, mu, w0, split(zpad(w2)), a0, split(zpad(a2)), jnp.stack(_split(g2)),
      row(k_k), row(k_a), row(r_k.reshape(-1)), cos, sin, bd, jnp.asarray(_chunk_tri()).astype(BF16))
    v, bonus, g, at, rt, bt, kt, wc = outs

    shared = lambda d: pl.BlockSpec((BATCH, RW_C, RW_W), lambda s: (0, _rw_chunk_index(d, s), 0))
    perdir = lambda d: pl.BlockSpec((1, BATCH, RW_C, RW_W), lambda s: (d, 0, _rw_chunk_index(d, s), 0))
    wcdir = lambda d: pl.BlockSpec((1, BATCH, 1, 8, RW_W), lambda s: (d, 0, _rw_chunk_index(d, s), 0, 0))
    yf, yb = pl.pallas_call(
        _rw_scan_kernel,
        grid=(n_ch,),
        in_specs=[shared(0), shared(1)] + [perdir(0), perdir(1)] * 4 + [wcdir(0), wcdir(1)],
        out_specs=[shared(0), shared(1)],
        out_shape=[tok_shape, tok_shape],
        scratch_shapes=[pltpu.VMEM((2 * BATCH * RW_HEADS, RW_HD, RW_HD), F32)],
        compiler_params=_cparams(("arbitrary",)),
        name="rwkv_chunk_scan",
    )(v, v, at, at, rt, rt, bt, bt, kt, kt, wc, wc)

    tok2 = lambda w: pl.BlockSpec((1, PROJ_TM, w), lambda b, i: (b, i, 0))
    return pl.pallas_call(
        _rw_out_kernel,
        grid=(BATCH, T_ALL // PROJ_TM),
        in_specs=[tok2(RW_W), tok2(RW_W), tok2(RW_W), tok2(RW_W), tok2(RW_W),
                  const((1, RW_W)), const((1, RW_W)), const((RW_W, RW_W))],
        out_specs=tok2(RW_W),
        out_shape=jax.ShapeDtypeStruct((BATCH, T_ALL, RW_W), F32),
        compiler_params=_cparams(("parallel", "parallel")),
        name="rwkv_output",
    )(yf, yb, bonus, v, g, row(ln_w), row(ln_b), bd)


@functools.lru_cache(maxsize=None)
def _na_bias_tables():
    qc = np.arange(GRID_W)[:, None]
    kc = np.arange(GRID_W)[None, :]
    win0 = np.clip(qc - WIN_COLS // 2, 0, GRID_W - WIN_COLS)
    valid = (kc >= win0) & (kc < win0 + WIN_COLS)
    off = np.clip(kc - qc + WIN_COLS - 1, 0, 2 * WIN_COLS - 2)
    onehot = np.zeros((128, GRID_W * GRID_W), np.float32)
    onehot[off.reshape(-1), np.arange(GRID_W * GRID_W)] = valid.reshape(-1)
    mask = np.where(valid, 0.0, NEG_INF).astype(np.float32).reshape(1, GRID_W * GRID_W)
    return onehot, mask


def _na_bias_kernel(rpb_ref, oh_ref, mask_ref, o_ref):
    o_ref[...] = _dot(rpb_ref[...], oh_ref[...]) + mask_ref[...]


def _na_lat_kernel(q_ref, k_ref, v_ref, tp_ref, o_ref):
    for g in range(NA_G):
        _na_lat_row(pl.program_id(1) * NA_G + g, g, q_ref, k_ref, v_ref, tp_ref, o_ref)


def _na_lat_row(i, g, q_ref, k_ref, v_ref, tp_ref, o_ref):
    start = jnp.clip(i - WIN_ROWS // 2, 0, N_ROWS - WIN_ROWS)
    dr0 = start - i + WIN_ROWS - 1
    scale = NA_HD ** -0.5
    lane = lax.broadcasted_iota(jnp.int32, (1, NA_W), 1)
    hmask = [(lane // NA_HD) == h for h in range(NA_HEADS)]
    q = q_ref[0, g * GRID_W:(g + 1) * GRID_W, :]
    qm = jnp.concatenate([jnp.where(hmask[h], q, jnp.zeros_like(q)) for h in range(NA_HEADS)], axis=0)
    row0 = pl.multiple_of(start * GRID_W, GRID_W)
    kl = k_ref[0, pl.ds(row0, WIN_ROWS * GRID_W), :]
    vl = v_ref[0, pl.ds(row0, WIN_ROWS * GRID_W), :]
    kc = k_ref[0, SEQ:T_ALL, :]
    vc = v_ref[0, SEQ:T_ALL, :]
    nt = (((1,), (1,)), ((), ()))
    bias = jnp.concatenate(
        [jnp.concatenate([tp_ref[h, pl.ds(dr0 + 2 * j, 1)][0] for j in range(WIN_ROWS // 2)], axis=1)
         for h in range(NA_HEADS)], axis=0)
    s_loc = lax.dot_general(qm, kl, nt, preferred_element_type=F32) * scale + bias
    s_ctx = lax.dot_general(qm, kc, nt, preferred_element_type=F32) * scale
    m = jnp.maximum(jnp.max(s_loc, axis=-1, keepdims=True), jnp.max(s_ctx, axis=-1, keepdims=True))
    p_loc = jnp.exp(s_loc - m)
    p_ctx = jnp.exp(s_ctx - m)
    inv = 1.0 / (jnp.sum(p_loc, axis=-1, keepdims=True) + jnp.sum(p_ctx, axis=-1, keepdims=True))
    o_all = (jnp.dot((p_loc * inv).astype(BF16), vl, preferred_element_type=F32)
             + jnp.dot((p_ctx * inv).astype(BF16), vc, preferred_element_type=F32))
    o = jnp.zeros((GRID_W, NA_W), F32)
    for h in range(NA_HEADS):
        o = o + jnp.where(hmask[h], o_all[h * GRID_W:(h + 1) * GRID_W, :], 0.0)
    o_ref[0, g * GRID_W:(g + 1) * GRID_W, :] = o


def _na_ctx_kernel(q_ref, k_ref, v_ref, o_ref):
    scale = NA_HD ** -0.5
    lane = lax.broadcasted_iota(jnp.int32, (1, NA_W), 1)
    hmask = [(lane // NA_HD) == h for h in range(NA_HEADS)]
    q, k, v = q_ref[0], k_ref[0], v_ref[0]
    qm = jnp.concatenate([jnp.where(hmask[h], q, jnp.zeros_like(q)) for h in range(NA_HEADS)], axis=0)
    s = lax.dot_general(qm, k, (((1,), (1,)), ((), ())), preferred_element_type=F32) * scale
    p = jnp.exp(s - jnp.max(s, axis=-1, keepdims=True))
    p = p / jnp.sum(p, axis=-1, keepdims=True)
    o_all = jnp.dot(p.astype(BF16), v, preferred_element_type=F32)
    o = jnp.zeros((CTX_LEN, NA_W), F32)
    for h in range(NA_HEADS):
        o = o + jnp.where(hmask[h], o_all[h * CTX_LEN:(h + 1) * CTX_LEN, :], 0.0)
    o_ref[0] = o


def _natten(p_na, rpb):
    onehot, mask = (jnp.asarray(a) for a in _na_bias_tables())
    n_dr = 2 * WIN_ROWS - 1
    n_rel = 2 * WIN_COLS - 1
    rows = NA_HEADS * n_dr
    rows_p = -(-rows // 8) * 8
    rpb2 = jnp.zeros((rows_p, 128), F32).at[:rows, :n_rel].set(rpb.reshape(rows, n_rel))
    tb = pl.pallas_call(
        _na_bias_kernel,
        out_shape=jax.ShapeDtypeStruct((rows_p, GRID_W * GRID_W), F32),
        name="natten_bias_expand",
    )(rpb2, onehot, mask)
    tb = tb[:rows].reshape(NA_HEADS, n_dr, GRID_W, GRID_W)
    nxt = jnp.concatenate([tb[:, 1:], jnp.full_like(tb[:, :1], NEG_INF)], axis=1)
    tp = jnp.concatenate([tb, nxt], axis=-1)

    o_lat = pl.pallas_call(
        _na_lat_kernel,
        grid=(BATCH, N_ROWS // NA_G),
        in_specs=[pl.BlockSpec((1, NA_G * GRID_W, NA_W), lambda b, i: (b, i, 0)),
                  pl.BlockSpec((1, T_ALL, NA_W), lambda b, i: (b, 0, 1)),
                  pl.BlockSpec((1, T_ALL, NA_W), lambda b, i: (b, 0, 2)),
                  pl.BlockSpec((NA_HEADS, n_dr, GRID_W, 2 * GRID_W), lambda b, i: (0, 0, 0, 0))],
        out_specs=pl.BlockSpec((1, NA_G * GRID_W, NA_W), lambda b, i: (b, i, 0)),
        out_shape=jax.ShapeDtypeStruct((BATCH, SEQ, NA_W), F32),
        compiler_params=_cparams(("parallel", "parallel")),
        name="natten_latent",
    )(p_na, p_na, p_na, tp)

    cblk = SEQ // CTX_LEN
    o_ctx = pl.pallas_call(
        _na_ctx_kernel,
        grid=(BATCH,),
        in_specs=[pl.BlockSpec((1, CTX_LEN, NA_W), lambda b: (b, cblk, 0)),
                  pl.BlockSpec((1, CTX_LEN, NA_W), lambda b: (b, cblk, 1)),
                  pl.BlockSpec((1, CTX_LEN, NA_W), lambda b: (b, cblk, 2))],
        out_specs=pl.BlockSpec((1, CTX_LEN, NA_W), lambda b: (b, 0, 0)),
        out_shape=jax.ShapeDtypeStruct((BATCH, CTX_LEN, NA_W), F32),
        compiler_params=_cparams(("parallel",)),
        name="natten_context",
    )(p_na, p_na, p_na)
    return jnp.concatenate([o_lat, o_ctx], axis=1)


def kernel(x, c, ctx, c_ctx, mod_w, mod_b, norm_g, ffn1_wgu, ffn1_wdn, ffn2_wgu, ffn2_wdn, w_in, w_out, hy_conv_w, hy_conv_b, hy_f_w1, hy_f_b1, hy_f_w2, hy_f_b2, hy_f_w3, hy_freq, hy_bias, rw_mu, rw_w0, rw_w2, rw_a0, rw_a2, rw_g2, rw_k_k, rw_k_a, rw_r_k, rw_ln_w, rw_ln_b, na_rpb):
    assert x.shape == (BATCH, SEQ, D_MODEL) and ctx.shape == (BATCH, CTX_LEN, D_MODEL)
    mods = _modulation(c, c_ctx, mod_w, mod_b)
    h = jnp.concatenate([x, ctx], axis=1)
    gpad = jnp.concatenate([norm_g, jnp.zeros((DEPTH, 2, D_MODEL), F32)], axis=1)
    for l in range(DEPTH):
        m_l, g_l = mods[l], gpad[l]
        h = _ffn(h, m_l, g_l, ffn1_wgu[l].astype(BF16), ffn1_wdn[l].astype(BF16), 0)
        p_hy, p_rw, p_na = _in_proj(h, m_l, g_l, w_in[l].astype(BF16))
        o_hy = _hyena(p_hy, hy_conv_w[l], hy_conv_b[l], hy_f_w1[l], hy_f_b1[l], hy_f_w2[l], hy_f_b2[l],
                      hy_f_w3[l], hy_freq[l], hy_bias[l])
        o_rw = _rwkv(p_rw, rw_mu[l], rw_w0[l], rw_w2[l], rw_a0[l], rw_a2[l], rw_g2[l], rw_k_k[l], rw_k_a[l],
                     rw_r_k[l], rw_ln_w[l], rw_ln_b[l])
        o_na = _natten(p_na, na_rpb[l])
        h = _out_proj(h, o_hy, o_rw, o_na, m_l, g_l, w_out[l].astype(BF16))
        h = _ffn(h, m_l, g_l, ffn2_wgu[l].astype(BF16), ffn2_wdn[l].astype(BF16), 1)
    return h[:, :SEQ]
```

```python
import functools
import math

import numpy as np
import jax
import jax.numpy as jnp
from jax import lax
from jax.experimental import pallas as pl
from jax.experimental.pallas import tpu as pltpu

D_MODEL = 1024
BATCH = 2
SEQ = 8192
DEPTH = 4
GRID_W = 64
CTX_LEN = 256
T_ALL = SEQ + CTX_LEN
N_MOD = 9
D_FF = 2816
MACARON_W = 0.5
NORM_EPS = 1e-6
NEG_INF = -1e30

HY_CH = 256
HY_ORDER = 2
HY_EMB = 33
HY_FILT = 64
HY_TARGET = 1e-2
HY_FAST = 0.3
HY_SLOW = 1.5
HY_IN = (HY_ORDER + 1) * HY_CH

RW_HEADS = 6
RW_HD = 64
RW_W = RW_HEADS * RW_HD
RW_LORA = 64
RW_GATE_LORA = 128
RW_GN_EPS = 64e-5
ROPE_BASE = 10000.0
RW_IN = 3 * RW_W + 4 * RW_LORA + RW_GATE_LORA

NA_HEADS = 6
NA_HD = 64
NA_W = NA_HEADS * NA_HD
WIN_ROWS = 8
WIN_COLS = 16
NA_IN = 3 * NA_W
N_ROWS = SEQ // GRID_W

F32 = jnp.float32
BF16 = jnp.bfloat16
HI = lax.Precision.HIGHEST

VMEM_LIMIT = 56 * 1024 * 1024

FFN_TM = 768
FFN_TF = 1408
FFN_SUB = 2
PROJ_TM = 768
PROJ_SUB = 2
RW_TR = 256
RW_C = 64
FFT_N1 = 128
FFT_N2 = 128
FFT_CT = 16
FFT_SUB = 2
NA_G = 8


def _dot(a, b):
    return jnp.dot(a, b, precision=HI, preferred_element_type=F32)


def _split(x):
    hi = x.astype(BF16)
    return hi, (x - hi.astype(F32)).astype(BF16)


def _dot3(x, w_ref, idx=()):
    xh, xm = _split(x)
    wh, wm = w_ref[idx + (0,)], w_ref[idx + (1,)]
    d = functools.partial(jnp.dot, preferred_element_type=F32)
    return d(xh, wh) + d(xm, wh) + d(xh, wm)


def _split3(x):
    hi = x.astype(BF16)
    r1 = x - hi.astype(F32)
    mid = r1.astype(BF16)
    return hi, mid, (r1 - mid.astype(F32)).astype(BF16)


def _dot01(x, m, const_left=False):
    d = functools.partial(jnp.dot, preferred_element_type=F32)
    parts = _split3(x)
    if const_left:
        return d(m, parts[0]) + d(m, parts[1]) + d(m, parts[2])
    return d(parts[0], m) + d(parts[1], m) + d(parts[2], m)


def _cparams(sem, vmem=VMEM_LIMIT):
    return pltpu.CompilerParams(dimension_semantics=sem, vmem_limit_bytes=vmem)


def _mod_kernel(cv_ref, w_ref, b_ref, o_ref):
    cv = cv_ref[...]
    s = cv * jax.nn.sigmoid(cv)
    o_ref[0] = _dot(s, w_ref[0]) + b_ref[0]


def _modulation(c, c_ctx, mod_w, mod_b):
    cv = jnp.concatenate([c, c_ctx[None, :], jnp.zeros((8 - BATCH - 1, D_MODEL), F32)], axis=0)
    tn = 1536
    return pl.pallas_call(
        _mod_kernel,
        grid=(DEPTH, N_MOD * D_MODEL // tn),
        in_specs=[pl.BlockSpec((8, D_MODEL), lambda l, j: (0, 0)),
                  pl.BlockSpec((1, D_MODEL, tn), lambda l, j: (l, 0, j)),
                  pl.BlockSpec((1, 1, tn), lambda l, j: (l, 0, j))],
        out_specs=pl.BlockSpec((1, 8, tn), lambda l, j: (l, 0, j)),
        out_shape=jax.ShapeDtypeStruct((DEPTH, 8, N_MOD * D_MODEL), F32),
        compiler_params=_cparams(("parallel", "parallel")),
        name="adaln_modulation",
    )(cv, mod_w, mod_b.reshape(DEPTH, 1, N_MOD * D_MODEL))


def _mod_rows(m_ref, b, i, tm, idx, sub=0, n_sub=1):
    n = tm // n_sub
    t = i * tm + sub * n + lax.broadcasted_iota(jnp.int32, (n, 1), 0)
    ml = m_ref[pl.ds(b, 1), idx * D_MODEL:(idx + 1) * D_MODEL]
    mc = m_ref[BATCH:BATCH + 1, idx * D_MODEL:(idx + 1) * D_MODEL]
    return jnp.where(t >= SEQ, mc, ml)


def _rms(x, g):
    return x * lax.rsqrt(jnp.mean(x * x, axis=-1, keepdims=True) + NORM_EPS) * g


def _ffn_kernel(h_ref, m_ref, g_ref, wgu_ref, wdn_ref, o_ref, *, mi, gi):
    b, i = pl.program_id(0), pl.program_id(1)
    n = FFN_TM // FFN_SUB
    for sub in range(FFN_SUB):
        mod = functools.partial(_mod_rows, m_ref, b, i, FFN_TM, sub=sub, n_sub=FFN_SUB)
        h = h_ref[0, sub * n:(sub + 1) * n, :]
        u = (_rms(h, g_ref[gi:gi + 1, :]) * (1.0 + mod(mi + 1)) + mod(mi)).astype(BF16)
        acc = jnp.zeros((n, D_MODEL), F32)
        for k in range(D_FF // FFN_TF):
            gate = jnp.dot(u, wgu_ref[:, k * FFN_TF:(k + 1) * FFN_TF], preferred_element_type=F32)
            up = jnp.dot(u, wgu_ref[:, D_FF + k * FFN_TF:D_FF + (k + 1) * FFN_TF], preferred_element_type=F32)
            a = (gate * jax.nn.sigmoid(gate) * up).astype(BF16)
            acc = acc + jnp.dot(a, wdn_ref[k * FFN_TF:(k + 1) * FFN_TF, :], preferred_element_type=F32)
        y = _rms(acc, g_ref[gi + 1:gi + 2, :])
        o_ref[0, sub * n:(sub + 1) * n, :] = h + MACARON_W * mod(mi + 2) * y


def _ffn(h, mods_l, g_l, wgu, wdn, which):
    mi, gi = (0, 0) if which == 0 else (6, 4)
    resident = pl.Buffered(1)
    return pl.pallas_call(
        functools.partial(_ffn_kernel, mi=mi, gi=gi),
        grid=(BATCH, T_ALL // FFN_TM),
        in_specs=[pl.BlockSpec((1, FFN_TM, D_MODEL), lambda b, i: (b, i, 0)),
                  pl.BlockSpec((8, N_MOD * D_MODEL), lambda b, i: (0, 0)),
                  pl.BlockSpec((8, D_MODEL), lambda b, i: (0, 0)),
                  pl.BlockSpec((D_MODEL, 2 * D_FF), lambda b, i: (0, 0), pipeline_mode=resident),
                  pl.BlockSpec((D_FF, D_MODEL), lambda b, i: (0, 0), pipeline_mode=resident)],
        out_specs=pl.BlockSpec((1, FFN_TM, D_MODEL), lambda b, i: (b, i, 0)),
        out_shape=jax.ShapeDtypeStruct((BATCH, T_ALL, D_MODEL), F32),
        compiler_params=_cparams(("parallel", "parallel")),
        name="ffn_sublayer",
    )(h, mods_l, g_l, wgu, wdn)


def _inproj_kernel(h_ref, m_ref, g_ref, w_ref, hy_ref, rw_ref, na_ref):
    b, i = pl.program_id(0), pl.program_id(1)
    n = PROJ_TM // PROJ_SUB
    for sub in range(PROJ_SUB):
        rows = slice(sub * n, (sub + 1) * n)
        mod = functools.partial(_mod_rows, m_ref, b, i, PROJ_TM, sub=sub, n_sub=PROJ_SUB)
        u = _rms(h_ref[0, rows, :], g_ref[2:3, :]) * (1.0 + mod(4)) + mod(3)
        p = jnp.dot(u.astype(BF16), w_ref[...], preferred_element_type=F32)
        hy_ref[0, rows, :] = p[:, :HY_IN]
        rw_ref[0, rows, :] = p[:, HY_IN:HY_IN + RW_IN]
        na_ref[0, rows, :] = p[:, HY_IN + RW_IN:].astype(BF16)


def _in_proj(h, mods_l, g_l, w_in):
    n_in = HY_IN + RW_IN + NA_IN
    return pl.pallas_call(
        _inproj_kernel,
        grid=(BATCH, T_ALL // PROJ_TM),
        in_specs=[pl.BlockSpec((1, PROJ_TM, D_MODEL), lambda b, i: (b, i, 0)),
                  pl.BlockSpec((8, N_MOD * D_MODEL), lambda b, i: (0, 0)),
                  pl.BlockSpec((8, D_MODEL), lambda b, i: (0, 0)),
                  pl.BlockSpec((D_MODEL, n_in), lambda b, i: (0, 0), pipeline_mode=pl.Buffered(1))],
        out_specs=[pl.BlockSpec((1, PROJ_TM, HY_IN), lambda b, i: (b, i, 0)),
                   pl.BlockSpec((1, PROJ_TM, RW_IN), lambda b, i: (b, i, 0)),
                   pl.BlockSpec((1, PROJ_TM, NA_IN), lambda b, i: (b, i, 0))],
        out_shape=[jax.ShapeDtypeStruct((BATCH, T_ALL, HY_IN), F32),
                   jax.ShapeDtypeStruct((BATCH, T_ALL, RW_IN), F32),
                   jax.ShapeDtypeStruct((BATCH, T_ALL, NA_IN), BF16)],
        compiler_params=_cparams(("parallel", "parallel")),
        name="mixer_in_proj",
    )(h, mods_l, g_l, w_in)


def _outproj_kernel(h_ref, hy_ref, rw_ref, na_ref, m_ref, g_ref, w_ref, o_ref):
    b, i = pl.program_id(0), pl.program_id(1)
    n = PROJ_TM // PROJ_SUB
    for sub in range(PROJ_SUB):
        rows = slice(sub * n, (sub + 1) * n)
        cat = jnp.concatenate([hy_ref[0, rows, :], rw_ref[0, rows, :], na_ref[0, rows, :]], axis=-1).astype(BF16)
        o = jnp.dot(cat, w_ref[...], preferred_element_type=F32)
        gate = _mod_rows(m_ref, b, i, PROJ_TM, 5, sub=sub, n_sub=PROJ_SUB)
        o_ref[0, rows, :] = h_ref[0, rows, :] + gate * _rms(o, g_ref[3:4, :])


def _out_proj(h, o_hy, o_rw, o_na, mods_l, g_l, w_out):
    return pl.pallas_call(
        _outproj_kernel,
        grid=(BATCH, T_ALL // PROJ_TM),
        in_specs=[pl.BlockSpec((1, PROJ_TM, D_MODEL), lambda b, i: (b, i, 0)),
                  pl.BlockSpec((1, PROJ_TM, HY_CH), lambda b, i: (b, i, 0)),
                  pl.BlockSpec((1, PROJ_TM, RW_W), lambda b, i: (b, i, 0)),
                  pl.BlockSpec((1, PROJ_TM, NA_W), lambda b, i: (b, i, 0)),
                  pl.BlockSpec((8, N_MOD * D_MODEL), lambda b, i: (0, 0)),
                  pl.BlockSpec((8, D_MODEL), lambda b, i: (0, 0)),
                  pl.BlockSpec((D_MODEL, D_MODEL), lambda b, i: (0, 0), pipeline_mode=pl.Buffered(1))],
        out_specs=pl.BlockSpec((1, PROJ_TM, D_MODEL), lambda b, i: (b, i, 0)),
        out_shape=jax.ShapeDtypeStruct((BATCH, T_ALL, D_MODEL), F32),
        compiler_params=_cparams(("parallel", "parallel")),
        name="mixer_out_proj",
    )(h, o_hy, o_rw, o_na, mods_l, g_l, w_out)


def _hy_short_kernel(x_ref, w_ref, b_ref, o_ref):
    x = x_ref[0]
    n = x.shape[0]
    row = lax.broadcasted_iota(jnp.int32, (n, 1), 0)
    prev = jnp.where(row == 0, 0.0, pltpu.roll(x, 1, 0))
    nxt = jnp.where(row == n - 1, 0.0, pltpu.roll(x, n - 1, 0))
    o_ref[0] = prev * w_ref[0:1, :] + x * w_ref[1:2, :] + nxt * w_ref[2:3, :] + b_ref[...]


def _hy_short(p_hy, conv_w, conv_b, seq_len, row_block):
    lanes = 128
    return pl.pallas_call(
        _hy_short_kernel,
        grid=(BATCH, HY_IN // lanes),
        in_specs=[pl.BlockSpec((1, seq_len, lanes), lambda b, j: (b, row_block, j)),
                  pl.BlockSpec((3, lanes), lambda b, j: (0, j)),
                  pl.BlockSpec((1, lanes), lambda b, j: (0, j))],
        out_specs=pl.BlockSpec((1, seq_len, lanes), lambda b, j: (b, 0, j)),
        out_shape=jax.ShapeDtypeStruct((BATCH, seq_len, HY_IN), F32),
        compiler_params=_cparams(("parallel", "parallel")),
        name="hyena_short_conv",
    )(p_hy, conv_w, conv_b.reshape(1, HY_IN))


@functools.lru_cache(maxsize=None)
def _filter_features(seq_len):
    n = 2 * seq_len
    m = np.arange(n)
    d = np.where(m < seq_len, m, n - m).astype(np.float64)
    t = d / (seq_len - 1)
    ang = 2.0 * math.pi * d / seq_len
    bands = (HY_EMB - 1) // 2
    fr = np.linspace(1e-4, bands - 1, bands)
    z = np.zeros((n, 128), np.float64)
    z[:, 0] = t
    z[:, 1:1 + bands] = np.cos(fr[None, :] * ang[:, None])
    z[:, 1 + bands:1 + 2 * bands] = -np.sin(fr[None, :] * ang[:, None])
    z[:, 33] = (m < seq_len)
    z[:, 34] = (m > seq_len)
    z[seq_len, :] = 0.0
    return z.astype(np.float32)


@functools.lru_cache(maxsize=None)
def _filter_deltas():
    d = np.abs(np.linspace(math.log(HY_TARGET) / HY_SLOW, math.log(HY_TARGET) / HY_FAST, HY_CH))
    return d.astype(np.float32).reshape(1, HY_CH)


def _hy_filter_kernel(z_ref, w1_ref, b1_ref, w2_ref, b2_ref, w3_ref, fq_ref, dl_ref, o_ref):
    z = z_ref[...]
    fq = fq_ref[...]
    h = jnp.sin(fq * (_dot(z, w1_ref[...]) + b1_ref[...]))
    h = jnp.sin(fq * (_dot(h, w2_ref[...]) + b2_ref[...]))
    o = _dot3(h, w3_ref)
    dec = jnp.exp(-z[:, 0:1] * dl_ref[...])
    mf = z[:, 33:34]
    mb = z[:, 34:35]
    outs = []
    for od in range(HY_ORDER):
        fwd = o[:, od * 2 * HY_CH:od * 2 * HY_CH + HY_CH]
        bwd = o[:, od * 2 * HY_CH + HY_CH:(od + 1) * 2 * HY_CH]
        outs.append((mf * fwd + mb * bwd) * dec)
    o_ref[...] = jnp.concatenate(outs, axis=1)


def _hy_filter(seq_len, w1, b1, w2, b2, w3, freq):
    n = 2 * seq_len
    tp = min(n, 1024)
    zt = jnp.asarray(_filter_features(seq_len))
    w1p = jnp.zeros((128, HY_FILT), F32).at[:HY_EMB].set(w1)
    full = lambda shape: pl.BlockSpec(shape, lambda i: (0,) * len(shape))
    return pl.pallas_call(
        _hy_filter_kernel,
        grid=(n // tp,),
        in_specs=[pl.BlockSpec((tp, 128), lambda i: (i, 0)),
                  full((128, HY_FILT)), full((1, HY_FILT)), full((HY_FILT, HY_FILT)), full((1, HY_FILT)),
                  full((2, HY_FILT, HY_ORDER * 2 * HY_CH)), full((1, HY_FILT)), full((1, HY_CH))],
        out_specs=pl.BlockSpec((tp, HY_ORDER * HY_CH), lambda i: (i, 0)),
        out_shape=jax.ShapeDtypeStruct((n, HY_ORDER * HY_CH), F32),
        compiler_params=_cparams(("parallel",)),
        name="hyena_filter_mlp",
    )(zt, w1p, b1.reshape(1, -1), w2, b2.reshape(1, -1), jnp.stack(_split(w3)), freq.reshape(1, -1),
      jnp.asarray(_filter_deltas()))


@functools.lru_cache(maxsize=None)
def _fft_tables():
    n1 = np.arange(FFT_N1)[:, None]
    k1 = np.arange(FFT_N1)[None, :]
    th = 2.0 * math.pi * n1 * k1 / FFT_N1
    c, s = np.cos(th), np.sin(th)
    half = FFT_N1 // 2
    big_n = FFT_N1 * FFT_N2
    wa_c = np.block([[c[:half], -s[:half]], [s[:half], c[:half]]])
    wa_r = np.concatenate([c, -s], axis=1)
    tw = 2.0 * math.pi * np.arange(FFT_N2)[:, None] * np.arange(FFT_N1)[None, :] / big_n
    twf = np.concatenate([np.cos(tw), -np.sin(tw)], axis=1)
    twi = np.concatenate([np.cos(tw.T), np.sin(tw.T)], axis=1)
    wb = np.block([[c, -s], [s, c]])
    wc = np.block([[c, s], [-s, c]])
    wd = np.block([[c[:, :half], s[:, :half]], [-s[:, :half], c[:, :half]]]) / big_n
    return tuple(np.asarray(a, np.float32) for a in (wa_c, wa_r, twf, twi, wb, wc, wd))


def _fft_fwd(z, wa_ref, tw, wb_ref):
    ct = z.shape[0]
    a = _dot3(z.reshape(ct * FFT_N2, z.shape[2]), wa_ref).reshape(ct, FFT_N2, 2 * FFT_N1)
    ar, ai = a[:, :, :FFT_N1], a[:, :, FFT_N1:]
    tr, ti = tw[:, :FFT_N1], tw[:, FFT_N1:]
    br = ar * tr - ai * ti
    bi = ar * ti + ai * tr
    bt = jnp.concatenate([jnp.swapaxes(br, 1, 2), jnp.swapaxes(bi, 1, 2)], axis=2)
    return _dot3(bt.reshape(ct * FFT_N1, 2 * FFT_N2), wb_ref).reshape(ct, FFT_N1, 2 * FFT_N2)


def _hy_kfft_kernel(k_ref, wa_ref, tw_ref, wb_ref, o_ref):
    ct = FFT_CT // FFT_SUB
    for sub in range(FFT_SUB):
        ch = slice(sub * ct, (sub + 1) * ct)
        o_ref[ch] = _fft_fwd(k_ref[ch], wa_ref, tw_ref[...], wb_ref)


def _hy_long_kernel(z_ref, g_ref, kh_ref, bias_ref, wa_ref, twf_ref, wb_ref, wc_ref, twi_ref, wd_ref, o_ref):
    ct = FFT_CT // FFT_SUB
    for sub in range(FFT_SUB):
        ch = slice(sub * ct, (sub + 1) * ct)
        _hy_long_group(ch, z_ref, g_ref, kh_ref, bias_ref, wa_ref, twf_ref, wb_ref, wc_ref, twi_ref, wd_ref, o_ref)


def _hy_long_group(ch, z_ref, g_ref, kh_ref, bias_ref, wa_ref, twf_ref, wb_ref, wc_ref, twi_ref, wd_ref, o_ref):
    z = z_ref[ch]
    ct = z.shape[0]
    x = _fft_fwd(z, wa_ref, twf_ref[...], wb_ref)
    kh = kh_ref[ch]
    xr, xi = x[:, :, :FFT_N2], x[:, :, FFT_N2:]
    kr, ki = kh[:, :, :FFT_N2], kh[:, :, FFT_N2:]
    y = jnp.concatenate([xr * kr - xi * ki, xr * ki + xi * kr], axis=2)
    c1 = _dot3(y.reshape(ct * FFT_N1, 2 * FFT_N2), wc_ref).reshape(ct, FFT_N1, 2 * FFT_N2)
    cr, ci = c1[:, :, :FFT_N2], c1[:, :, FFT_N2:]
    twi = twi_ref[...]
    tr, ti = twi[:, :FFT_N2], twi[:, FFT_N2:]
    dr = cr * tr - ci * ti
    di = cr * ti + ci * tr
    dt = jnp.concatenate([jnp.swapaxes(dr, 1, 2), jnp.swapaxes(di, 1, 2)], axis=2)
    conv = _dot3(dt.reshape(ct * FFT_N2, 2 * FFT_N1), wd_ref).reshape(ct, FFT_N2, FFT_N1)
    o_ref[ch] = (conv + bias_ref[ch] * z) * g_ref[ch]


def _hy_ctx_kernel(x_ref, kc_ref, bias_ref, cf_ref, sf_ref, o_ref):
    x = x_ref[0]
    cf, sf = cf_ref[...], sf_ref[...]
    n = CTX_LEN
    z = x[:, :HY_CH]
    for od in range(HY_ORDER):
        k = kc_ref[:, od * HY_CH:(od + 1) * HY_CH]
        kr, ki = _dot(cf, k), -_dot(sf, k)
        xr, xi = _dot(cf[:, :n], z), -_dot(sf[:, :n], z)
        yr = xr * kr - xi * ki
        yi = xr * ki + xi * kr
        conv = (_dot(cf[:n, :], yr) - _dot(sf[:n, :], yi)) * (1.0 / (2 * n))
        z = x[:, (od + 1) * HY_CH:(od + 2) * HY_CH] * (conv + bias_ref[od:od + 1, :] * z)
    o_ref[0] = z


@functools.lru_cache(maxsize=None)
def _ctx_dft():
    m = np.arange(2 * CTX_LEN)
    th = 2.0 * math.pi * m[:, None] * m[None, :] / (2 * CTX_LEN)
    return np.cos(th).astype(np.float32), np.sin(th).astype(np.float32)


def _hyena(p_hy, conv_w, conv_b, w1, b1, w2, b2, w3, freq, bias):
    wa_c, wa_r, twf, twi, wb, wc, wd = (jnp.asarray(a) for a in _fft_tables())
    wa_c, wa_r, wb, wc, wd = (jnp.stack(_split(a)) for a in (wa_c, wa_r, wb, wc, wd))
    nch = HY_ORDER * HY_CH
    kf = _hy_filter(SEQ, w1, b1, w2, b2, w3, freq)
    kt = kf.reshape(FFT_N1, FFT_N2, nch).transpose(2, 1, 0)
    const = lambda shape: pl.BlockSpec(shape, lambda j: (0,) * len(shape))
    khat = pl.pallas_call(
        _hy_kfft_kernel,
        grid=(nch // FFT_CT,),
        in_specs=[pl.BlockSpec((FFT_CT, FFT_N2, FFT_N1), lambda j: (j, 0, 0)),
                  const((2, FFT_N1, 2 * FFT_N1)), const((FFT_N2, 2 * FFT_N1)), const((2, 2 * FFT_N2, 2 * FFT_N2))],
        out_specs=pl.BlockSpec((FFT_CT, FFT_N1, 2 * FFT_N2), lambda j: (j, 0, 0)),
        out_shape=jax.ShapeDtypeStruct((nch, FFT_N1, 2 * FFT_N2), F32),
        compiler_params=_cparams(("parallel",)),
        name="hyena_filter_fft",
    )(kt, wa_r, twf, wb)

    pc = _hy_short(p_hy, conv_w, conv_b, SEQ, 0)
    half = FFT_N1 // 2
    zall = pc.reshape(BATCH, half, FFT_N2, HY_IN).transpose(3, 2, 0, 1).reshape(HY_IN, FFT_N2, FFT_N1)
    nblk = HY_CH // FFT_CT
    z = zall
    for od in range(HY_ORDER):
        z_spec = pl.BlockSpec((FFT_CT, FFT_N2, FFT_N1), lambda j: (j, 0, 0))
        g_spec = pl.BlockSpec((FFT_CT, FFT_N2, FFT_N1), lambda j, od=od: ((od + 1) * nblk + j, 0, 0))
        z = pl.pallas_call(
            _hy_long_kernel,
            grid=(nblk,),
            in_specs=[z_spec, g_spec,
                      pl.BlockSpec((FFT_CT, FFT_N1, 2 * FFT_N2), lambda j, od=od: (od * nblk + j, 0, 0)),
                      pl.BlockSpec((FFT_CT, 1, 1), lambda j, od=od: (od * nblk + j, 0, 0)),
                      const((2, FFT_N1, 2 * FFT_N1)), const((FFT_N2, 2 * FFT_N1)), const((2, 2 * FFT_N2, 2 * FFT_N2)),
                      const((2, 2 * FFT_N2, 2 * FFT_N2)), const((FFT_N1, 2 * FFT_N2)), const((2, 2 * FFT_N1, FFT_N1))],
            out_specs=pl.BlockSpec((FFT_CT, FFT_N2, FFT_N1), lambda j: (j, 0, 0)),
            out_shape=jax.ShapeDtypeStruct((HY_CH, FFT_N2, FFT_N1), F32),
            compiler_params=_cparams(("parallel",)),
            name="hyena_long_conv",
        )(z, zall, khat, bias.reshape(nch, 1, 1), wa_c, twf, wb, wc, twi, wd)
    o_lat = z.reshape(HY_CH, FFT_N2, BATCH, half).transpose(2, 3, 1, 0).reshape(BATCH, SEQ, HY_CH)

    kc = _hy_filter(CTX_LEN, w1, b1, w2, b2, w3, freq)
    pcc = _hy_short(p_hy, conv_w, conv_b, CTX_LEN, SEQ // CTX_LEN)
    cf, sf = (jnp.asarray(a) for a in _ctx_dft())
    o_ctx = pl.pallas_call(
        _hy_ctx_kernel,
        grid=(BATCH,),
        in_specs=[pl.BlockSpec((1, CTX_LEN, HY_IN), lambda b: (b, 0, 0)),
                  pl.BlockSpec((2 * CTX_LEN, nch), lambda b: (0, 0)),
                  pl.BlockSpec((HY_ORDER, HY_CH), lambda b: (0, 0)),
                  pl.BlockSpec((2 * CTX_LEN, 2 * CTX_LEN), lambda b: (0, 0)),
                  pl.BlockSpec((2 * CTX_LEN, 2 * CTX_LEN), lambda b: (0, 0))],
        out_specs=pl.BlockSpec((1, CTX_LEN, HY_CH), lambda b: (b, 0, 0)),
        out_shape=jax.ShapeDtypeStruct((BATCH, CTX_LEN, HY_CH), F32),
        compiler_params=_cparams(("parallel",)),
        name="hyena_context",
    )(pcc, kc, bias, cf, sf)
    return jnp.concatenate([o_lat, o_ctx], axis=1)


@functools.lru_cache(maxsize=None)
def _rope_tables():
    nf = RW_HD // 4
    t = np.arange(SEQ)
    row = (t // GRID_W).astype(np.float64)
    col = (t % GRID_W).astype(np.float64)
    inv = ROPE_BASE ** (-np.arange(nf, dtype=np.float64) / nf)
    ang = np.concatenate([row[:, None] * inv] * 2 + [col[:, None] * inv] * 2, axis=-1)
    cos = np.concatenate([np.cos(ang), np.ones((CTX_LEN, RW_HD))], axis=0)
    sin = np.concatenate([np.sin(ang), np.zeros((CTX_LEN, RW_HD))], axis=0)
    return (np.tile(cos, (1, RW_HEADS)).astype(np.float32), np.tile(sin, (1, RW_HEADS)).astype(np.float32))


@functools.lru_cache(maxsize=None)
def _head_blockdiag():
    h = np.arange(RW_W) // RW_HD
    return (h[:, None] == h[None, :]).astype(np.float32)


@functools.lru_cache(maxsize=None)
def _chunk_tri():
    t = np.arange(RW_TR)[:, None]
    s = np.arange(RW_TR)[None, :]
    same = (t // RW_C) == (s // RW_C)
    return np.stack([same & (s <= t), same & (s >= t)]).astype(np.float32)


def _rope(x, cos, sin):
    lane = lax.broadcasted_iota(jnp.int32, (1, RW_W), 1)
    first = (lane % (RW_HD // 2)) < (RW_HD // 4)
    rot = jnp.where(first, -pltpu.roll(x, RW_W - RW_HD // 4, 1), pltpu.roll(x, RW_HD // 4, 1))
    return x * cos + rot * sin


def _rw_prep_kernel(p_ref, pv_ref, nx_ref, mu_ref, w0_ref, w2_ref, a0_ref, a2_ref, g2_ref, kk_ref, ka_ref, rk_ref,
                    cos_ref, sin_ref, bd_ref, tri_ref,
                    v_ref, bon_ref, g_ref, at_ref, rt_ref, bt_ref, kt_ref, wc_ref):
    i = pl.program_id(1)
    x = p_ref[0]
    row = lax.broadcasted_iota(jnp.int32, (RW_TR, 1), 0)
    t = i * RW_TR + row
    seq_first = (t == 0) | (t == SEQ)
    seq_last = (t == SEQ - 1) | (t == T_ALL - 1)
    prev = jnp.where(row == 0, pv_ref[0, 7:8, :], pltpu.roll(x, 1, 0))
    prev = jnp.where(seq_first, 0.0, prev)
    nxt = jnp.where(row == RW_TR - 1, nx_ref[0, 0:1, :], pltpu.roll(x, RW_TR - 1, 0))
    nxt = jnp.where(seq_last, 0.0, nxt)
    xs = x + mu_ref[0:1, :] * (prev - x) + mu_ref[1:2, :] * (nxt - x)
    r = xs[:, 0:RW_W]
    k = xs[:, RW_W:2 * RW_W]
    v = xs[:, 2 * RW_W:3 * RW_W]
    i1 = 3 * RW_W
    wd = jnp.tanh(xs[:, i1:i1 + 2 * RW_LORA])
    ad = xs[:, i1 + 2 * RW_LORA:i1 + 4 * RW_LORA]
    gd = xs[:, i1 + 4 * RW_LORA:]
    cos, sin, bd = cos_ref[...], sin_ref[...], bd_ref[...]
    g_ref[0] = _dot3(jax.nn.sigmoid(gd), g2_ref)
    kk = k * kk_ref[...]
    nrm = jnp.sqrt(_dot01(kk * kk, bd))
    kk = kk / jnp.maximum(nrm, 1e-12)
    ksum = jnp.zeros_like(k)
    r_s = _rope(r, cos, sin)
    na_s = -_rope(kk, cos, sin)
    n_chunks = RW_TR // RW_C
    for d in range(2):
        logit = w0_ref[d:d + 1, :] + _dot3(wd, w2_ref, (d,))
        logw = -jax.nn.softplus(-logit) - 0.5
        lw = -jnp.exp(logw)
        a = jax.nn.sigmoid(a0_ref[d:d + 1, :] + _dot3(ad, a2_ref, (d,)))
        kdir = k * (1.0 + (a - 1.0) * ka_ref[...])
        ksum = ksum + kdir
        cum = _dot01(lw, tri_ref[d], const_left=True)
        e_neg = jnp.exp(-cum)
        at_ref[d, 0] = na_s * jnp.exp(cum - lw)
        rt_ref[d, 0] = r_s * jnp.exp(cum)
        bt_ref[d, 0] = _rope(kk * a, cos, sin) * e_neg
        kt_ref[d, 0] = _rope(kdir, cos, sin) * e_neg
        tot = jnp.sum(lw.reshape(n_chunks, RW_C, RW_W), axis=1, keepdims=True)
        wc_ref[d, 0] = jnp.broadcast_to(jnp.exp(tot), (n_chunks, 8, RW_W))
    bon_ref[0] = _dot01(r * ksum * rk_ref[...], bd)
    v_ref[0] = v


def _bmm3(spec, a, b):
    (ah, am), (bh, bm) = a, b
    e = functools.partial(jnp.einsum, spec, preferred_element_type=F32)
    return e(ah, bh) + e(am, bh) + e(ah, bm)


def _rw_scan_kernel(vf_ref, vb_ref, atf_ref, atb_ref, rtf_ref, rtb_ref, btf_ref, btb_ref, ktf_ref, ktb_ref,
                    wcf_ref, wcb_ref, yf_ref, yb_ref, s_sc):
    step = pl.program_id(0)
    nb = BATCH * RW_HEADS
    c = RW_C

    @pl.when(step == 0)
    def _():
        s_sc[...] = jnp.zeros_like(s_sc)

    def heads(xf, xb):
        return jnp.stack([x[b][:, h * RW_HD:(h + 1) * RW_HD]
                          for x in (xf, xb) for b in range(BATCH) for h in range(RW_HEADS)])

    v = heads(vf_ref[...], vb_ref[...])
    ar = jnp.concatenate([heads(atf_ref[0], atb_ref[0]), heads(rtf_ref[0], rtb_ref[0])], axis=1)
    bt = heads(btf_ref[0], btb_ref[0])
    kt = heads(ktf_ref[0], ktb_ref[0])
    wc = heads(wcf_ref[0, :, 0], wcb_ref[0, :, 0])[:, 0:1, :]

    ti = lax.broadcasted_iota(jnp.int32, (c, c), 0)
    si = lax.broadcasted_iota(jnp.int32, (c, c), 1)
    sign = jnp.where(lax.broadcasted_iota(jnp.int32, (2 * nb, 1, 1), 0) < nb, 1, -1)
    ahead = (ti - si)[None] * sign
    strict = ahead > 0
    incl = ahead >= 0
    eye = (si == ti).astype(F32)[None]

    ar_s, bt_s, kt_s, v_s = _split(ar), _split(bt), _split(kt), _split(v)
    lb = _bmm3('htj,hsj->hts', ar_s, bt_s)
    lk = _bmm3('htj,hsj->hts', ar_s, kt_s)
    lab = jnp.where(strict, lb[:, :c], 0.0)
    lrb = jnp.where(incl, lb[:, c:], 0.0)
    lak = jnp.where(strict, lk[:, :c], 0.0)
    lrk = jnp.where(incl, lk[:, c:], 0.0)

    tt = eye + lab
    pw = lab
    for _ in range(int(math.log2(c)) - 1):
        pw_s = _split(pw)
        pw = _bmm3('hts,hsu->htu', pw_s, pw_s)
        tt = tt + _bmm3('hts,hsu->htu', _split(tt), _split(pw))

    s0 = s_sc[...]
    ars = _bmm3('htk,hvk->htv', ar_s, _split(s0))
    lv = _bmm3('hts,hsv->htv', _split(jnp.concatenate([lak, lrk], axis=1)), v_s)
    u = _bmm3('hts,hsv->htv', _split(tt), _split(lv[:, :c] + ars[:, :c]))
    y = ars[:, c:] + _bmm3('hts,hsv->htv', _split(lrb), _split(u)) + lv[:, c:]
    for d, y_ref in enumerate((yf_ref, yb_ref)):
        for b in range(BATCH):
            y_ref[b] = jnp.concatenate([y[(d * BATCH + b) * RW_HEADS + h] for h in range(RW_HEADS)], axis=-1)
    uv = jnp.concatenate([u, v], axis=1)
    bk = jnp.concatenate([bt, kt], axis=1)
    s_sc[...] = (s0 + _bmm3('htv,htj->hvj', _split(uv), _split(bk))) * wc


def _rw_out_kernel(yf_ref, yb_ref, bon_ref, v_ref, g_ref, lnw_ref, lnb_ref, bd_ref, o_ref):
    y = yf_ref[0] + yb_ref[0]
    bd = bd_ref[...]
    mean = _dot01(y, bd) * (1.0 / RW_HD)
    yc = y - mean
    var = _dot01(yc * yc, bd) * (1.0 / RW_HD)
    yn = yc * lax.rsqrt(var + RW_GN_EPS) * lnw_ref[...] + lnb_ref[...]
    o_ref[0] = (yn + bon_ref[0] * v_ref[0]) * g_ref[0]


def _rw_chunk_index(d, s):
    n_lat = SEQ // RW_C
    n_all = T_ALL // RW_C
    fwd = jnp.where(s < n_all - n_lat, n_lat + s, s - (n_all - n_lat))
    return jnp.where(d == 0, fwd, n_all - 1 - s)


def _rwkv(p_rw, mu, w0, w2, a0, a2, g2, k_k, k_a, r_k, ln_w, ln_b):
    cos, sin = (jnp.asarray(a) for a in _rope_tables())
    bd = jnp.asarray(_head_blockdiag()).astype(BF16)
    zpad = lambda w: jnp.stack([jnp.concatenate([w[0], jnp.zeros_like(w[0])], axis=0),
                                jnp.concatenate([jnp.zeros_like(w[1]), w[1]], axis=0)])
    split = lambda w: jnp.stack(_split(w), axis=1)
    nt = T_ALL // RW_TR
    n_ch = T_ALL // RW_C
    tok_shape = jax.ShapeDtypeStruct((BATCH, T_ALL, RW_W), F32)
    dir_shape = jax.ShapeDtypeStruct((2, BATCH, T_ALL, RW_W), F32)
    dir_spec = pl.BlockSpec((2, 1, RW_TR, RW_W), lambda b, i: (0, b, i, 0))
    wc_shape = jax.ShapeDtypeStruct((2, BATCH, n_ch, 8, RW_W), F32)
    wc_spec = pl.BlockSpec((2, 1, RW_TR // RW_C, 8, RW_W), lambda b, i: (0, b, i, 0, 0))
    tok = lambda w: pl.BlockSpec((1, RW_TR, w), lambda b, i: (b, i, 0))
    const = lambda shape: pl.BlockSpec(shape, lambda b, i: (0,) * len(shape))
    row = lambda a: a.reshape(1, RW_W)
    outs = pl.pallas_call(
        _rw_prep_kernel,
        grid=(BATCH, nt),
        in_specs=[tok(RW_IN),
                  pl.BlockSpec((1, 8, RW_IN), lambda b, i: (b, jnp.maximum(i * (RW_TR // 8) - 1, 0), 0)),
                  pl.BlockSpec((1, 8, RW_IN), lambda b, i: (b, jnp.minimum((i + 1) * (RW_TR // 8), T_ALL // 8 - 1), 0)),
                  const((2, RW_IN)), const((2, RW_W)), const((2, 2, 2 * RW_LORA, RW_W)), const((2, RW_W)),
                  const((2, 2, 2 * RW_LORA, RW_W)), const((2, RW_GATE_LORA, RW_W)),
                  const((1, RW_W)), const((1, RW_W)), const((1, RW_W)),
                  pl.BlockSpec((RW_TR, RW_W), lambda b, i: (i, 0)), pl.BlockSpec((RW_TR, RW_W), lambda b, i: (i, 0)),
                  const((RW_W, RW_W)), const((2, RW_TR, RW_TR))],
        out_specs=[tok(RW_W)] * 3 + [dir_spec] * 4 + [wc_spec],
        out_shape=[tok_shape] * 3 + [dir_shape] * 4 + [wc_shape],
        compiler_params=_cparams(("parallel", "parallel")),
        name="rwkv_prepare",
    )(p_rw, p_rw, p_rw, mu, w0, split(zpad(w2)), a0, split(zpad(a2)), jnp.stack(_split(g2)),
      row(k_k), row(k_a), row(r_k.reshape(-1)), cos, sin, bd, jnp.asarray(_chunk_tri()).astype(BF16))
    v, bonus, g, at, rt, bt, kt, wc = outs

    shared = lambda d: pl.BlockSpec((BATCH, RW_C, RW_W), lambda s: (0, _rw_chunk_index(d, s), 0))
    perdir = lambda d: pl.BlockSpec((1, BATCH, RW_C, RW_W), lambda s: (d, 0, _rw_chunk_index(d, s), 0))
    wcdir = lambda d: pl.BlockSpec((1, BATCH, 1, 8, RW_W), lambda s: (d, 0, _rw_chunk_index(d, s), 0, 0))
    yf, yb = pl.pallas_call(
        _rw_scan_kernel,
        grid=(n_ch,),
        in_specs=[shared(0), shared(1)] + [perdir(0), perdir(1)] * 4 + [wcdir(0), wcdir(1)],
        out_specs=[shared(0), shared(1)],
        out_shape=[tok_shape, tok_shape],
        scratch_shapes=[pltpu.VMEM((2 * BATCH * RW_HEADS, RW_HD, RW_HD), F32)],
        compiler_params=_cparams(("arbitrary",)),
        name="rwkv_chunk_scan",
    )(v, v, at, at, rt, rt, bt, bt, kt, kt, wc, wc)

    tok2 = lambda w: pl.BlockSpec((1, PROJ_TM, w), lambda b, i: (b, i, 0))
    return pl.pallas_call(
        _rw_out_kernel,
        grid=(BATCH, T_ALL // PROJ_TM),
        in_specs=[tok2(RW_W), tok2(RW_W), tok2(RW_W), tok2(RW_W), tok2(RW_W),
                  const((1, RW_W)), const((1, RW_W)), const((RW_W, RW_W))],
        out_specs=tok2(RW_W),
        out_shape=jax.ShapeDtypeStruct((BATCH, T_ALL, RW_W), F32),
        compiler_params=_cparams(("parallel", "parallel")),
        name="rwkv_output",
    )(yf, yb, bonus, v, g, row(ln_w), row(ln_b), bd)


@functools.lru_cache(maxsize=None)
def _na_bias_tables():
    qc = np.arange(GRID_W)[:, None]
    kc = np.arange(GRID_W)[None, :]
    win0 = np.clip(qc - WIN_COLS // 2, 0, GRID_W - WIN_COLS)
    valid = (kc >= win0) & (kc < win0 + WIN_COLS)
    off = np.clip(kc - qc + WIN_COLS - 1, 0, 2 * WIN_COLS - 2)
    onehot = np.zeros((128, GRID_W * GRID_W), np.float32)
    onehot[off.reshape(-1), np.arange(GRID_W * GRID_W)] = valid.reshape(-1)
    mask = np.where(valid, 0.0, NEG_INF).astype(np.float32).reshape(1, GRID_W * GRID_W)
    return onehot, mask


def _na_bias_kernel(rpb_ref, oh_ref, mask_ref, o_ref):
    o_ref[...] = _dot(rpb_ref[...], oh_ref[...]) + mask_ref[...]


def _na_lat_kernel(q_ref, k_ref, v_ref, tp_ref, o_ref):
    for g in range(NA_G):
        _na_lat_row(pl.program_id(1) * NA_G + g, g, q_ref, k_ref, v_ref, tp_ref, o_ref)


def _na_lat_row(i, g, q_ref, k_ref, v_ref, tp_ref, o_ref):
    start = jnp.clip(i - WIN_ROWS // 2, 0, N_ROWS - WIN_ROWS)
    dr0 = start - i + WIN_ROWS - 1
    scale = NA_HD ** -0.5
    lane = lax.broadcasted_iota(jnp.int32, (1, NA_W), 1)
    hmask = [(lane // NA_HD) == h for h in range(NA_HEADS)]
    q = q_ref[0, g * GRID_W:(g + 1) * GRID_W, :]
    qm = jnp.concatenate([jnp.where(hmask[h], q, jnp.zeros_like(q)) for h in range(NA_HEADS)], axis=0)
    row0 = pl.multiple_of(start * GRID_W, GRID_W)
    kl = k_ref[0, pl.ds(row0, WIN_ROWS * GRID_W), :]
    vl = v_ref[0, pl.ds(row0, WIN_ROWS * GRID_W), :]
    kc = k_ref[0, SEQ:T_ALL, :]
    vc = v_ref[0, SEQ:T_ALL, :]
    nt = (((1,), (1,)), ((), ()))
    bias = jnp.concatenate(
        [jnp.concatenate([tp_ref[h, pl.ds(dr0 + 2 * j, 1)][0] for j in range(WIN_ROWS // 2)], axis=1)
         for h in range(NA_HEADS)], axis=0)
    s_loc = lax.dot_general(qm, kl, nt, preferred_element_type=F32) * scale + bias
    s_ctx = lax.dot_general(qm, kc, nt, preferred_element_type=F32) * scale
    m = jnp.maximum(jnp.max(s_loc, axis=-1, keepdims=True), jnp.max(s_ctx, axis=-1, keepdims=True))
    p_loc = jnp.exp(s_loc - m)
    p_ctx = jnp.exp(s_ctx - m)
    inv = 1.0 / (jnp.sum(p_loc, axis=-1, keepdims=True) + jnp.sum(p_ctx, axis=-1, keepdims=True))
    o_all = (jnp.dot((p_loc * inv).astype(BF16), vl, preferred_element_type=F32)
             + jnp.dot((p_ctx * inv).astype(BF16), vc, preferred_element_type=F32))
    o = jnp.zeros((GRID_W, NA_W), F32)
    for h in range(NA_HEADS):
        o = o + jnp.where(hmask[h], o_all[h * GRID_W:(h + 1) * GRID_W, :], 0.0)
    o_ref[0, g * GRID_W:(g + 1) * GRID_W, :] = o


def _na_ctx_kernel(q_ref, k_ref, v_ref, o_ref):
    scale = NA_HD ** -0.5
    lane = lax.broadcasted_iota(jnp.int32, (1, NA_W), 1)
    hmask = [(lane // NA_HD) == h for h in range(NA_HEADS)]
    q, k, v = q_ref[0], k_ref[0], v_ref[0]
    qm = jnp.concatenate([jnp.where(hmask[h], q, jnp.zeros_like(q)) for h in range(NA_HEADS)], axis=0)
    s = lax.dot_general(qm, k, (((1,), (1,)), ((), ())), preferred_element_type=F32) * scale
    p = jnp.exp(s - jnp.max(s, axis=-1, keepdims=True))
    p = p / jnp.sum(p, axis=-1, keepdims=True)
    o_all = jnp.dot(p.astype(BF16), v, preferred_element_type=F32)
    o = jnp.zeros((CTX_LEN, NA_W), F32)
    for h in range(NA_HEADS):
        o = o + jnp.where(hmask[h], o_all[h * CTX_LEN:(h + 1) * CTX_LEN, :], 0.0)
    o_ref[0] = o


def _natten(p_na, rpb):
    onehot, mask = (jnp.asarray(a) for a in _na_bias_tables())
    n_dr = 2 * WIN_ROWS - 1
    n_rel = 2 * WIN_COLS - 1
    rows = NA_HEADS * n_dr
    rows_p = -(-rows // 8) * 8
    rpb2 = jnp.zeros((rows_p, 128), F32).at[:rows, :n_rel].set(rpb.reshape(rows, n_rel))
    tb = pl.pallas_call(
        _na_bias_kernel,
        out_shape=jax.ShapeDtypeStruct((rows_p, GRID_W * GRID_W), F32),
        name="natten_bias_expand",
    )(rpb2, onehot, mask)
    tb = tb[:rows].reshape(NA_HEADS, n_dr, GRID_W, GRID_W)
    nxt = jnp.concatenate([tb[:, 1:], jnp.full_like(tb[:, :1], NEG_INF)], axis=1)
    tp = jnp.concatenate([tb, nxt], axis=-1)

    o_lat = pl.pallas_call(
        _na_lat_kernel,
        grid=(BATCH, N_ROWS // NA_G),
        in_specs=[pl.BlockSpec((1, NA_G * GRID_W, NA_W), lambda b, i: (b, i, 0)),
                  pl.BlockSpec((1, T_ALL, NA_W), lambda b, i: (b, 0, 1)),
                  pl.BlockSpec((1, T_ALL, NA_W), lambda b, i: (b, 0, 2)),
                  pl.BlockSpec((NA_HEADS, n_dr, GRID_W, 2 * GRID_W), lambda b, i: (0, 0, 0, 0))],
        out_specs=pl.BlockSpec((1, NA_G * GRID_W, NA_W), lambda b, i: (b, i, 0)),
        out_shape=jax.ShapeDtypeStruct((BATCH, SEQ, NA_W), F32),
        compiler_params=_cparams(("parallel", "parallel")),
        name="natten_latent",
    )(p_na, p_na, p_na, tp)

    cblk = SEQ // CTX_LEN
    o_ctx = pl.pallas_call(
        _na_ctx_kernel,
        grid=(BATCH,),
        in_specs=[pl.BlockSpec((1, CTX_LEN, NA_W), lambda b: (b, cblk, 0)),
                  pl.BlockSpec((1, CTX_LEN, NA_W), lambda b: (b, cblk, 1)),
                  pl.BlockSpec((1, CTX_LEN, NA_W), lambda b: (b, cblk, 2))],
        out_specs=pl.BlockSpec((1, CTX_LEN, NA_W), lambda b: (b, 0, 0)),
        out_shape=jax.ShapeDtypeStruct((BATCH, CTX_LEN, NA_W), F32),
        compiler_params=_cparams(("parallel",)),
        name="natten_context",
    )(p_na, p_na, p_na)
    return jnp.concatenate([o_lat, o_ctx], axis=1)


def kernel(x, c, ctx, c_ctx, mod_w, mod_b, norm_g, ffn1_wgu, ffn1_wdn, ffn2_wgu, ffn2_wdn, w_in, w_out, hy_conv_w, hy_conv_b, hy_f_w1, hy_f_b1, hy_f_w2, hy_f_b2, hy_f_w3, hy_freq, hy_bias, rw_mu, rw_w0, rw_w2, rw_a0, rw_a2, rw_g2, rw_k_k, rw_k_a, rw_r_k, rw_ln_w, rw_ln_b, na_rpb):
    assert x.shape == (BATCH, SEQ, D_MODEL) and ctx.shape == (BATCH, CTX_LEN, D_MODEL)
    mods = _modulation(c, c_ctx, mod_w, mod_b)
    h = jnp.concatenate([x, ctx], axis=1)
    gpad = jnp.concatenate([norm_g, jnp.zeros((DEPTH, 2, D_MODEL), F32)], axis=1)
    for l in range(DEPTH):
        m_l, g_l = mods[l], gpad[l]
        h = _ffn(h, m_l, g_l, ffn1_wgu[l].astype(BF16), ffn1_wdn[l].astype(BF16), 0)
        p_hy, p_rw, p_na = _in_proj(h, m_l, g_l, w_in[l].astype(BF16))
        o_hy = _hyena(p_hy, hy_conv_w[l], hy_conv_b[l], hy_f_w1[l], hy_f_b1[l], hy_f_w2[l], hy_f_b2[l],
                      hy_f_w3[l], hy_freq[l], hy_bias[l])
        o_rw = _rwkv(p_rw, rw_mu[l], rw_w0[l], rw_w2[l], rw_a0[l], rw_a2[l], rw_g2[l], rw_k_k[l], rw_k_a[l],
                     rw_r_k[l], rw_ln_w[l], rw_ln_b[l])
        o_na = _natten(p_na, na_rpb[l])
        h = _out_proj(h, o_hy, o_rw, o_na, m_l, g_l, w_out[l].astype(BF16))
        h = _ffn(h, m_l, g_l, ffn2_wgu[l].astype(BF16), ffn2_wdn[l].astype(BF16), 1)
    return h[:, :SEQ]
```
